```python
import jax, jax.numpy as jnp
from jax import lax
import numpy as np

D_MODEL = 2048
BATCH = 2
SEQ = 16384
DEPTH = 2

CHUNK = 64
Q_BLOCK = 128
HEAD_DIM = 128
D_MIX = D_MODEL
RET_HEADS = D_MIX // 4 // HEAD_DIM
FOX_HEADS = D_MIX // 2 // HEAD_DIM
MLSTM_HEADS = D_MIX // 4 // HEAD_DIM
RET_W = RET_HEADS * HEAD_DIM
FOX_W = FOX_HEADS * HEAD_DIM
MLSTM_W = MLSTM_HEADS * HEAD_DIM
CONV_WIDTH = 4
D_FF = 4 * D_MODEL
ROPE_BASE = 10000.0
NORM_EPS = 1e-6
N_MOD = 6
IN_SIZES = [RET_W] * 4 + [FOX_W] * 3 + [FOX_HEADS] + [MLSTM_W] * 4 + [MLSTM_HEADS, MLSTM_HEADS]
D_IN = sum(IN_SIZES)
IN_SPLIT_IDX = [int(i) for i in np.cumsum(IN_SIZES)[:-1]]

kernel_name = "hymba_retention_fox_mlstm_trunk"


def rmsnorm(x, w):
    x32 = x.astype(jnp.float32)
    y = x32 * lax.rsqrt(jnp.mean(x32 * x32, axis=-1, keepdims=True) + NORM_EPS)
    return (y * w.astype(jnp.float32)).astype(x.dtype)


def head_norm(y):
    B, S, H, d = y.shape
    y32 = y.astype(jnp.float32)
    mu = jnp.mean(y32, axis=-1, keepdims=True)
    var = jnp.mean(jnp.square(y32 - mu), axis=-1, keepdims=True)
    return ((y32 - mu) * lax.rsqrt(var + NORM_EPS)).reshape(B, S, H * d).astype(y.dtype)


def rotary(t, positions):
    d = t.shape[-1]
    inv_freq = ROPE_BASE ** (-jnp.arange(0, d, 2, dtype=jnp.float32) / d)
    ang = positions.astype(jnp.float32)[:, :, None] * inv_freq
    cos = jnp.cos(ang)[:, :, None, :]
    sin = jnp.sin(ang)[:, :, None, :]
    t32 = t.astype(jnp.float32)
    t1, t2 = t32[..., : d // 2], t32[..., d // 2:]
    return jnp.concatenate([t1 * cos - t2 * sin, t2 * cos + t1 * sin], axis=-1).astype(t.dtype)


def to_chunks(t):
    B, S, H, d = t.shape
    return t.reshape(B, S // CHUNK, CHUNK, H, d).transpose(1, 0, 3, 2, 4)


def from_chunks(t):
    NC, B, H, L, d = t.shape
    return t.transpose(1, 0, 3, 2, 4).reshape(B, NC * L, H, d)


def gate_chunks(g):
    B, S, H = g.shape
    return g.reshape(B, S // CHUNK, CHUNK, H).transpose(1, 0, 3, 2)


def causal_conv(x, w, b):
    K, C = w.shape
    y = lax.conv_general_dilated(x, w[:, None, :].astype(x.dtype), window_strides=(1,),
                                 padding=[(K - 1, 0)], dimension_numbers=('NWC', 'WIO', 'NWC'),
                                 feature_group_count=C)
    return y + b.astype(x.dtype)


def retention(q, k, v, positions):
    B, S, H, d = q.shape
    q = rotary(q, positions)
    k = rotary(k, positions) * (d ** -0.5)
    log_gamma = jnp.log(1.0 - 2.0 ** (-5.0 - jnp.arange(H, dtype=jnp.float32)))
    idx = jnp.arange(CHUNK, dtype=jnp.float32)
    rel = idx[:, None] - idx[None, :]
    intra_decay = jnp.where(rel >= 0, jnp.exp(jnp.maximum(rel, 0.0)[None] * log_gamma[:, None, None]), 0.0)
    q_decay = jnp.exp((idx + 1.0)[None, :] * log_gamma[:, None])[..., None]
    k_decay = jnp.exp((CHUNK - 1.0 - idx)[None, :] * log_gamma[:, None])[..., None]
    chunk_decay = jnp.exp(CHUNK * log_gamma)[:, None, None]

    def step(state, inp):
        qi, ki, vi = inp
        scores = jnp.einsum('bhld,bhmd->bhlm', qi, ki) * intra_decay
        intra = jnp.einsum('bhlm,bhmd->bhld', scores, vi)
        inter = jnp.einsum('bhld,bhde->bhle', qi * q_decay, state)
        new_state = state * chunk_decay + jnp.einsum('bhmd,bhme->bhde', ki * k_decay, vi)
        return new_state, intra + inter

    state0 = jnp.zeros((B, H, d, d), jnp.float32)
    _, out = lax.scan(step, state0, (to_chunks(q), to_chunks(k), to_chunks(v)))
    return from_chunks(out).astype(q.dtype)


def forgetting_attention(q, k, v, f_logit):
    B, S, H, d = q.shape
    cum = jnp.cumsum(jax.nn.log_sigmoid(f_logit.astype(jnp.float32)), axis=1).transpose(0, 2, 1)
    q_h = q.transpose(0, 2, 1, 3) * (d ** -0.5)
    k_h = k.transpose(0, 2, 1, 3)
    v_h = v.transpose(0, 2, 1, 3)
    nb = S // Q_BLOCK
    qb = q_h.reshape(B, H, nb, Q_BLOCK, d).transpose(2, 0, 1, 3, 4)
    cb = cum.reshape(B, H, nb, Q_BLOCK).transpose(2, 0, 1, 3)
    qpos = jnp.arange(S).reshape(nb, Q_BLOCK)
    kpos = jnp.arange(S)

    def block(inp):
        qi, ci, pi = inp
        logits = jnp.einsum('bhqd,bhkd->bhqk', qi, k_h).astype(jnp.float32) + ci[..., None] - cum[:, :, None, :]
        logits = jnp.where((pi[:, None] >= kpos[None, :])[None, None], logits, -jnp.inf)
        p = jax.nn.softmax(logits, axis=-1)
        return jnp.einsum('bhqk,bhkd->bhqd', p.astype(v_h.dtype), v_h)

    out = lax.map(block, (qb, cb, qpos))
    return out.transpose(1, 0, 3, 2, 4).reshape(B, S, H, d)


def mlstm(q, k, v, i_logit, f_logit):
    B, S, H, d = q.shape
    k = k * (d ** -0.5)
    ic = gate_chunks(i_logit.astype(jnp.float32))
    lfc = gate_chunks(jax.nn.log_sigmoid(f_logit.astype(jnp.float32)))
    causal = jnp.tril(jnp.ones((CHUNK, CHUNK), dtype=bool))

    def step(carry, inp):
        C, n, m = carry
        qi, ki, vi, ii, lfi = inp
        b = jnp.cumsum(lfi, axis=-1)
        D = jnp.where(causal, b[..., :, None] - b[..., None, :] + ii[..., None, :], -jnp.inf)
        inter_log = b + m[..., None]
        m_q = jnp.maximum(inter_log, jnp.max(D, axis=-1))
        w_intra = jnp.exp(D - m_q[..., None])
        w_inter = jnp.exp(inter_log - m_q)
        s = jnp.einsum('bhlk,bhmk->bhlm', qi, ki) * w_intra
        num = jnp.einsum('bhlm,bhmv->bhlv', s, vi) + w_inter[..., None] * jnp.einsum('bhlk,bhkv->bhlv', qi, C)
        den = jnp.sum(s, axis=-1) + w_inter * jnp.einsum('bhlk,bhk->bhl', qi, n)
        h = num / jnp.maximum(jnp.abs(den), jnp.exp(-m_q))[..., None]
        b_last = b[..., -1]
        k_log = b_last[..., None] - b + ii
        m_new = jnp.maximum(b_last + m, jnp.max(k_log, axis=-1))
        wk = jnp.exp(k_log - m_new[..., None])
        carry_scale = jnp.exp(b_last + m - m_new)
        C_new = carry_scale[..., None, None] * C + jnp.einsum('bhl,bhlk,bhlv->bhkv', wk, ki, vi)
        n_new = carry_scale[..., None] * n + jnp.einsum('bhl,bhlk->bhk', wk, ki)
        return (C_new, n_new, m_new), h

    carry0 = (jnp.zeros((B, H, d, d), jnp.float32), jnp.zeros((B, H, d), jnp.float32),
              jnp.zeros((B, H), jnp.float32))
    _, out = lax.scan(step, carry0, (to_chunks(q), to_chunks(k), to_chunks(v), ic, lfc))
    return from_chunks(out).astype(q.dtype)


def hybrid_mixer(h, positions, w_in, conv_w, conv_b, fox_f_bias, mlstm_i_bias, mlstm_f_bias,
                 merge_scale, w_out):
    B, S, _ = h.shape
    heads = lambda t, H: t.reshape(B, S, H, HEAD_DIM)
    proj = h @ w_in
    (rq, rk, rv, rg, fq, fk, fv, ff, mq, mk, mv, mo, mi, mf) = jnp.split(proj, IN_SPLIT_IDX, axis=-1)
    y_ret = head_norm(retention(heads(rq, RET_HEADS), heads(rk, RET_HEADS), heads(rv, RET_HEADS), positions))
    y_ret = y_ret * jax.nn.silu(rg)
    y_fox = head_norm(forgetting_attention(heads(fq, FOX_HEADS), heads(fk, FOX_HEADS), heads(fv, FOX_HEADS),
                                           ff + fox_f_bias))
    mqk = jax.nn.silu(causal_conv(jnp.concatenate([mq, mk], axis=-1), conv_w, conv_b))
    mq, mk = jnp.split(mqk, 2, axis=-1)
    y_m = head_norm(mlstm(heads(mq, MLSTM_HEADS), heads(mk, MLSTM_HEADS), heads(mv, MLSTM_HEADS),
                          mi + mlstm_i_bias, mf + mlstm_f_bias))
    y_m = y_m * jax.nn.sigmoid(mo)
    y = jnp.concatenate([y_ret, y_fox, y_m], axis=-1) * merge_scale
    return y @ w_out


def setup_inputs(seed: int = 0) -> dict:
    key = jax.random.key(seed)
    ks = jax.random.split(key, 20)
    nrm = lambda k, shape, scale: jax.random.normal(k, shape, jnp.float32) * scale
    x = nrm(ks[0], (BATCH, SEQ, D_MODEL), 1.0)
    c = nrm(ks[1], (BATCH, D_MODEL), 1.0)
    offsets = jax.random.randint(ks[2], (BATCH, 1), 0, 1024) * CHUNK
    positions = (offsets + jnp.arange(SEQ)[None, :]).astype(jnp.int32)
    return {
        "x": x,
        "c": c,
        "positions": positions,
        "ada_w": nrm(ks[3], (DEPTH, D_MODEL, N_MOD * D_MODEL), 0.5 * D_MODEL ** -0.5),
        "ada_b": nrm(ks[4], (DEPTH, N_MOD * D_MODEL), 0.02),
        "norm_mix_w": 1.0 + nrm(ks[5], (DEPTH, D_MODEL), 0.02),
        "norm_mlp_w": 1.0 + nrm(ks[6], (DEPTH, D_MODEL), 0.02),
        "w_in": nrm(ks[7], (DEPTH, D_MODEL, D_IN), D_MODEL ** -0.5),
        "conv_w": nrm(ks[8], (DEPTH, CONV_WIDTH, 2 * MLSTM_W), CONV_WIDTH ** -0.5),
        "conv_b": nrm(ks[9], (DEPTH, 2 * MLSTM_W), 0.02),
        "fox_f_bias": jax.random.uniform(ks[10], (DEPTH, FOX_HEADS), jnp.float32, 1.0, 4.0),
        "mlstm_i_bias": nrm(ks[11], (DEPTH, MLSTM_HEADS), 0.1),
        "mlstm_f_bias": jax.random.uniform(ks[12], (DEPTH, MLSTM_HEADS), jnp.float32, 3.0, 6.0),
        "merge_scale": 1.0 + nrm(ks[13], (DEPTH, D_MIX), 0.02),
        "w_out": nrm(ks[14], (DEPTH, D_MIX, D_MODEL), D_MIX ** -0.5),
        "w_ff1": nrm(ks[15], (DEPTH, D_MODEL, D_FF), D_MODEL ** -0.5),
        "w_ff2": nrm(ks[16], (DEPTH, D_FF, D_MODEL), D_FF ** -0.5),
        "final_norm_w": 1.0 + nrm(ks[17], (D_MODEL,), 0.02),
    }


def reference(x, c, positions, ada_w, ada_b, norm_mix_w, norm_mlp_w, w_in, conv_w, conv_b,
              fox_f_bias, mlstm_i_bias, mlstm_f_bias, merge_scale, w_out, w_ff1, w_ff2, final_norm_w):
    cond = jax.nn.silu(c)
    for layer in range(DEPTH):
        mod = (cond @ ada_w[layer] + ada_b[layer])[:, None, :]
        sh_a, sc_a, g_a, sh_m, sc_m, g_m = jnp.split(mod, N_MOD, axis=-1)
        h = rmsnorm(x, norm_mix_w[layer]) * (1.0 + sc_a) + sh_a
        x = x + g_a * hybrid_mixer(h, positions, w_in[layer], conv_w[layer], conv_b[layer],
                                   fox_f_bias[layer], mlstm_i_bias[layer], mlstm_f_bias[layer],
                                   merge_scale[layer], w_out[layer])
        h = rmsnorm(x, norm_mlp_w[layer]) * (1.0 + sc_m) + sh_m
        x = x + g_m * (jnp.square(jax.nn.relu(h @ w_ff1[layer])) @ w_ff2[layer])
    return rmsnorm(x, final_norm_w)
```

```python
import functools
import math

import jax
import jax.numpy as jnp
from jax import lax
from jax.experimental import pallas as pl
from jax.experimental.pallas import tpu as pltpu

HEAD_DIM = 128
CONV_WIDTH = 4
ROPE_BASE = 10000.0
NORM_EPS = 1e-6
N_MOD = 6
GATE_LANES = 128
CONV_HALO = 8

F32 = jnp.float32
BF16 = jnp.bfloat16

VMEM_LIMIT_BYTES = 56 * 1024 * 1024


def _params(*sem):
    return pltpu.CompilerParams(dimension_semantics=sem, vmem_limit_bytes=VMEM_LIMIT_BYTES)


def _dot(a, b):
    return jnp.dot(a, b, preferred_element_type=F32)


def _dot_nt(a, b):
    return lax.dot_general(a, b, (((1,), (1,)), ((), ())), preferred_element_type=F32)


def _dot_tn(a, b):
    return lax.dot_general(a, b, (((0,), (0,)), ((), ())), preferred_element_type=F32)


def _sigmoid(x):
    return 1.0 / (1.0 + jnp.exp(-x))


def _log_sigmoid(x):
    return jnp.minimum(x, 0.0) - jnp.log1p(jnp.exp(-jnp.abs(x)))


def _head_norm(y):
    mu = jnp.mean(y, axis=-1, keepdims=True)
    yc = y - mu
    var = jnp.mean(yc * yc, axis=-1, keepdims=True)
    return yc * lax.rsqrt(var + NORM_EPS)


def _ada_kernel(ct_ref, w_ref, b_ref, o_ref, *, kc):
    d, nb = ct_ref.shape
    tn = w_ref.shape[2]

    def body(i, accs):
        r0 = pl.multiple_of(i * kc, kc)
        ck = ct_ref[pl.ds(r0, kc), :]
        ck = ck * _sigmoid(ck)
        wk = w_ref[0, pl.ds(r0, kc), :]
        return tuple(acc + jnp.sum(wk * ck[:, b:b + 1], axis=0, keepdims=True)
                     for b, acc in enumerate(accs))

    accs = lax.fori_loop(0, d // kc, body, tuple(jnp.zeros((1, tn), F32) for _ in range(nb)))
    o_ref[0] = jnp.concatenate(accs, axis=0) + b_ref[0]


def _ada_mod(c, ada_w, ada_b):
    depth, d, n = ada_w.shape
    nb = c.shape[0]
    tn = min(1024, n)
    kc = min(256, d)
    return pl.pallas_call(
        functools.partial(_ada_kernel, kc=kc),
        grid=(depth, n // tn),
        in_specs=[pl.BlockSpec((d, nb), lambda l, j: (0, 0)),
                  pl.BlockSpec((1, d, tn), lambda l, j: (l, 0, j)),
                  pl.BlockSpec((1, 1, tn), lambda l, j: (l, 0, j))],
        out_specs=pl.BlockSpec((1, nb, tn), lambda l, j: (l, 0, j)),
        out_shape=jax.ShapeDtypeStruct((depth, nb, n), F32),
        compiler_params=_params("arbitrary", "arbitrary"),
        name="ada_mod",
    )(c.T, ada_w, ada_b.reshape(depth, 1, n))


def _rope_kernel(pos_ref, invf_ref, sign_ref, cos_ref, sin_ref):
    ang = pos_ref[0] * invf_ref[...]
    cos_ref[0] = jnp.cos(ang)
    sin_ref[0] = jnp.sin(ang) * sign_ref[...]


def _rope_tables(positions):
    b, s = positions.shape
    ts = min(2048, s)
    half = HEAD_DIM // 2
    inv_freq = ROPE_BASE ** (-jnp.arange(0, HEAD_DIM, 2, dtype=F32) / HEAD_DIM)
    invf = jnp.concatenate([inv_freq, inv_freq]).reshape(1, HEAD_DIM)
    sign = jnp.concatenate([-jnp.ones((half,), F32), jnp.ones((half,), F32)]).reshape(1, HEAD_DIM)
    pos = positions.astype(F32).reshape(b, s, 1)
    out = jax.ShapeDtypeStruct((b, s, HEAD_DIM), F32)
    return pl.pallas_call(
        _rope_kernel,
        grid=(b, s // ts),
        in_specs=[pl.BlockSpec((1, ts, 1), lambda i, j: (i, j, 0)),
                  pl.BlockSpec((1, HEAD_DIM), lambda i, j: (0, 0)),
                  pl.BlockSpec((1, HEAD_DIM), lambda i, j: (0, 0))],
        out_specs=[pl.BlockSpec((1, ts, HEAD_DIM), lambda i, j: (i, j, 0))] * 2,
        out_shape=[out, out],
        compiler_params=_params("arbitrary", "arbitrary"),
        name="rope_tables",
    )(pos, invf, sign)


def _modulated_norm(x, nw, sc, sh):
    ms = jnp.mean(x * x, axis=-1, keepdims=True)
    return (x * lax.rsqrt(ms + NORM_EPS) * nw) * (1.0 + sc) + sh


def _inproj_kernel(x_ref, nw_ref, sc_ref, sh_ref, w_ref, wg_ref, o_ref, g_ref, h_scr):
    @pl.when(pl.program_id(1) == 0)
    def _():
        hb = _modulated_norm(x_ref[...], nw_ref[...], sc_ref[0], sh_ref[0]).astype(BF16)
        h_scr[...] = hb
        g_ref[...] = _dot(hb, wg_ref[...])

    o_ref[...] = _dot(h_scr[...], w_ref[...]).astype(BF16)


def _inproj(x2, nw, sc, sh, w_main, w_gate, seq):
    m, d = x2.shape
    nm = w_main.shape[1]
    tm = min(1024, seq)
    tn = 1024 if nm % 1024 == 0 else 512
    bpr = seq // tm
    return pl.pallas_call(
        _inproj_kernel,
        grid=(m // tm, nm // tn),
        in_specs=[pl.BlockSpec((tm, d), lambda i, j: (i, 0)),
                  pl.BlockSpec((1, d), lambda i, j: (0, 0)),
                  pl.BlockSpec((1, 1, d), lambda i, j: (i // bpr, 0, 0)),
                  pl.BlockSpec((1, 1, d), lambda i, j: (i // bpr, 0, 0)),
                  pl.BlockSpec((d, tn), lambda i, j: (0, j)),
                  pl.BlockSpec((d, GATE_LANES), lambda i, j: (0, 0))],
        out_specs=[pl.BlockSpec((tm, tn), lambda i, j: (i, j)),
                   pl.BlockSpec((tm, GATE_LANES), lambda i, j: (i, 0))],
        out_shape=[jax.ShapeDtypeStruct((m, nm), BF16),
                   jax.ShapeDtypeStruct((m, GATE_LANES), F32)],
        scratch_shapes=[pltpu.VMEM((tm, d), BF16)],
        compiler_params=_params("arbitrary", "arbitrary"),
        name="inproj",
    )(x2, nw, sc, sh, w_main, w_gate)


def _gates_kernel(g_ref, bias_ref, cum_ref, o_ref, carry):
    @pl.when(pl.program_id(1) == 0)
    def _():
        carry[...] = jnp.zeros_like(carry)

    g = g_ref[0] + bias_ref[...]
    ls = _log_sigmoid(g)
    ts = ls.shape[0]
    row = lax.broadcasted_iota(jnp.int32, (ts, ts), 0)
    col = lax.broadcasted_iota(jnp.int32, (ts, ts), 1)
    tri = jnp.where(row >= col, 1.0, 0.0).astype(BF16)
    hi = ls.astype(BF16)
    r1 = ls - hi.astype(F32)
    mid = r1.astype(BF16)
    lo = (r1 - mid.astype(F32)).astype(BF16)
    csum = (_dot(tri, hi) + _dot(tri, mid)) + _dot(tri, lo) + carry[...]
    carry[...] = csum[ts - 1:ts, :]
    o_ref[0] = jnp.where(cum_ref[...] > 0.5, csum, g)


def _gates(g3, bias, cum_mask):
    b, s, n = g3.shape
    ts = min(512, s)
    return pl.pallas_call(
        _gates_kernel,
        grid=(b, s // ts),
        in_specs=[pl.BlockSpec((1, ts, n), lambda i, j: (i, j, 0)),
                  pl.BlockSpec((1, n), lambda i, j: (0, 0)),
                  pl.BlockSpec((1, n), lambda i, j: (0, 0))],
        out_specs=pl.BlockSpec((1, ts, n), lambda i, j: (i, j, 0)),
        out_shape=jax.ShapeDtypeStruct((b, s, n), F32),
        scratch_shapes=[pltpu.VMEM((1, n), F32)],
        compiler_params=_params("arbitrary", "arbitrary"),
        name="gates",
    )(g3, bias, cum_mask)


def _retention_kernel(lg_ref, q_ref, k_ref, v_ref, g_ref, cos_ref, sin_ref, ms_ref, o_ref, state):
    h = pl.program_id(1)

    @pl.when(pl.program_id(2) == 0)
    def _():
        state[...] = jnp.zeros_like(state)

    lg = lg_ref[h]
    cos = cos_ref[0]
    sin = sin_ref[0]
    n = cos.shape[0]
    half = HEAD_DIM // 2

    def rot(t):
        return t * cos + pltpu.roll(t, half, 1) * sin

    q = rot(q_ref[0].astype(F32))
    k = rot(k_ref[0].astype(F32)) * (HEAD_DIM ** -0.5)
    v = v_ref[0]

    row = lax.broadcasted_iota(jnp.int32, (n, n), 0)
    col = lax.broadcasted_iota(jnp.int32, (n, n), 1)
    rel = (row - col).astype(F32)
    intra_decay = jnp.where(rel >= 0, jnp.exp(jnp.maximum(rel, 0.0) * lg), 0.0)
    idx = lax.broadcasted_iota(jnp.int32, (n, 1), 0).astype(F32)
    q_decay = jnp.exp((idx + 1.0) * lg)
    k_decay = jnp.exp((n - 1.0 - idx) * lg)
    chunk_decay = jnp.exp(n * lg)

    qb = q.astype(BF16)
    scores = _dot_nt(qb, k.astype(BF16)) * intra_decay
    intra = _dot(scores.astype(BF16), v)
    st = state[...]
    inter = _dot((q * q_decay).astype(BF16), st.astype(BF16))
    state[...] = st * chunk_decay + _dot_tn((k * k_decay).astype(BF16), v)

    y = _head_norm(intra + inter)
    gate = g_ref[0].astype(F32)
    o_ref[0] = (y * (gate * _sigmoid(gate)) * ms_ref[...]).astype(BF16)


def _retention(proj, cos, sin, merge_scale, log_gamma, heads, col0, chunk):
    b, s, _ = proj.shape
    hb = lambda base: (lambda i, h, c, lg: (i, c, col0 + base * heads + h))
    tok = pl.BlockSpec((1, chunk, HEAD_DIM), lambda i, h, c, lg: (i, c, 0))
    grid_spec = pltpu.PrefetchScalarGridSpec(
        num_scalar_prefetch=1,
        grid=(b, heads, s // chunk),
        in_specs=[pl.BlockSpec((1, chunk, HEAD_DIM), hb(0)),
                  pl.BlockSpec((1, chunk, HEAD_DIM), hb(1)),
                  pl.BlockSpec((1, chunk, HEAD_DIM), hb(2)),
                  pl.BlockSpec((1, chunk, HEAD_DIM), hb(3)),
                  tok, tok,
                  pl.BlockSpec((1, HEAD_DIM), lambda i, h, c, lg: (0, h))],
        out_specs=pl.BlockSpec((1, chunk, HEAD_DIM), lambda i, h, c, lg: (i, c, h)),
        scratch_shapes=[pltpu.VMEM((HEAD_DIM, HEAD_DIM), F32)],
    )
    return pl.pallas_call(
        _retention_kernel,
        grid_spec=grid_spec,
        out_shape=jax.ShapeDtypeStruct((b, s, heads * HEAD_DIM), BF16),
        compiler_params=_params("arbitrary", "arbitrary", "arbitrary"),
        name="retention",
    )(log_gamma, proj, proj, proj, proj, cos, sin, merge_scale)


def _fox_kernel(q_ref, k_ref, v_ref, c_ref, ms_ref, o_ref, m_scr, l_scr, acc_scr, *, tq):
    qi = pl.program_id(2)
    q = (q_ref[0].astype(F32) * (HEAD_DIM ** -0.5)).astype(BF16)
    q0 = pl.multiple_of(qi * tq, tq)
    c_first = c_ref[0, :, pl.ds(q0, tq)][:, 0:1]

    m_scr[...] = jnp.full_like(m_scr, -jnp.inf)
    l_scr[...] = jnp.zeros_like(l_scr)
    acc_scr[...] = jnp.zeros_like(acc_scr)

    def step(k0, masked):
        k = k_ref[0, pl.ds(k0, tq), :]
        v = v_ref[0, pl.ds(k0, tq), :]
        ck = c_ref[0, :, pl.ds(k0, tq)] - c_first
        t = _dot_nt(q, k) - ck
        if masked:
            row = lax.broadcasted_iota(jnp.int32, (tq, tq), 0)
            col = lax.broadcasted_iota(jnp.int32, (tq, tq), 1)
            t = jnp.where(row >= col, t, -jnp.inf)
        m_prev = m_scr[...]
        m_new = jnp.maximum(m_prev, jnp.max(t, axis=-1, keepdims=True))
        p = jnp.exp(t - m_new[:, 0:1])
        alpha = jnp.exp(m_prev - m_new)
        l_scr[...] = alpha * l_scr[...] + jnp.sum(p, axis=-1, keepdims=True)
        acc_scr[...] = alpha * acc_scr[...] + _dot(p.astype(BF16), v)
        m_scr[...] = m_new

    def body(kb, carry):
        step(pl.multiple_of(kb * tq, tq), False)
        return carry

    lax.fori_loop(0, qi, body, 0)
    step(q0, True)

    out = acc_scr[...] / l_scr[...]
    o_ref[0] = (_head_norm(out) * ms_ref[...]).astype(BF16)


def _fox(proj, c_rows, merge_scale, heads, col0, ms_col0, tq):
    b, s, _ = proj.shape
    return pl.pallas_call(
        functools.partial(_fox_kernel, tq=tq),
        grid=(b, heads, s // tq),
        in_specs=[pl.BlockSpec((1, tq, HEAD_DIM), lambda i, h, q: (i, q, col0 + h)),
                  pl.BlockSpec((1, s, HEAD_DIM), lambda i, h, q: (i, 0, col0 + heads + h)),
                  pl.BlockSpec((1, s, HEAD_DIM), lambda i, h, q: (i, 0, col0 + 2 * heads + h)),
                  pl.BlockSpec((1, 1, s), lambda i, h, q: (i * heads + h, 0, 0)),
                  pl.BlockSpec((1, HEAD_DIM), lambda i, h, q: (0, ms_col0 + h))],
        out_specs=pl.BlockSpec((1, tq, HEAD_DIM), lambda i, h, q: (i, q, h)),
        out_shape=jax.ShapeDtypeStruct((b, s, heads * HEAD_DIM), BF16),
        scratch_shapes=[pltpu.VMEM((tq, HEAD_DIM), F32)] * 3,
        compiler_params=_params("arbitrary", "arbitrary", "arbitrary"),
        name="fox_attention",
    )(proj, proj, proj, c_rows, merge_scale)


def _mlstm_kernel(q_ref, k_ref, v_ref, og_ref, wq_ref, wk_ref, bq_ref, bk_ref, gp_ref, ir_ref, gr_ref,
                  ms_ref, o_ref, c_scr, n_scr, m_scr, g0_scr, xq_scr, xk_scr, *, i_lane0, f_lane0):
    h = pl.program_id(1)
    n = q_ref.shape[1]

    @pl.when(pl.program_id(2) == 0)
    def _():
        c_scr[...] = jnp.zeros_like(c_scr)
        n_scr[...] = jnp.zeros_like(n_scr)
        m_scr[...] = jnp.zeros_like(m_scr)
        g0_scr[...] = jnp.zeros_like(g0_scr)
        xq_scr[0:CONV_HALO, :] = jnp.zeros((CONV_HALO, HEAD_DIM), F32)
        xk_scr[0:CONV_HALO, :] = jnp.zeros((CONV_HALO, HEAD_DIM), F32)

    def conv_silu(x_ref, x_scr, w_ref, b_ref):
        x_scr[CONV_HALO:CONV_HALO + n, :] = x_ref[0].astype(F32)
        y = b_ref[...] + jnp.zeros((n, HEAD_DIM), F32)
        for j in range(CONV_WIDTH):
            shift = CONV_WIDTH - 1 - j
            y = y + w_ref[j:j + 1, :] * x_scr[CONV_HALO - shift:CONV_HALO - shift + n, :]
        x_scr[0:CONV_HALO, :] = x_scr[n:n + CONV_HALO, :]
        return y * _sigmoid(y)

    q = conv_silu(q_ref, xq_scr, wq_ref, bq_ref)
    k = conv_silu(k_ref, xk_scr, wk_ref, bk_ref) * (HEAD_DIM ** -0.5)
    v = v_ref[0]

    gp = gp_ref[0]
    lane = lax.broadcasted_iota(jnp.int32, gp.shape, 1)
    ic = jnp.sum(jnp.where(lane == i_lane0 + h, gp, 0.0), axis=-1, keepdims=True)
    gc = jnp.sum(jnp.where(lane == f_lane0 + h, gp, 0.0), axis=-1, keepdims=True)
    ir = ir_ref[0]
    gr = gr_ref[0]

    g0 = g0_scr[:, 0:1]
    m_prev = m_scr[:, 0:1]
    b_col = gc - g0
    row = lax.broadcasted_iota(jnp.int32, (n, n), 0)
    col = lax.broadcasted_iota(jnp.int32, (n, n), 1)
    dmat = jnp.where(row >= col, gc + (ir - gr), -jnp.inf)
    inter_log = b_col + m_prev
    m_q = jnp.maximum(inter_log, jnp.max(dmat, axis=-1, keepdims=True))
    w_intra = jnp.exp(dmat - m_q)
    w_inter = jnp.exp(inter_log - m_q)

    qb = q.astype(BF16)
    s = _dot_nt(qb, k.astype(BF16)) * w_intra
    cst = c_scr[...]
    nst = n_scr[...]
    num = _dot(s.astype(BF16), v) + w_inter * _dot(qb, cst.astype(BF16))
    den = jnp.sum(s, axis=-1, keepdims=True) + w_inter * jnp.sum(q * nst, axis=-1, keepdims=True)
    hid = num / jnp.maximum(jnp.abs(den), jnp.exp(-m_q))

    b_last = b_col[n - 1:n, :]
    k_log = b_last - b_col + ic
    m_new = jnp.maximum(b_last + m_prev, jnp.max(k_log, axis=0, keepdims=True))
    wk = jnp.exp(k_log - m_new)
    carry_scale = jnp.exp(b_last + m_prev - m_new)
    kw = k * wk
    c_scr[...] = carry_scale * cst + _dot_tn(kw.astype(BF16), v)
    n_scr[...] = carry_scale * nst + jnp.sum(kw, axis=0, keepdims=True)
    m_scr[...] = jnp.broadcast_to(m_new, m_scr.shape)
    g0_scr[...] = jnp.broadcast_to(gc[n - 1:n, :], g0_scr.shape)

    og = og_ref[0].astype(F32)
    o_ref[0] = (_head_norm(hid) * _sigmoid(og) * ms_ref[...]).astype(BF16)


def _mlstm(proj, conv_w, conv_b, gp, i_rows, g_rows, merge_scale, heads, col0, ms_col0, i_lane0, f_lane0, chunk):
    b, s, _ = proj.shape
    hb = lambda base: (lambda i, h, c: (i, c, col0 + base * heads + h))
    row_spec = pl.BlockSpec((1, 1, chunk), lambda i, h, c: (i * heads + h, 0, c))
    vec = lambda base: pl.BlockSpec((1, HEAD_DIM), lambda i, h, c: (0, base + h))
    return pl.pallas_call(
        functools.partial(_mlstm_kernel, i_lane0=i_lane0, f_lane0=f_lane0),
        grid=(b, heads, s // chunk),
        in_specs=[pl.BlockSpec((1, chunk, HEAD_DIM), hb(0)),
                  pl.BlockSpec((1, chunk, HEAD_DIM), hb(1)),
                  pl.BlockSpec((1, chunk, HEAD_DIM), hb(2)),
                  pl.BlockSpec((1, chunk, HEAD_DIM), hb(3)),
                  pl.BlockSpec((CONV_WIDTH, HEAD_DIM), lambda i, h, c: (0, h)),
                  pl.BlockSpec((CONV_WIDTH, HEAD_DIM), lambda i, h, c: (0, heads + h)),
                  vec(0), vec(heads),
                  pl.BlockSpec((1, chunk, GATE_LANES), lambda i, h, c: (i, c, 0)),
                  row_spec, row_spec,
                  vec(ms_col0)],
        out_specs=pl.BlockSpec((1, chunk, HEAD_DIM), lambda i, h, c: (i, c, h)),
        out_shape=jax.ShapeDtypeStruct((b, s, heads * HEAD_DIM), BF16),
        scratch_shapes=[pltpu.VMEM((HEAD_DIM, HEAD_DIM), F32),
                        pltpu.VMEM((1, HEAD_DIM), F32),
                        pltpu.VMEM((1, HEAD_DIM), F32),
                        pltpu.VMEM((1, HEAD_DIM), F32),
                        pltpu.VMEM((chunk + CONV_HALO, HEAD_DIM), F32),
                        pltpu.VMEM((chunk + CONV_HALO, HEAD_DIM), F32)],
        compiler_params=_params("arbitrary", "arbitrary", "arbitrary"),
        name="mlstm",
    )(proj, proj, proj, proj, conv_w, conv_w, conv_b, conv_b, gp, i_rows, g_rows, merge_scale)


def _outproj_kernel(yr_ref, yf_ref, ym_ref, w_ref, x_ref, ga_ref, o_ref):
    r = yr_ref.shape[1]
    f = yf_ref.shape[1]
    y = _dot(yr_ref[...], w_ref[0:r, :]) + _dot(yf_ref[...], w_ref[r:r + f, :])
    y = y + _dot(ym_ref[...], w_ref[r + f:, :])
    o_ref[...] = x_ref[...] + ga_ref[0] * y


def _outproj(y_ret, y_fox, y_m, w_out, x2, g_a, seq):
    m, d = x2.shape
    tm = min(512, seq)
    bpr = seq // tm
    rows = lambda w: pl.BlockSpec((tm, w), lambda i: (i, 0))
    return pl.pallas_call(
        _outproj_kernel,
        grid=(m // tm,),
        in_specs=[rows(y_ret.shape[1]), rows(y_fox.shape[1]), rows(y_m.shape[1]),
                  pl.BlockSpec(w_out.shape, lambda i: (0, 0)),
                  rows(d),
                  pl.BlockSpec((1, 1, d), lambda i: (i // bpr, 0, 0))],
        out_specs=rows(d),
        out_shape=jax.ShapeDtypeStruct((m, d), F32),
        compiler_params=_params("arbitrary"),
        name="outproj",
    )(y_ret, y_fox, y_m, w_out, x2, g_a)


def _mlp_kernel(x_ref, nw_ref, sc_ref, sh_ref, gm_ref, w1_ref, w2_ref, fw_ref, o_ref, h_scr, acc_scr,
                *, final_norm):
    f = pl.program_id(1)

    @pl.when(f == 0)
    def _():
        h_scr[...] = _modulated_norm(x_ref[...], nw_ref[...], sc_ref[0], sh_ref[0]).astype(BF16)
        acc_scr[...] = jnp.zeros_like(acc_scr)

    a = jnp.maximum(_dot(h_scr[...], w1_ref[...]), 0.0)
    acc_scr[...] += _dot((a * a).astype(BF16), w2_ref[...])

    @pl.when(f == pl.num_programs(1) - 1)
    def _():
        y = x_ref[...] + gm_ref[0] * acc_scr[...]
        if final_norm:
            ms = jnp.mean(y * y, axis=-1, keepdims=True)
            y = y * lax.rsqrt(ms + NORM_EPS) * fw_ref[...]
        o_ref[...] = y


def _mlp(x2, nw, sc, sh, g_m, w1, w2, final_w, seq, final_norm):
    m, d = x2.shape
    dff = w1.shape[1]
    tm = min(512, seq)
    tf = min(512, dff)
    bpr = seq // tm
    mod = pl.BlockSpec((1, 1, d), lambda i, f: (i // bpr, 0, 0))
    vec = pl.BlockSpec((1, d), lambda i, f: (0, 0))
    return pl.pallas_call(
        functools.partial(_mlp_kernel, final_norm=final_norm),
        grid=(m // tm, dff // tf),
        in_specs=[pl.BlockSpec((tm, d), lambda i, f: (i, 0)),
                  vec, mod, mod, mod,
                  pl.BlockSpec((d, tf), lambda i, f: (0, f)),
                  pl.BlockSpec((tf, d), lambda i, f: (f, 0)),
                  vec],
        out_specs=pl.BlockSpec((tm, d), lambda i, f: (i, 0)),
        out_shape=jax.ShapeDtypeStruct((m, d), F32),
        scratch_shapes=[pltpu.VMEM((tm, d), BF16), pltpu.VMEM((tm, d), F32)],
        compiler_params=_params("arbitrary", "arbitrary"),
        name="mlp",
    )(x2, nw, sc, sh, g_m, w1, w2, final_w)


def kernel(x, c, positions, ada_w, ada_b, norm_mix_w, norm_mlp_w, w_in, conv_w, conv_b, fox_f_bias,
           mlstm_i_bias, mlstm_f_bias, merge_scale, w_out, w_ff1, w_ff2, final_norm_w):
    b, s, d = x.shape
    depth = ada_w.shape[0]
    ret_w, fox_w, ml_w = d // 4, d // 2, d // 4
    ret_h, fox_h, ml_h = ret_w // HEAD_DIM, fox_w // HEAD_DIM, ml_w // HEAD_DIM
    assert s % 128 == 0 and d % (4 * HEAD_DIM) == 0
    assert fox_h + 2 * ml_h <= GATE_LANES

    sizes = [ret_w] * 4 + [fox_w] * 3 + [fox_h] + [ml_w] * 4 + [ml_h, ml_h]
    offs = [0]
    for sz in sizes:
        offs.append(offs[-1] + sz)
    main_cols = jnp.concatenate([jnp.arange(offs[0], offs[7]), jnp.arange(offs[8], offs[12])])
    gate_cols = jnp.concatenate([jnp.arange(offs[7], offs[8]), jnp.arange(offs[12], offs[14])])
    n_gate = fox_h + 2 * ml_h
    ret_col0 = 0
    fox_col0 = (4 * ret_w) // HEAD_DIM
    ml_col0 = (4 * ret_w + 3 * fox_w) // HEAD_DIM
    i_lane0, f_lane0 = fox_h, fox_h + ml_h

    chunk = min(256, s)
    tq = min(512, s)

    mod = _ada_mod(c, ada_w, ada_b)
    cos, sin = _rope_tables(positions)
    log_gamma = jnp.log(1.0 - 2.0 ** (-5.0 - jnp.arange(ret_h, dtype=F32)))
    cum_mask = jnp.zeros((1, GATE_LANES), F32).at[0, :fox_h].set(1.0).at[0, f_lane0:f_lane0 + ml_h].set(1.0)

    x2 = x.reshape(b * s, d)
    for layer in range(depth):
        sh_a, sc_a, g_a, sh_m, sc_m, g_m = [t.reshape(b, 1, d) for t in jnp.split(mod[layer], N_MOD, axis=-1)]
        wl = w_in[layer]
        w_main = wl[:, main_cols].astype(BF16)
        w_gate = jnp.pad(wl[:, gate_cols], ((0, 0), (0, GATE_LANES - n_gate))).astype(BF16)
        gate_bias = jnp.pad(jnp.concatenate([fox_f_bias[layer], mlstm_i_bias[layer], mlstm_f_bias[layer]]),
                            (0, GATE_LANES - n_gate)).reshape(1, GATE_LANES)
        ms = merge_scale[layer].reshape(1, d)

        proj, gates = _inproj(x2, norm_mix_w[layer].reshape(1, d), sc_a, sh_a, w_main, w_gate, s)
        proj = proj.reshape(b, s, -1)
        gp = _gates(gates.reshape(b, s, GATE_LANES), gate_bias, cum_mask)
        gp_t = jnp.swapaxes(gp[:, :, :n_gate], 1, 2)
        c_rows = gp_t[:, :fox_h].reshape(b * fox_h, 1, s)
        i_rows = gp_t[:, i_lane0:i_lane0 + ml_h].reshape(b * ml_h, 1, s)
        g_rows = gp_t[:, f_lane0:f_lane0 + ml_h].reshape(b * ml_h, 1, s)

        y_ret = _retention(proj, cos, sin, ms, log_gamma, ret_h, ret_col0, chunk)
        y_fox = _fox(proj, c_rows, ms, fox_h, fox_col0, ret_h, tq)
        y_m = _mlstm(proj, conv_w[layer], conv_b[layer].reshape(1, -1), gp, i_rows, g_rows, ms,
                     ml_h, ml_col0, ret_h + fox_h, i_lane0, f_lane0, chunk)

        x2 = _outproj(y_ret.reshape(b * s, -1), y_fox.reshape(b * s, -1), y_m.reshape(b * s, -1),
                      w_out[layer].astype(BF16), x2, g_a, s)
        x2 = _mlp(x2, norm_mlp_w[layer].reshape(1, d), sc_m, sh_m, g_m,
                  w_ff1[layer].astype(BF16), w_ff2[layer].astype(BF16),
                  final_norm_w.reshape(1, d), s, layer == depth - 1)
    return x2.reshape(b, s, d)
```

```python
import functools
import math

import jax
import jax.numpy as jnp
from jax import lax
from jax.experimental import pallas as pl
from jax.experimental.pallas import tpu as pltpu

HEAD_DIM = 128
CONV_WIDTH = 4
ROPE_BASE = 10000.0
NORM_EPS = 1e-6
N_MOD = 6
GATE_LANES = 128
LOG2E = math.log2(math.e)
FOX_TQ, FOX_TK, FOX_UNROLL = 1024, 512, 2
CONV_HALO = 8

F32 = jnp.float32
BF16 = jnp.bfloat16

VMEM_LIMIT_BYTES = 56 * 1024 * 1024


def _params(*sem, flags=None):
    return pltpu.CompilerParams(dimension_semantics=sem, vmem_limit_bytes=VMEM_LIMIT_BYTES, flags=flags)


def _dot(a, b):
    return jnp.dot(a, b, preferred_element_type=F32)


def _dot_nt(a, b):
    return lax.dot_general(a, b, (((1,), (1,)), ((), ())), preferred_element_type=F32)


def _dot_tn(a, b):
    return lax.dot_general(a, b, (((0,), (0,)), ((), ())), preferred_element_type=F32)


def _sigmoid(x):
    return 1.0 / (1.0 + jnp.exp(-x))


def _log_sigmoid(x):
    return jnp.minimum(x, 0.0) - jnp.log1p(jnp.exp(-jnp.abs(x)))


def _head_norm(y):
    mu = jnp.mean(y, axis=-1, keepdims=True)
    yc = y - mu
    var = jnp.mean(yc * yc, axis=-1, keepdims=True)
    return yc * lax.rsqrt(var + NORM_EPS)


def _ada_kernel(ct_ref, w_ref, b_ref, o_ref, *, kc):
    d, nb = ct_ref.shape
    tn = w_ref.shape[2]

    def body(i, accs):
        r0 = pl.multiple_of(i * kc, kc)
        ck = ct_ref[pl.ds(r0, kc), :]
        ck = ck * _sigmoid(ck)
        wk = w_ref[0, pl.ds(r0, kc), :]
        return tuple(acc + jnp.sum(wk * ck[:, b:b + 1], axis=0, keepdims=True)
                     for b, acc in enumerate(accs))

    accs = lax.fori_loop(0, d // kc, body, tuple(jnp.zeros((1, tn), F32) for _ in range(nb)))
    o_ref[0] = jnp.concatenate(accs, axis=0) + b_ref[0]


def _ada_mod(c, ada_w, ada_b):
    depth, d, n = ada_w.shape
    nb = c.shape[0]
    tn = min(1024, n)
    kc = min(256, d)
    return pl.pallas_call(
        functools.partial(_ada_kernel, kc=kc),
        grid=(depth, n // tn),
        in_specs=[pl.BlockSpec((d, nb), lambda l, j: (0, 0)),
                  pl.BlockSpec((1, d, tn), lambda l, j: (l, 0, j)),
                  pl.BlockSpec((1, 1, tn), lambda l, j: (l, 0, j))],
        out_specs=pl.BlockSpec((1, nb, tn), lambda l, j: (l, 0, j)),
        out_shape=jax.ShapeDtypeStruct((depth, nb, n), F32),
        compiler_params=_params("arbitrary", "arbitrary"),
        name="ada_mod",
    )(c.T, ada_w, ada_b.reshape(depth, 1, n))


def _rope_kernel(pos_ref, invf_ref, sign_ref, cos_ref, sin_ref):
    ang = pos_ref[0] * invf_ref[...]
    cos_ref[0] = jnp.cos(ang)
    sin_ref[0] = jnp.sin(ang) * sign_ref[...]


def _rope_tables(positions):
    b, s = positions.shape
    ts = min(2048, s)
    half = HEAD_DIM // 2
    inv_freq = ROPE_BASE ** (-jnp.arange(0, HEAD_DIM, 2, dtype=F32) / HEAD_DIM)
    invf = jnp.concatenate([inv_freq, inv_freq]).reshape(1, HEAD_DIM)
    sign = jnp.concatenate([-jnp.ones((half,), F32), jnp.ones((half,), F32)]).reshape(1, HEAD_DIM)
    pos = positions.astype(F32).reshape(b, s, 1)
    out = jax.ShapeDtypeStruct((b, s, HEAD_DIM), F32)
    return pl.pallas_call(
        _rope_kernel,
        grid=(b, s // ts),
        in_specs=[pl.BlockSpec((1, ts, 1), lambda i, j: (i, j, 0)),
                  pl.BlockSpec((1, HEAD_DIM), lambda i, j: (0, 0)),
                  pl.BlockSpec((1, HEAD_DIM), lambda i, j: (0, 0))],
        out_specs=[pl.BlockSpec((1, ts, HEAD_DIM), lambda i, j: (i, j, 0))] * 2,
        out_shape=[out, out],
        compiler_params=_params("arbitrary", "arbitrary"),
        name="rope_tables",
    )(pos, invf, sign)


def _modulated_norm(x, nw, sc, sh):
    ms = jnp.mean(x * x, axis=-1, keepdims=True)
    return (x * lax.rsqrt(ms + NORM_EPS) * nw) * (1.0 + sc) + sh


def _inproj_kernel(x_ref, nw_ref, sc_ref, sh_ref, w_ref, wg_ref, o_ref, g_ref, h_scr):
    @pl.when(pl.program_id(1) == 0)
    def _():
        hb = _modulated_norm(x_ref[...], nw_ref[...], sc_ref[0], sh_ref[0]).astype(BF16)
        h_scr[...] = hb
        g_ref[...] = _dot(hb, wg_ref[...])

    o_ref[...] = _dot(h_scr[...], w_ref[...]).astype(BF16)


def _inproj(x2, nw, sc, sh, w_main, w_gate, seq):
    m, d = x2.shape
    nm = w_main.shape[1]
    tm = min(1024, seq)
    tn = 1024 if nm % 1024 == 0 else 512
    bpr = seq // tm
    return pl.pallas_call(
        _inproj_kernel,
        grid=(m // tm, nm // tn),
        in_specs=[pl.BlockSpec((tm, d), lambda i, j: (i, 0)),
                  pl.BlockSpec((1, d), lambda i, j: (0, 0)),
                  pl.BlockSpec((1, 1, d), lambda i, j: (i // bpr, 0, 0)),
                  pl.BlockSpec((1, 1, d), lambda i, j: (i // bpr, 0, 0)),
                  pl.BlockSpec((d, tn), lambda i, j: (0, j)),
                  pl.BlockSpec((d, GATE_LANES), lambda i, j: (0, 0))],
        out_specs=[pl.BlockSpec((tm, tn), lambda i, j: (i, j)),
                   pl.BlockSpec((tm, GATE_LANES), lambda i, j: (i, 0))],
        out_shape=[jax.ShapeDtypeStruct((m, nm), BF16),
                   jax.ShapeDtypeStruct((m, GATE_LANES), F32)],
        scratch_shapes=[pltpu.VMEM((tm, d), BF16)],
        compiler_params=_params("arbitrary", "arbitrary"),
        name="inproj",
    )(x2, nw, sc, sh, w_main, w_gate)


def _gates_kernel(g_ref, bias_ref, cum_ref, o_ref, carry):
    @pl.when(pl.program_id(1) == 0)
    def _():
        carry[...] = jnp.zeros_like(carry)

    g = g_ref[0] + bias_ref[...]
    ls = _log_sigmoid(g)
    ts = ls.shape[0]
    row = lax.broadcasted_iota(jnp.int32, (ts, ts), 0)
    col = lax.broadcasted_iota(jnp.int32, (ts, ts), 1)
    tri = jnp.where(row >= col, 1.0, 0.0).astype(BF16)
    hi = ls.astype(BF16)
    r1 = ls - hi.astype(F32)
    mid = r1.astype(BF16)
    lo = (r1 - mid.astype(F32)).astype(BF16)
    csum = (_dot(tri, hi) + _dot(tri, mid)) + _dot(tri, lo) + carry[...]
    carry[...] = csum[ts - 1:ts, :]
    o_ref[0] = jnp.where(cum_ref[...] > 0.5, csum, g)


def _gates(g3, bias, cum_mask):
    b, s, n = g3.shape
    ts = min(512, s)
    return pl.pallas_call(
        _gates_kernel,
        grid=(b, s // ts),
        in_specs=[pl.BlockSpec((1, ts, n), lambda i, j: (i, j, 0)),
                  pl.BlockSpec((1, n), lambda i, j: (0, 0)),
                  pl.BlockSpec((1, n), lambda i, j: (0, 0))],
        out_specs=pl.BlockSpec((1, ts, n), lambda i, j: (i, j, 0)),
        out_shape=jax.ShapeDtypeStruct((b, s, n), F32),
        scratch_shapes=[pltpu.VMEM((1, n), F32)],
        compiler_params=_params("arbitrary", "arbitrary"),
        name="gates",
    )(g3, bias, cum_mask)


def _retention_kernel(q_ref, k_ref, v_ref, g_ref, cos_ref, sin_ref, ms_ref, o_ref, state, decay, *, heads):
    n = cos_ref.shape[1]
    log_gamma = [math.log(1.0 - 2.0 ** (-5.0 - h)) for h in range(heads)]

    @pl.when(pl.program_id(1) == 0)
    def _():
        state[...] = jnp.zeros_like(state)
        row = lax.broadcasted_iota(jnp.int32, (n, n), 0)
        col = lax.broadcasted_iota(jnp.int32, (n, n), 1)
        rel = (row - col).astype(F32)
        for h in range(heads):
            decay[h] = jnp.where(rel >= 0, jnp.exp(jnp.maximum(rel, 0.0) * log_gamma[h]), 0.0)

    cos = cos_ref[0]
    sin = sin_ref[0]
    half = HEAD_DIM // 2
    idx = lax.broadcasted_iota(jnp.int32, (n, 1), 0).astype(F32)

    def rot(t):
        return t * cos + pltpu.roll(t, half, 1) * sin

    for h in range(heads):
        lg = log_gamma[h]
        cs = slice(h * HEAD_DIM, (h + 1) * HEAD_DIM)
        q = rot(q_ref[0, :, cs].astype(F32))
        k = rot(k_ref[0, :, cs].astype(F32)) * (HEAD_DIM ** -0.5)
        v = v_ref[0, :, cs]
        q_decay = jnp.exp((idx + 1.0) * lg)
        k_decay = jnp.exp((n - 1.0 - idx) * lg)
        chunk_decay = math.exp(n * lg)

        scores = _dot_nt(q.astype(BF16), k.astype(BF16)) * decay[h]
        intra = _dot(scores.astype(BF16), v)
        st = state[h]
        inter = _dot((q * q_decay).astype(BF16), st.astype(BF16))
        state[h] = st * chunk_decay + _dot_tn((k * k_decay).astype(BF16), v)

        y = _head_norm(intra + inter)
        gate = g_ref[0, :, cs].astype(F32)
        o_ref[0, :, cs] = (y * (gate * _sigmoid(gate)) * ms_ref[:, cs]).astype(BF16)


def _retention(proj, cos, sin, merge_scale, heads, col0, chunk):
    b, s, _ = proj.shape
    w = heads * HEAD_DIM
    assert (col0 * HEAD_DIM) % w == 0
    blk0 = col0 * HEAD_DIM // w
    grp = lambda base: pl.BlockSpec((1, chunk, w), lambda i, c: (i, c, blk0 + base))
    tok = pl.BlockSpec((1, chunk, HEAD_DIM), lambda i, c: (i, c, 0))
    return pl.pallas_call(
        functools.partial(_retention_kernel, heads=heads),
        grid=(b, s // chunk),
        in_specs=[grp(0), grp(1), grp(2), grp(3), tok, tok,
                  pl.BlockSpec((1, w), lambda i, c: (0, 0))],
        out_specs=pl.BlockSpec((1, chunk, w), lambda i, c: (i, c, 0)),
        out_shape=jax.ShapeDtypeStruct((b, s, w), BF16),
        scratch_shapes=[pltpu.VMEM((heads, HEAD_DIM, HEAD_DIM), F32),
                        pltpu.VMEM((heads, chunk, chunk), F32)],
        compiler_params=_params("arbitrary", "arbitrary"),
        name="retention",
    )(proj, proj, proj, proj, cos, sin, merge_scale)


def _fox_keys_kernel(k_ref, gp_ref, o_ref, *, tk):
    h = pl.program_id(1)
    ts = k_ref.shape[1]
    c = gp_ref[0]
    parts = []
    for i in range(ts // tk):
        blk = c[i * tk:(i + 1) * tk]
        parts.append((blk[0:1, :] - blk) * LOG2E)
    x = jnp.concatenate(parts, axis=0) if len(parts) > 1 else parts[0]
    hi = x.astype(BF16)
    r1 = x - hi.astype(F32)
    mid = r1.astype(BF16)
    lo = (r1 - mid.astype(F32)).astype(BF16)
    row = lax.broadcasted_iota(jnp.int32, (GATE_LANES, HEAD_DIM), 0)
    col = lax.broadcasted_iota(jnp.int32, (GATE_LANES, HEAD_DIM), 1)
    sel = lambda piece: jnp.where((row == h) & (col == piece), 1.0, 0.0).astype(BF16)
    bias = _dot(hi, sel(0)) + _dot(mid, sel(1)) + _dot(lo, sel(2))
    o_ref[0, 0] = jnp.concatenate([k_ref[0], bias.astype(BF16)], axis=1)


def _fox_keys(proj, gp, heads, k_col0, tk):
    b, s, _ = proj.shape
    ts = min(2048, s)
    return pl.pallas_call(
        functools.partial(_fox_keys_kernel, tk=tk),
        grid=(b, heads, s // ts),
        in_specs=[pl.BlockSpec((1, ts, HEAD_DIM), lambda i, h, j: (i, j, k_col0 + h)),
                  pl.BlockSpec((1, ts, GATE_LANES), lambda i, h, j: (i, j, 0))],
        out_specs=pl.BlockSpec((1, 1, ts, 2 * HEAD_DIM), lambda i, h, j: (i, h, j, 0)),
        out_shape=jax.ShapeDtypeStruct((b, heads, s, 2 * HEAD_DIM), BF16),
        compiler_params=_params("arbitrary", "arbitrary", "arbitrary"),
        name="fox_keys",
    )(proj, gp)


def _fox_kernel(q_ref, k_ref, v_ref, c_ref, ms_ref, o_ref, s_scr, p_scr, al_scr, m_scr, l_scr, acc_scr,
                *, tq, tk, unroll):
    ndiag = tq // tk
    static_slots = unroll % 2 == 0 and ndiag % 2 == 0
    qi = pl.program_id(2)
    nfull = qi * ndiag
    q0 = pl.multiple_of(qi * tq, tq)
    q = (q_ref[0].astype(F32) * (HEAD_DIM ** -0.5 * LOG2E)).astype(BF16)
    lane = lax.broadcasted_iota(jnp.int32, (tq, HEAD_DIM), 1)
    q_aug = jnp.concatenate([q, jnp.where(lane < 3, 1.0, 0.0).astype(BF16)], axis=1)
    c_first = c_ref[0, :, pl.ds(q0, 128)][:, 0:1]

    m_scr[...] = jnp.full_like(m_scr, -jnp.inf)
    l_scr[...] = jnp.zeros_like(l_scr)
    acc_scr[...] = jnp.zeros_like(acc_scr)
    p_scr[1] = jnp.zeros((tk, tq), BF16)
    al_scr[1] = jnp.ones((1, tq), F32)

    def scores(j, slot):
        k0 = pl.multiple_of(j * tk, tk)
        s_scr[slot] = _dot_nt(k_ref[0, 0, pl.ds(k0, tk), :], q_aug)

    def softmax(j, slot, diag):
        k0 = pl.multiple_of(j * tk, tk)
        delta = (c_first - c_ref[0, :, pl.ds(k0, 128)][:, 0:1]) * LOG2E
        t = s_scr[slot]
        if diag is not None:
            key = lax.broadcasted_iota(jnp.int32, (tk, tq), 0) + diag * tk
            qry = lax.broadcasted_iota(jnp.int32, (tk, tq), 1)
            t = jnp.where(qry >= key, t, -jnp.inf)
        m_prev = m_scr[...]
        m_new = jnp.maximum(m_prev, jnp.max(t, axis=0, keepdims=True) + delta)
        p = jnp.exp2(t - (m_new - delta))
        alpha = jnp.exp2(m_prev - m_new)
        l_scr[...] = alpha * l_scr[...] + jnp.sum(p, axis=0, keepdims=True)
        m_scr[...] = m_new
        al_scr[slot] = alpha
        p_scr[slot] = p.astype(BF16)

    def values(j, slot):
        k0 = pl.multiple_of(j * tk, tk)
        acc_scr[...] = al_scr[slot] * acc_scr[...] + _dot_tn(v_ref[0, pl.ds(k0, tk), :], p_scr[slot])

    def step(j, slot, diag, last):
        values(jnp.maximum(j - 1, 0), 1 - slot)
        softmax(j, slot, diag)
        if not last:
            scores(j + 1, 1 - slot)

    scores(0, 0)

    def body(i, carry):
        for u in range(unroll):
            j = i * unroll + u
            step(j, u % 2 if static_slots else j & 1, None, False)
        return carry

    lax.fori_loop(0, nfull // unroll, body, 0)
    for d in range(ndiag):
        j = nfull + d
        slot = d % 2 if ndiag % 2 == 0 else j & 1
        step(j, slot, d, d == ndiag - 1)
    values(j, slot)

    out = (acc_scr[...] / l_scr[...]).T
    o_ref[0] = (_head_norm(out) * ms_ref[...]).astype(BF16)


def _fox(proj, k_aug, c_rows, merge_scale, heads, col0, ms_col0, tq, tk, unroll):
    b, s, _ = proj.shape
    assert tq % tk == 0 and (tq // tk) % unroll == 0
    return pl.pallas_call(
        functools.partial(_fox_kernel, tq=tq, tk=tk, unroll=unroll),
        grid=(b, heads, s // tq),
        in_specs=[pl.BlockSpec((1, tq, HEAD_DIM), lambda i, h, q: (i, q, col0 + h)),
                  pl.BlockSpec((1, 1, s, 2 * HEAD_DIM), lambda i, h, q: (i, h, 0, 0)),
                  pl.BlockSpec((1, s, HEAD_DIM), lambda i, h, q: (i, 0, col0 + 2 * heads + h)),
                  pl.BlockSpec((1, 1, s), lambda i, h, q: (i * heads + h, 0, 0)),
                  pl.BlockSpec((1, HEAD_DIM), lambda i, h, q: (0, ms_col0 + h))],
        out_specs=pl.BlockSpec((1, tq, HEAD_DIM), lambda i, h, q: (i, q, h)),
        out_shape=jax.ShapeDtypeStruct((b, s, heads * HEAD_DIM), BF16),
        scratch_shapes=[pltpu.VMEM((2, tk, tq), F32),
                        pltpu.VMEM((2, tk, tq), BF16),
                        pltpu.VMEM((2, 1, tq), F32),
                        pltpu.VMEM((1, tq), F32),
                        pltpu.VMEM((1, tq), F32),
                        pltpu.VMEM((HEAD_DIM, tq), F32)],
        compiler_params=_params("arbitrary", "arbitrary", "arbitrary"),
        name="fox_attention",
    )(proj, k_aug, proj, c_rows, merge_scale)


def _mlstm_kernel(q_ref, k_ref, v_ref, og_ref, w_ref, b_ref, gp_ref, ir_ref, gr_ref, ms_ref, o_ref,
                  c_scr, n_scr, m_scr, g0_scr, xq_scr, xk_scr, *, heads, i_lane0, f_lane0):
    n = q_ref.shape[1]
    w = heads * HEAD_DIM

    @pl.when(pl.program_id(1) == 0)
    def _():
        c_scr[...] = jnp.zeros_like(c_scr)
        n_scr[...] = jnp.zeros_like(n_scr)
        m_scr[...] = jnp.zeros_like(m_scr)
        g0_scr[...] = jnp.zeros_like(g0_scr)
        xq_scr[0:CONV_HALO, :] = jnp.zeros((CONV_HALO, w), F32)
        xk_scr[0:CONV_HALO, :] = jnp.zeros((CONV_HALO, w), F32)

    def conv_silu(x_ref, x_scr, cols):
        x_scr[CONV_HALO:CONV_HALO + n, :] = x_ref[0].astype(F32)
        y = b_ref[:, cols] + jnp.zeros((n, w), F32)
        for j in range(CONV_WIDTH):
            shift = CONV_WIDTH - 1 - j
            y = y + w_ref[j:j + 1, cols] * x_scr[CONV_HALO - shift:CONV_HALO - shift + n, :]
        x_scr[0:CONV_HALO, :] = x_scr[n:n + CONV_HALO, :]
        return y * _sigmoid(y)

    q_all = conv_silu(q_ref, xq_scr, slice(0, w))
    k_all = conv_silu(k_ref, xk_scr, slice(w, 2 * w)) * (HEAD_DIM ** -0.5)
    gp = gp_ref[0]
    row = lax.broadcasted_iota(jnp.int32, (n, n), 0)
    col = lax.broadcasted_iota(jnp.int32, (n, n), 1)
    causal = row >= col

    for h in range(heads):
        cs = slice(h * HEAD_DIM, (h + 1) * HEAD_DIM)
        q = q_all[:, cs]
        k = k_all[:, cs]
        v = v_ref[0, :, cs]
        ic = gp[:, i_lane0 + h:i_lane0 + h + 1]
        gc = gp[:, f_lane0 + h:f_lane0 + h + 1]
        ir = ir_ref[0, h:h + 1, :]
        gr = gr_ref[0, h:h + 1, :]

        g0 = g0_scr[h][:, 0:1]
        m_prev = m_scr[h][:, 0:1]
        b_col = gc - g0
        dmat = jnp.where(causal, gc + (ir - gr), -jnp.inf)
        inter_log = b_col + m_prev
        m_q = jnp.maximum(inter_log, jnp.max(dmat, axis=-1, keepdims=True))
        w_intra = jnp.exp(dmat - m_q)
        w_inter = jnp.exp(inter_log - m_q)

        qb = q.astype(BF16)
        s = _dot_nt(qb, k.astype(BF16)) * w_intra
        cst = c_scr[h]
        nst = n_scr[h]
        num = _dot(s.astype(BF16), v) + w_inter * _dot(qb, cst.astype(BF16))
        den = jnp.sum(s, axis=-1, keepdims=True) + w_inter * jnp.sum(q * nst, axis=-1, keepdims=True)
        hid = num / jnp.maximum(jnp.abs(den), jnp.exp(-m_q))

        b_last = b_col[n - 1:n, :]
        k_log = b_last - b_col + ic
        m_new = jnp.maximum(b_last + m_prev, jnp.max(k_log, axis=0, keepdims=True))
        wk = jnp.exp(k_log - m_new)
        carry_scale = jnp.exp(b_last + m_prev - m_new)
        kw = k * wk
        c_scr[h] = carry_scale * cst + _dot_tn(kw.astype(BF16), v)
        n_scr[h] = carry_scale * nst + jnp.sum(kw, axis=0, keepdims=True)
        m_scr[h] = jnp.broadcast_to(m_new, (1, HEAD_DIM))
        g0_scr[h] = jnp.broadcast_to(gc[n - 1:n, :], (1, HEAD_DIM))

        og = og_ref[0, :, cs].astype(F32)
        o_ref[0, :, cs] = (_head_norm(hid) * _sigmoid(og) * ms_ref[:, cs]).astype(BF16)


def _mlstm(proj, conv_w, conv_b, gp, i_rows, g_rows, merge_scale, heads, col0, ms_col0, i_lane0, f_lane0, chunk):
    b, s, _ = proj.shape
    w = heads * HEAD_DIM
    assert (col0 * HEAD_DIM) % w == 0 and (ms_col0 * HEAD_DIM) % w == 0
    blk0 = col0 * HEAD_DIM // w
    grp = lambda base: pl.BlockSpec((1, chunk, w), lambda i, c: (i, c, blk0 + base))
    row_spec = pl.BlockSpec((1, heads, chunk), lambda i, c: (i, 0, c))
    state = lambda r, c: pltpu.VMEM((heads, r, c), F32)
    return pl.pallas_call(
        functools.partial(_mlstm_kernel, heads=heads, i_lane0=i_lane0, f_lane0=f_lane0),
        grid=(b, s // chunk),
        in_specs=[grp(0), grp(1), grp(2), grp(3),
                  pl.BlockSpec(conv_w.shape, lambda i, c: (0, 0)),
                  pl.BlockSpec(conv_b.shape, lambda i, c: (0, 0)),
                  pl.BlockSpec((1, chunk, GATE_LANES), lambda i, c: (i, c, 0)),
                  row_spec, row_spec,
                  pl.BlockSpec((1, w), lambda i, c: (0, ms_col0 * HEAD_DIM // w))],
        out_specs=pl.BlockSpec((1, chunk, w), lambda i, c: (i, c, 0)),
        out_shape=jax.ShapeDtypeStruct((b, s, w), BF16),
        scratch_shapes=[state(HEAD_DIM, HEAD_DIM), state(1, HEAD_DIM), state(1, HEAD_DIM), state(1, HEAD_DIM),
                        pltpu.VMEM((chunk + CONV_HALO, w), F32),
                        pltpu.VMEM((chunk + CONV_HALO, w), F32)],
        compiler_params=_params("arbitrary", "arbitrary"),
        name="mlstm",
    )(proj, proj, proj, proj, conv_w, conv_b, gp, i_rows, g_rows, merge_scale)


def _outproj_kernel(yr_ref, yf_ref, ym_ref, w_ref, x_ref, ga_ref, o_ref):
    r = yr_ref.shape[1]
    f = yf_ref.shape[1]
    y = _dot(yr_ref[...], w_ref[0:r, :]) + _dot(yf_ref[...], w_ref[r:r + f, :])
    y = y + _dot(ym_ref[...], w_ref[r + f:, :])
    o_ref[...] = x_ref[...] + ga_ref[0] * y


def _outproj(y_ret, y_fox, y_m, w_out, x2, g_a, seq):
    m, d = x2.shape
    tm = min(512, seq)
    bpr = seq // tm
    rows = lambda w: pl.BlockSpec((tm, w), lambda i: (i, 0))
    return pl.pallas_call(
        _outproj_kernel,
        grid=(m // tm,),
        in_specs=[rows(y_ret.shape[1]), rows(y_fox.shape[1]), rows(y_m.shape[1]),
                  pl.BlockSpec(w_out.shape, lambda i: (0, 0)),
                  rows(d),
                  pl.BlockSpec((1, 1, d), lambda i: (i // bpr, 0, 0))],
        out_specs=rows(d),
        out_shape=jax.ShapeDtypeStruct((m, d), F32),
        compiler_params=_params("arbitrary"),
        name="outproj",
    )(y_ret, y_fox, y_m, w_out, x2, g_a)


def _mlp_kernel(x_ref, nw_ref, sc_ref, sh_ref, gm_ref, w1_ref, w2_ref, fw_ref, o_ref, h_scr, acc_scr,
                *, final_norm):
    f = pl.program_id(1)

    @pl.when(f == 0)
    def _():
        h_scr[...] = _modulated_norm(x_ref[...], nw_ref[...], sc_ref[0], sh_ref[0]).astype(BF16)
        acc_scr[...] = jnp.zeros_like(acc_scr)

    a = jnp.maximum(_dot(h_scr[...], w1_ref[...]), 0.0)
    acc_scr[...] += _dot((a * a).astype(BF16), w2_ref[...])

    @pl.when(f == pl.num_programs(1) - 1)
    def _():
        y = x_ref[...] + gm_ref[0] * acc_scr[...]
        if final_norm:
            ms = jnp.mean(y * y, axis=-1, keepdims=True)
            y = y * lax.rsqrt(ms + NORM_EPS) * fw_ref[...]
        o_ref[...] = y


def _mlp(x2, nw, sc, sh, g_m, w1, w2, final_w, seq, final_norm):
    m, d = x2.shape
    dff = w1.shape[1]
    tm = min(512, seq)
    tf = min(512, dff)
    bpr = seq // tm
    mod = pl.BlockSpec((1, 1, d), lambda i, f: (i // bpr, 0, 0))
    vec = pl.BlockSpec((1, d), lambda i, f: (0, 0))
    return pl.pallas_call(
        functools.partial(_mlp_kernel, final_norm=final_norm),
        grid=(m // tm, dff // tf),
        in_specs=[pl.BlockSpec((tm, d), lambda i, f: (i, 0)),
                  vec, mod, mod, mod,
                  pl.BlockSpec((d, tf), lambda i, f: (0, f)),
                  pl.BlockSpec((tf, d), lambda i, f: (f, 0)),
                  vec],
        out_specs=pl.BlockSpec((tm, d), lambda i, f: (i, 0)),
        out_shape=jax.ShapeDtypeStruct((m, d), F32),
        scratch_shapes=[pltpu.VMEM((tm, d), BF16), pltpu.VMEM((tm, d), F32)],
        compiler_params=_params("arbitrary", "arbitrary"),
        name="mlp",
    )(x2, nw, sc, sh, g_m, w1, w2, final_w)


def kernel(x, c, positions, ada_w, ada_b, norm_mix_w, norm_mlp_w, w_in, conv_w, conv_b, fox_f_bias,
           mlstm_i_bias, mlstm_f_bias, merge_scale, w_out, w_ff1, w_ff2, final_norm_w):
    b, s, d = x.shape
    depth = ada_w.shape[0]
    ret_w, fox_w, ml_w = d // 4, d // 2, d // 4
    ret_h, fox_h, ml_h = ret_w // HEAD_DIM, fox_w // HEAD_DIM, ml_w // HEAD_DIM
    assert s % 128 == 0 and d % (4 * HEAD_DIM) == 0
    assert fox_h + 2 * ml_h <= GATE_LANES

    sizes = [ret_w] * 4 + [fox_w] * 3 + [fox_h] + [ml_w] * 4 + [ml_h, ml_h]
    offs = [0]
    for sz in sizes:
        offs.append(offs[-1] + sz)
    n_gate = fox_h + 2 * ml_h
    ret_col0 = 0
    fox_col0 = (4 * ret_w) // HEAD_DIM
    ml_col0 = (4 * ret_w + 3 * fox_w) // HEAD_DIM
    i_lane0, f_lane0 = fox_h, fox_h + ml_h

    chunk = min(256, s)
    tq = min(FOX_TQ, s)
    tk = min(FOX_TK, tq)

    mod = _ada_mod(c, ada_w, ada_b)
    cos, sin = _rope_tables(positions)
    cum_mask = jnp.zeros((1, GATE_LANES), F32).at[0, :fox_h].set(1.0).at[0, f_lane0:f_lane0 + ml_h].set(1.0)

    x2 = x.reshape(b * s, d)
    for layer in range(depth):
        sh_a, sc_a, g_a, sh_m, sc_m, g_m = [t.reshape(b, 1, d) for t in jnp.split(mod[layer], N_MOD, axis=-1)]
        wl = w_in[layer]
        w_main = jnp.concatenate([wl[:, offs[0]:offs[7]], wl[:, offs[8]:offs[12]]], axis=1).astype(BF16)
        w_gate = jnp.concatenate([wl[:, offs[7]:offs[8]], wl[:, offs[12]:offs[14]],
                                  jnp.zeros((d, GATE_LANES - n_gate), F32)], axis=1).astype(BF16)
        gate_bias = jnp.pad(jnp.concatenate([fox_f_bias[layer], mlstm_i_bias[layer], mlstm_f_bias[layer]]),
                            (0, GATE_LANES - n_gate)).reshape(1, GATE_LANES)
        ms = merge_scale[layer].reshape(1, d)

        proj, gates = _inproj(x2, norm_mix_w[layer].reshape(1, d), sc_a, sh_a, w_main, w_gate, s)
        proj = proj.reshape(b, s, -1)
        gp = _gates(gates.reshape(b, s, GATE_LANES), gate_bias, cum_mask)
        gp_t = jnp.swapaxes(gp[:, :, :n_gate], 1, 2)
        c_rows = gp_t[:, :fox_h].reshape(b * fox_h, 1, s)
        i_rows = gp_t[:, i_lane0:i_lane0 + ml_h]
        g_rows = gp_t[:, f_lane0:f_lane0 + ml_h]

        y_ret = _retention(proj, cos, sin, ms, ret_h, ret_col0, chunk)
        k_aug = _fox_keys(proj, gp, fox_h, fox_col0 + fox_h, tk)
        y_fox = _fox(proj, k_aug, c_rows, ms, fox_h, fox_col0, ret_h, tq, tk, FOX_UNROLL if tq // tk % FOX_UNROLL == 0 else 1)
        y_m = _mlstm(proj, conv_w[layer], conv_b[layer].reshape(1, -1), gp, i_rows, g_rows, ms,
                     ml_h, ml_col0, ret_h + fox_h, i_lane0, f_lane0, chunk)

        x2 = _outproj(y_ret.reshape(b * s, -1), y_fox.reshape(b * s, -1), y_m.reshape(b * s, -1),
                      w_out[layer].astype(BF16), x2, g_a, s)
        x2 = _mlp(x2, norm_mlp_w[layer].reshape(1, d), sc_m, sh_m, g_m,
                  w_ff1[layer].astype(BF16), w_ff2[layer].astype(BF16),
                  final_norm_w.reshape(1, d), s, layer == depth - 1)
    return x2.reshape(b, s, d)
```

```python
import functools
import math

import jax
import jax.numpy as jnp
from jax import lax
from jax.experimental import pallas as pl
from jax.experimental.pallas import tpu as pltpu

HEAD_DIM = 128
CONV_WIDTH = 4
ROPE_BASE = 10000.0
NORM_EPS = 1e-6
N_MOD = 6
GATE_LANES = 128
LOG2E = math.log2(math.e)
FOX_TQ, FOX_TK, FOX_UNROLL = 1024, 512, 2
CONV_HALO = 8

F32 = jnp.float32
BF16 = jnp.bfloat16

VMEM_LIMIT_BYTES = 56 * 1024 * 1024


def _params(*sem, flags=None):
    return pltpu.CompilerParams(dimension_semantics=sem, vmem_limit_bytes=VMEM_LIMIT_BYTES, flags=flags)


def _dot(a, b):
    return jnp.dot(a, b, preferred_element_type=F32)


def _dot_nt(a, b):
    return lax.dot_general(a, b, (((1,), (1,)), ((), ())), preferred_element_type=F32)


def _dot_tn(a, b):
    return lax.dot_general(a, b, (((0,), (0,)), ((), ())), preferred_element_type=F32)


def _sigmoid(x):
    return 1.0 / (1.0 + jnp.exp(-x))


def _log_sigmoid(x):
    return jnp.minimum(x, 0.0) - jnp.log1p(jnp.exp(-jnp.abs(x)))


def _head_norm(y):
    mu = jnp.mean(y, axis=-1, keepdims=True)
    yc = y - mu
    var = jnp.mean(yc * yc, axis=-1, keepdims=True)
    return yc * lax.rsqrt(var + NORM_EPS)


def _ada_kernel(ct_ref, w_ref, b_ref, o_ref, *, kc):
    d, nb = ct_ref.shape
    tn = w_ref.shape[2]

    def body(i, accs):
        r0 = pl.multiple_of(i * kc, kc)
        ck = ct_ref[pl.ds(r0, kc), :]
        ck = ck * _sigmoid(ck)
        wk = w_ref[0, pl.ds(r0, kc), :]
        return tuple(acc + jnp.sum(wk * ck[:, b:b + 1], axis=0, keepdims=True)
                     for b, acc in enumerate(accs))

    accs = lax.fori_loop(0, d // kc, body, tuple(jnp.zeros((1, tn), F32) for _ in range(nb)))
    o_ref[0] = jnp.concatenate(accs, axis=0) + b_ref[0]


def _ada_mod(c, ada_w, ada_b):
    depth, d, n = ada_w.shape
    nb = c.shape[0]
    tn = min(1024, n)
    kc = min(256, d)
    return pl.pallas_call(
        functools.partial(_ada_kernel, kc=kc),
        grid=(depth, n // tn),
        in_specs=[pl.BlockSpec((d, nb), lambda l, j: (0, 0)),
                  pl.BlockSpec((1, d, tn), lambda l, j: (l, 0, j)),
                  pl.BlockSpec((1, 1, tn), lambda l, j: (l, 0, j))],
        out_specs=pl.BlockSpec((1, nb, tn), lambda l, j: (l, 0, j)),
        out_shape=jax.ShapeDtypeStruct((depth, nb, n), F32),
        compiler_params=_params("arbitrary", "arbitrary"),
        name="ada_mod",
    )(c.T, ada_w, ada_b.reshape(depth, 1, n))


def _rope_kernel(pos_ref, invf_ref, sign_ref, cos_ref, sin_ref):
    ang = pos_ref[0] * invf_ref[...]
    cos_ref[0] = jnp.cos(ang)
    sin_ref[0] = jnp.sin(ang) * sign_ref[...]


def _rope_tables(positions):
    b, s = positions.shape
    ts = min(2048, s)
    half = HEAD_DIM // 2
    inv_freq = ROPE_BASE ** (-jnp.arange(0, HEAD_DIM, 2, dtype=F32) / HEAD_DIM)
    invf = jnp.concatenate([inv_freq, inv_freq]).reshape(1, HEAD_DIM)
    sign = jnp.concatenate([-jnp.ones((half,), F32), jnp.ones((half,), F32)]).reshape(1, HEAD_DIM)
    pos = positions.astype(F32).reshape(b, s, 1)
    out = jax.ShapeDtypeStruct((b, s, HEAD_DIM), F32)
    return pl.pallas_call(
        _rope_kernel,
        grid=(b, s // ts),
        in_specs=[pl.BlockSpec((1, ts, 1), lambda i, j: (i, j, 0)),
                  pl.BlockSpec((1, HEAD_DIM), lambda i, j: (0, 0)),
                  pl.BlockSpec((1, HEAD_DIM), lambda i, j: (0, 0))],
        out_specs=[pl.BlockSpec((1, ts, HEAD_DIM), lambda i, j: (i, j, 0))] * 2,
        out_shape=[out, out],
        compiler_params=_params("arbitrary", "arbitrary"),
        name="rope_tables",
    )(pos, invf, sign)


def _modulated_norm_rows(x_ref, nw_ref, sc_ref, sh_ref, h_scr):
    x = x_ref[...]
    inv = lax.rsqrt(jnp.mean(x * x, axis=-1, keepdims=True) + NORM_EPS)
    h_scr[...] = ((x_ref[...] * inv * nw_ref[...]) * (1.0 + sc_ref[0]) + sh_ref[0]).astype(BF16)


def _inproj_kernel(x_ref, nw_ref, sc_ref, sh_ref, w_ref, wg_ref, o_ref, g_ref, h_scr):
    @pl.when(pl.program_id(1) == 0)
    def _():
        _modulated_norm_rows(x_ref, nw_ref, sc_ref, sh_ref, h_scr)
        g_ref[...] = _dot(h_scr[...], wg_ref[...])

    o_ref[...] = _dot(h_scr[...], w_ref[...]).astype(BF16)


def _inproj(x2, nw, sc, sh, w_main, w_gate, seq):
    m, d = x2.shape
    nm = w_main.shape[1]
    tm = min(1024, seq)
    tn = 1024 if nm % 1024 == 0 else 512
    bpr = seq // tm
    return pl.pallas_call(
        _inproj_kernel,
        grid=(m // tm, nm // tn),
        in_specs=[pl.BlockSpec((tm, d), lambda i, j: (i, 0)),
                  pl.BlockSpec((1, d), lambda i, j: (0, 0)),
                  pl.BlockSpec((1, 1, d), lambda i, j: (i // bpr, 0, 0)),
                  pl.BlockSpec((1, 1, d), lambda i, j: (i // bpr, 0, 0)),
                  pl.BlockSpec((d, tn), lambda i, j: (0, j)),
                  pl.BlockSpec((d, GATE_LANES), lambda i, j: (0, 0))],
        out_specs=[pl.BlockSpec((tm, tn), lambda i, j: (i, j)),
                   pl.BlockSpec((tm, GATE_LANES), lambda i, j: (i, 0))],
        out_shape=[jax.ShapeDtypeStruct((m, nm), BF16),
                   jax.ShapeDtypeStruct((m, GATE_LANES), F32)],
        scratch_shapes=[pltpu.VMEM((tm, d), BF16)],
        compiler_params=_params("arbitrary", "arbitrary"),
        name="inproj",
    )(x2, nw, sc, sh, w_main, w_gate)


def _gates_kernel(g_ref, bias_ref, cum_ref, o_ref, carry):
    @pl.when(pl.program_id(1) == 0)
    def _():
        carry[...] = jnp.zeros_like(carry)

    g = g_ref[0] + bias_ref[...]
    ls = _log_sigmoid(g)
    ts = ls.shape[0]
    row = lax.broadcasted_iota(jnp.int32, (ts, ts), 0)
    col = lax.broadcasted_iota(jnp.int32, (ts, ts), 1)
    tri = jnp.where(row >= col, 1.0, 0.0).astype(BF16)
    hi = ls.astype(BF16)
    r1 = ls - hi.astype(F32)
    mid = r1.astype(BF16)
    lo = (r1 - mid.astype(F32)).astype(BF16)
    csum = (_dot(tri, hi) + _dot(tri, mid)) + _dot(tri, lo) + carry[...]
    carry[...] = csum[ts - 1:ts, :]
    o_ref[0] = jnp.where(cum_ref[...] > 0.5, csum, g)


def _gates(g3, bias, cum_mask):
    b, s, n = g3.shape
    ts = min(512, s)
    return pl.pallas_call(
        _gates_kernel,
        grid=(b, s // ts),
        in_specs=[pl.BlockSpec((1, ts, n), lambda i, j: (i, j, 0)),
                  pl.BlockSpec((1, n), lambda i, j: (0, 0)),
                  pl.BlockSpec((1, n), lambda i, j: (0, 0))],
        out_specs=pl.BlockSpec((1, ts, n), lambda i, j: (i, j, 0)),
        out_shape=jax.ShapeDtypeStruct((b, s, n), F32),
        scratch_shapes=[pltpu.VMEM((1, n), F32)],
        compiler_params=_params("arbitrary", "arbitrary"),
        name="gates",
    )(g3, bias, cum_mask)


def _retention_kernel(q_ref, k_ref, v_ref, g_ref, cos_ref, sin_ref, ms_ref, o_ref, state, decay, *, heads):
    n = cos_ref.shape[1]
    log_gamma = [math.log(1.0 - 2.0 ** (-5.0 - h)) for h in range(heads)]

    @pl.when(pl.program_id(1) == 0)
    def _():
        state[...] = jnp.zeros_like(state)
        row = lax.broadcasted_iota(jnp.int32, (n, n), 0)
        col = lax.broadcasted_iota(jnp.int32, (n, n), 1)
        rel = (row - col).astype(F32)
        for h in range(heads):
            decay[h] = jnp.where(rel >= 0, jnp.exp(jnp.maximum(rel, 0.0) * log_gamma[h]), 0.0)

    cos = cos_ref[0]
    sin = sin_ref[0]
    half = HEAD_DIM // 2
    idx = lax.broadcasted_iota(jnp.int32, (n, 1), 0).astype(F32)

    def rot(t):
        return t * cos + pltpu.roll(t, half, 1) * sin

    for h in range(heads):
        lg = log_gamma[h]
        cs = slice(h * HEAD_DIM, (h + 1) * HEAD_DIM)
        q = rot(q_ref[0, :, cs].astype(F32))
        k = rot(k_ref[0, :, cs].astype(F32)) * (HEAD_DIM ** -0.5)
        v = v_ref[0, :, cs]
        q_decay = jnp.exp((idx + 1.0) * lg)
        k_decay = jnp.exp((n - 1.0 - idx) * lg)
        chunk_decay = math.exp(n * lg)

        scores = _dot_nt(q.astype(BF16), k.astype(BF16)) * decay[h]
        intra = _dot(scores.astype(BF16), v)
        st = state[h]
        inter = _dot((q * q_decay).astype(BF16), st.astype(BF16))
        state[h] = st * chunk_decay + _dot_tn((k * k_decay).astype(BF16), v)

        y = _head_norm(intra + inter)
        gate = g_ref[0, :, cs].astype(F32)
        o_ref[0, :, cs] = (y * (gate * _sigmoid(gate)) * ms_ref[:, cs]).astype(BF16)


def _retention(proj, cos, sin, merge_scale, heads, col0, chunk):
    b, s, _ = proj.shape
    w = heads * HEAD_DIM
    assert (col0 * HEAD_DIM) % w == 0
    blk0 = col0 * HEAD_DIM // w
    grp = lambda base: pl.BlockSpec((1, chunk, w), lambda i, c: (i, c, blk0 + base))
    tok = pl.BlockSpec((1, chunk, HEAD_DIM), lambda i, c: (i, c, 0))
    return pl.pallas_call(
        functools.partial(_retention_kernel, heads=heads),
        grid=(b, s // chunk),
        in_specs=[grp(0), grp(1), grp(2), grp(3), tok, tok,
                  pl.BlockSpec((1, w), lambda i, c: (0, 0))],
        out_specs=pl.BlockSpec((1, chunk, w), lambda i, c: (i, c, 0)),
        out_shape=jax.ShapeDtypeStruct((b, s, w), BF16),
        scratch_shapes=[pltpu.VMEM((heads, HEAD_DIM, HEAD_DIM), F32),
                        pltpu.VMEM((heads, chunk, chunk), F32)],
        compiler_params=_params("arbitrary", "arbitrary"),
        name="retention",
    )(proj, proj, proj, proj, cos, sin, merge_scale)


def _fox_keys_kernel(k_ref, gp_ref, o_ref, *, tk, heads):
    ts = k_ref.shape[1]
    c = gp_ref[0]
    parts = []
    for i in range(ts // tk):
        blk = c[i * tk:(i + 1) * tk]
        parts.append((blk[0:1, :] - blk) * LOG2E)
    x = jnp.concatenate(parts, axis=0) if len(parts) > 1 else parts[0]
    hi = x.astype(BF16)
    r1 = x - hi.astype(F32)
    mid = r1.astype(BF16)
    lo = (r1 - mid.astype(F32)).astype(BF16)
    row = lax.broadcasted_iota(jnp.int32, (GATE_LANES, HEAD_DIM), 0)
    col = lax.broadcasted_iota(jnp.int32, (GATE_LANES, HEAD_DIM), 1)
    for h in range(heads):
        sel = lambda piece: jnp.where((row == h) & (col == piece), 1.0, 0.0).astype(BF16)
        bias = _dot(hi, sel(0)) + _dot(mid, sel(1)) + _dot(lo, sel(2))
        o_ref[0, h] = jnp.concatenate([k_ref[0, :, h * HEAD_DIM:(h + 1) * HEAD_DIM], bias.astype(BF16)], axis=1)


def _fox_keys(proj, gp, heads, k_col0, tk):
    b, s, _ = proj.shape
    ts = min(1024, s)
    w = heads * HEAD_DIM
    assert (k_col0 * HEAD_DIM) % w == 0 and ts % tk == 0
    return pl.pallas_call(
        functools.partial(_fox_keys_kernel, tk=tk, heads=heads),
        grid=(b, s // ts),
        in_specs=[pl.BlockSpec((1, ts, w), lambda i, j: (i, j, k_col0 * HEAD_DIM // w)),
                  pl.BlockSpec((1, ts, GATE_LANES), lambda i, j: (i, j, 0))],
        out_specs=pl.BlockSpec((1, heads, ts, 2 * HEAD_DIM), lambda i, j: (i, 0, j, 0)),
        out_shape=jax.ShapeDtypeStruct((b, heads, s, 2 * HEAD_DIM), BF16),
        compiler_params=_params("arbitrary", "arbitrary"),
        name="fox_keys",
    )(proj, gp)


def _fox_kernel(q_ref, k_ref, v_ref, c_ref, ms_ref, o_ref, s_scr, p_scr, al_scr, m_scr, l_scr, acc_scr,
                *, tq, tk, unroll):
    ndiag = tq // tk
    static_slots = unroll % 2 == 0 and ndiag % 2 == 0
    qi = pl.program_id(2)
    nfull = qi * ndiag
    q0 = pl.multiple_of(qi * tq, tq)
    q = (q_ref[0].astype(F32) * (HEAD_DIM ** -0.5 * LOG2E)).astype(BF16)
    lane = lax.broadcasted_iota(jnp.int32, (tq, HEAD_DIM), 1)
    q_aug = jnp.concatenate([q, jnp.where(lane < 3, 1.0, 0.0).astype(BF16)], axis=1)
    c_first = c_ref[0, :, pl.ds(q0, 128)][:, 0:1]

    m_scr[...] = jnp.full_like(m_scr, -jnp.inf)
    l_scr[...] = jnp.zeros_like(l_scr)
    acc_scr[...] = jnp.zeros_like(acc_scr)
    p_scr[1] = jnp.zeros((tk, tq), BF16)
    al_scr[1] = jnp.ones((1, tq), F32)

    def scores(j, slot):
        k0 = pl.multiple_of(j * tk, tk)
        s_scr[slot] = _dot_nt(k_ref[0, 0, pl.ds(k0, tk), :], q_aug)

    def softmax(j, slot, diag):
        k0 = pl.multiple_of(j * tk, tk)
        delta = (c_first - c_ref[0, :, pl.ds(k0, 128)][:, 0:1]) * LOG2E
        t = s_scr[slot]
        if diag is not None:
            key = lax.broadcasted_iota(jnp.int32, (tk, tq), 0) + diag * tk
            qry = lax.broadcasted_iota(jnp.int32, (tk, tq), 1)
            t = jnp.where(qry >= key, t, -jnp.inf)
        m_prev = m_scr[...]
        m_new = jnp.maximum(m_prev, jnp.max(t, axis=0, keepdims=True) + delta)
        p = jnp.exp2(t - (m_new - delta)).astype(BF16)
        p_scr[slot] = p
        alpha = jnp.exp2(m_prev - m_new)
        half = p[:tk // 2] + p[tk // 2:]
        quarter = half[:tk // 4] + half[tk // 4:]
        l_scr[...] = alpha * l_scr[...] + jnp.sum(quarter.astype(F32), axis=0, keepdims=True)
        m_scr[...] = m_new
        al_scr[slot] = alpha

    def values(j, slot):
        k0 = pl.multiple_of(j * tk, tk)
        acc_scr[...] = al_scr[slot] * acc_scr[...] + _dot_tn(v_ref[0, pl.ds(k0, tk), :], p_scr[slot])

    def step(j, slot, diag, last):
        if not last:
            scores(j + 1, 1 - slot)
        softmax(j, slot, diag)
        values(jnp.maximum(j - 1, 0), 1 - slot)

    scores(0, 0)

    def body(i, carry):
        for u in range(unroll):
            j = i * unroll + u
            step(j, u % 2 if static_slots else j & 1, None, False)
        return carry

    lax.fori_loop(0, nfull // unroll, body, 0)
    for d in range(ndiag):
        j = nfull + d
        slot = d % 2 if ndiag % 2 == 0 else j & 1
        step(j, slot, d, d == ndiag - 1)
    values(j, slot)

    out = (acc_scr[...] / l_scr[...]).T
    o_ref[0] = (_head_norm(out) * ms_ref[...]).astype(BF16)


def _fox(proj, k_aug, c_rows, merge_scale, heads, col0, ms_col0, tq, tk, unroll):
    b, s, _ = proj.shape
    assert tq % tk == 0 and (tq // tk) % unroll == 0
    return pl.pallas_call(
        functools.partial(_fox_kernel, tq=tq, tk=tk, unroll=unroll),
        grid=(b, heads, s // tq),
        in_specs=[pl.BlockSpec((1, tq, HEAD_DIM), lambda i, h, q: (i, q, col0 + h)),
                  pl.BlockSpec((1, 1, s, 2 * HEAD_DIM), lambda i, h, q: (i, h, 0, 0)),
                  pl.BlockSpec((1, s, HEAD_DIM), lambda i, h, q: (i, 0, col0 + 2 * heads + h)),
                  pl.BlockSpec((1, 1, s), lambda i, h, q: (i * heads + h, 0, 0)),
                  pl.BlockSpec((1, HEAD_DIM), lambda i, h, q: (0, ms_col0 + h))],
        out_specs=pl.BlockSpec((1, tq, HEAD_DIM), lambda i, h, q: (i, q, h)),
        out_shape=jax.ShapeDtypeStruct((b, s, heads * HEAD_DIM), BF16),
        scratch_shapes=[pltpu.VMEM((2, tk, tq), F32),
                        pltpu.VMEM((2, tk, tq), BF16),
                        pltpu.VMEM((2, 1, tq), F32),
                        pltpu.VMEM((1, tq), F32),
                        pltpu.VMEM((1, tq), F32),
                        pltpu.VMEM((HEAD_DIM, tq), F32)],
        compiler_params=_params("arbitrary", "arbitrary", "arbitrary"),
        name="fox_attention",
    )(proj, k_aug, proj, c_rows, merge_scale)


def _mlstm_kernel(q_ref, k_ref, v_ref, og_ref, w_ref, b_ref, gp_ref, ir_ref, gr_ref, ms_ref, o_ref,
                  c_scr, n_scr, m_scr, g0_scr, xq_scr, xk_scr, *, heads, i_lane0, f_lane0):
    n = q_ref.shape[1]
    w = heads * HEAD_DIM

    @pl.when(pl.program_id(1) == 0)
    def _():
        c_scr[...] = jnp.zeros_like(c_scr)
        n_scr[...] = jnp.zeros_like(n_scr)
        m_scr[...] = jnp.zeros_like(m_scr)
        g0_scr[...] = jnp.zeros_like(g0_scr)
        xq_scr[0:CONV_HALO, :] = jnp.zeros((CONV_HALO, w), F32)
        xk_scr[0:CONV_HALO, :] = jnp.zeros((CONV_HALO, w), F32)

    def conv_silu(x_ref, x_scr, cols):
        x_scr[CONV_HALO:CONV_HALO + n, :] = x_ref[0].astype(F32)
        y = b_ref[:, cols] + jnp.zeros((n, w), F32)
        for j in range(CONV_WIDTH):
            shift = CONV_WIDTH - 1 - j
            y = y + w_ref[j:j + 1, cols] * x_scr[CONV_HALO - shift:CONV_HALO - shift + n, :]
        x_scr[0:CONV_HALO, :] = x_scr[n:n + CONV_HALO, :]
        return y * _sigmoid(y)

    q_all = conv_silu(q_ref, xq_scr, slice(0, w))
    k_all = conv_silu(k_ref, xk_scr, slice(w, 2 * w)) * (HEAD_DIM ** -0.5)
    gp = gp_ref[0]
    row = lax.broadcasted_iota(jnp.int32, (n, n), 0)
    col = lax.broadcasted_iota(jnp.int32, (n, n), 1)
    causal = row >= col

    for h in range(heads):
        cs = slice(h * HEAD_DIM, (h + 1) * HEAD_DIM)
        q = q_all[:, cs]
        k = k_all[:, cs]
        v = v_ref[0, :, cs]
        ic = gp[:, i_lane0 + h:i_lane0 + h + 1]
        gc = gp[:, f_lane0 + h:f_lane0 + h + 1]
        ir = ir_ref[0, h:h + 1, :]
        gr = gr_ref[0, h:h + 1, :]

        g0 = g0_scr[h][:, 0:1]
        m_prev = m_scr[h][:, 0:1]
        b_col = gc - g0
        dmat = jnp.where(causal, gc + (ir - gr), -jnp.inf)
        inter_log = b_col + m_prev
        m_q = jnp.maximum(inter_log, jnp.max(dmat, axis=-1, keepdims=True))
        w_intra = jnp.exp(dmat - m_q)
        w_inter = jnp.exp(inter_log - m_q)

        qb = q.astype(BF16)
        s = _dot_nt(qb, k.astype(BF16)) * w_intra
        cst = c_scr[h]
        nst = n_scr[h]
        num = _dot(s.astype(BF16), v) + w_inter * _dot(qb, cst.astype(BF16))
        den = jnp.sum(s, axis=-1, keepdims=True) + w_inter * jnp.sum(q * nst, axis=-1, keepdims=True)
        hid = num / jnp.maximum(jnp.abs(den), jnp.exp(-m_q))

        b_last = b_col[n - 1:n, :]
        k_log = b_last - b_col + ic
        m_new = jnp.maximum(b_last + m_prev, jnp.max(k_log, axis=0, keepdims=True))
        wk = jnp.exp(k_log - m_new)
        carry_scale = jnp.exp(b_last + m_prev - m_new)
        kw = k * wk
        c_scr[h] = carry_scale * cst + _dot_tn(kw.astype(BF16), v)
        n_scr[h] = carry_scale * nst + jnp.sum(kw, axis=0, keepdims=True)
        m_scr[h] = jnp.broadcast_to(m_new, (1, HEAD_DIM))
        g0_scr[h] = jnp.broadcast_to(gc[n - 1:n, :], (1, HEAD_DIM))

        og = og_ref[0, :, cs].astype(F32)
        o_ref[0, :, cs] = (_head_norm(hid) * _sigmoid(og) * ms_ref[:, cs]).astype(BF16)


def _mlstm(proj, conv_w, conv_b, gp, i_rows, g_rows, merge_scale, heads, col0, ms_col0, i_lane0, f_lane0, chunk):
    b, s, _ = proj.shape
    w = heads * HEAD_DIM
    assert (col0 * HEAD_DIM) % w == 0 and (ms_col0 * HEAD_DIM) % w == 0
    blk0 = col0 * HEAD_DIM // w
    grp = lambda base: pl.BlockSpec((1, chunk, w), lambda i, c: (i, c, blk0 + base))
    row_spec = pl.BlockSpec((1, heads, chunk), lambda i, c: (i, 0, c))
    state = lambda r, c: pltpu.VMEM((heads, r, c), F32)
    return pl.pallas_call(
        functools.partial(_mlstm_kernel, heads=heads, i_lane0=i_lane0, f_lane0=f_lane0),
        grid=(b, s // chunk),
        in_specs=[grp(0), grp(1), grp(2), grp(3),
                  pl.BlockSpec(conv_w.shape, lambda i, c: (0, 0)),
                  pl.BlockSpec(conv_b.shape, lambda i, c: (0, 0)),
                  pl.BlockSpec((1, chunk, GATE_LANES), lambda i, c: (i, c, 0)),
                  row_spec, row_spec,
                  pl.BlockSpec((1, w), lambda i, c: (0, ms_col0 * HEAD_DIM // w))],
        out_specs=pl.BlockSpec((1, chunk, w), lambda i, c: (i, c, 0)),
        out_shape=jax.ShapeDtypeStruct((b, s, w), BF16),
        scratch_shapes=[state(HEAD_DIM, HEAD_DIM), state(1, HEAD_DIM), state(1, HEAD_DIM), state(1, HEAD_DIM),
                        pltpu.VMEM((chunk + CONV_HALO, w), F32),
                        pltpu.VMEM((chunk + CONV_HALO, w), F32)],
        compiler_params=_params("arbitrary", "arbitrary"),
        name="mlstm",
    )(proj, proj, proj, proj, conv_w, conv_b, gp, i_rows, g_rows, merge_scale)


def _outproj_kernel(yr_ref, yf_ref, ym_ref, w_ref, x_ref, ga_ref, o_ref):
    r = yr_ref.shape[1]
    f = yf_ref.shape[1]
    y = _dot(yr_ref[...], w_ref[0:r, :]) + _dot(yf_ref[...], w_ref[r:r + f, :])
    y = y + _dot(ym_ref[...], w_ref[r + f:, :])
    o_ref[...] = x_ref[...] + ga_ref[0] * y


def _outproj(y_ret, y_fox, y_m, w_out, x2, g_a, seq):
    m, d = x2.shape
    tm = min(512, seq)
    bpr = seq // tm
    rows = lambda w: pl.BlockSpec((tm, w), lambda i: (i, 0))
    return pl.pallas_call(
        _outproj_kernel,
        grid=(m // tm,),
        in_specs=[rows(y_ret.shape[1]), rows(y_fox.shape[1]), rows(y_m.shape[1]),
                  pl.BlockSpec(w_out.shape, lambda i: (0, 0)),
                  rows(d),
                  pl.BlockSpec((1, 1, d), lambda i: (i // bpr, 0, 0))],
        out_specs=rows(d),
        out_shape=jax.ShapeDtypeStruct((m, d), F32),
        compiler_params=_params("arbitrary"),
        name="outproj",
    )(y_ret, y_fox, y_m, w_out, x2, g_a)


def _mlp_kernel(x_ref, nw_ref, sc_ref, sh_ref, gm_ref, w1_ref, w2_ref, fw_ref, o_ref, h_scr, acc_scr,
                *, final_norm):
    f = pl.program_id(1)

    @pl.when(f == 0)
    def _():
        _modulated_norm_rows(x_ref, nw_ref, sc_ref, sh_ref, h_scr)
        acc_scr[...] = jnp.zeros_like(acc_scr)

    a = jnp.maximum(_dot(h_scr[...], w1_ref[...]), 0.0)
    acc_scr[...] += _dot((a * a).astype(BF16), w2_ref[...])

    @pl.when(f == pl.num_programs(1) - 1)
    def _():
        y = x_ref[...] + gm_ref[0] * acc_scr[...]
        if final_norm:
            ms = jnp.mean(y * y, axis=-1, keepdims=True)
            y = y * lax.rsqrt(ms + NORM_EPS) * fw_ref[...]
        o_ref[...] = y


def _mlp(x2, nw, sc, sh, g_m, w1, w2, final_w, seq, final_norm):
    m, d = x2.shape
    dff = w1.shape[1]
    tm = min(512, seq)
    tf = min(1024, dff)
    bpr = seq // tm
    mod = pl.BlockSpec((1, 1, d), lambda i, f: (i // bpr, 0, 0))
    vec = pl.BlockSpec((1, d), lambda i, f: (0, 0))
    return pl.pallas_call(
        functools.partial(_mlp_kernel, final_norm=final_norm),
        grid=(m // tm, dff // tf),
        in_specs=[pl.BlockSpec((tm, d), lambda i, f: (i, 0)),
                  vec, mod, mod, mod,
                  pl.BlockSpec((d, tf), lambda i, f: (0, f)),
                  pl.BlockSpec((tf, d), lambda i, f: (f, 0)),
                  vec],
        out_specs=pl.BlockSpec((tm, d), lambda i, f: (i, 0)),
        out_shape=jax.ShapeDtypeStruct((m, d), F32),
        scratch_shapes=[pltpu.VMEM((tm, d), BF16), pltpu.VMEM((tm, d), F32)],
        compiler_params=_params("arbitrary", "arbitrary"),
        name="mlp",
    )(x2, nw, sc, sh, g_m, w1, w2, final_w)


def kernel(x, c, positions, ada_w, ada_b, norm_mix_w, norm_mlp_w, w_in, conv_w, conv_b, fox_f_bias,
           mlstm_i_bias, mlstm_f_bias, merge_scale, w_out, w_ff1, w_ff2, final_norm_w):
    b, s, d = x.shape
    depth = ada_w.shape[0]
    ret_w, fox_w, ml_w = d // 4, d // 2, d // 4
    ret_h, fox_h, ml_h = ret_w // HEAD_DIM, fox_w // HEAD_DIM, ml_w // HEAD_DIM
    assert s % 128 == 0 and d % (4 * HEAD_DIM) == 0
    assert fox_h + 2 * ml_h <= GATE_LANES

    sizes = [ret_w] * 4 + [fox_w] * 3 + [fox_h] + [ml_w] * 4 + [ml_h, ml_h]
    offs = [0]
    for sz in sizes:
        offs.append(offs[-1] + sz)
    n_gate = fox_h + 2 * ml_h
    ret_col0 = 0
    fox_col0 = (4 * ret_w) // HEAD_DIM
    ml_col0 = (4 * ret_w + 3 * fox_w) // HEAD_DIM
    i_lane0, f_lane0 = fox_h, fox_h + ml_h

    chunk = min(256, s)
    tq = min(FOX_TQ, s)
    tk = min(FOX_TK, tq)

    mod = _ada_mod(c, ada_w, ada_b)
    cos, sin = _rope_tables(positions)
    cum_mask = jnp.zeros((1, GATE_LANES), F32).at[0, :fox_h].set(1.0).at[0, f_lane0:f_lane0 + ml_h].set(1.0)

    x2 = x.reshape(b * s, d)
    for layer in range(depth):
        sh_a, sc_a, g_a, sh_m, sc_m, g_m = [t.reshape(b, 1, d) for t in jnp.split(mod[layer], N_MOD, axis=-1)]
        wl = w_in[layer]
        w_main = jnp.concatenate([wl[:, offs[0]:offs[7]], wl[:, offs[8]:offs[12]]], axis=1).astype(BF16)
        w_gate = jnp.concatenate([wl[:, offs[7]:offs[8]], wl[:, offs[12]:offs[14]],
                                  jnp.zeros((d, GATE_LANES - n_gate), F32)], axis=1).astype(BF16)
        gate_bias = jnp.pad(jnp.concatenate([fox_f_bias[layer], mlstm_i_bias[layer], mlstm_f_bias[layer]]),
                            (0, GATE_LANES - n_gate)).reshape(1, GATE_LANES)
        ms = merge_scale[layer].reshape(1, d)

        proj, gates = _inproj(x2, norm_mix_w[layer].reshape(1, d), sc_a, sh_a, w_main, w_gate, s)
        proj = proj.reshape(b, s, -1)
        gp = _gates(gates.reshape(b, s, GATE_LANES), gate_bias, cum_mask)
        gp_t = jnp.swapaxes(gp[:, :, :n_gate], 1, 2)
        c_rows = gp_t[:, :fox_h].reshape(b * fox_h, 1, s)
        i_rows = gp_t[:, i_lane0:i_lane0 + ml_h]
        g_rows = gp_t[:, f_lane0:f_lane0 + ml_h]

        y_ret = _retention(proj, cos, sin, ms, ret_h, ret_col0, chunk)
        k_aug = _fox_keys(proj, gp, fox_h, fox_col0 + fox_h, tk)
        y_fox = _fox(proj, k_aug, c_rows, ms, fox_h, fox_col0, ret_h, tq, tk, FOX_UNROLL if tq // tk % FOX_UNROLL == 0 else 1)
        y_m = _mlstm(proj, conv_w[layer], conv_b[layer].reshape(1, -1), gp, i_rows, g_rows, ms,
                     ml_h, ml_col0, ret_h + fox_h, i_lane0, f_lane0, chunk)

        x2 = _outproj(y_ret.reshape(b * s, -1), y_fox.reshape(b * s, -1), y_m.reshape(b * s, -1),
                      w_out[layer].astype(BF16), x2, g_a, s)
        x2 = _mlp(x2, norm_mlp_w[layer].reshape(1, d), sc_m, sh_m, g_m,
                  w_ff1[layer].astype(BF16), w_ff2[layer].astype(BF16),
                  final_norm_w.reshape(1, d), s, layer == depth - 1)
    return x2.reshape(b, s, d)
```

```python
import functools
import math

import jax
import jax.numpy as jnp
from jax import lax
from jax.experimental import pallas as pl
from jax.experimental.pallas import tpu as pltpu

HEAD_DIM = 128
CONV_WIDTH = 4
ROPE_BASE = 10000.0
NORM_EPS = 1e-6
N_MOD = 6
GATE_LANES = 128
LOG2E = math.log2(math.e)
FOX_TQ, FOX_TK, FOX_UNROLL = 1024, 512, 2
RET_CHUNK, MLSTM_CHUNK = 256, 512
CONV_HALO = 8

F32 = jnp.float32
BF16 = jnp.bfloat16

VMEM_LIMIT_BYTES = 56 * 1024 * 1024


def _params(*sem, flags=None):
    return pltpu.CompilerParams(dimension_semantics=sem, vmem_limit_bytes=VMEM_LIMIT_BYTES, flags=flags)


def _dot(a, b):
    return jnp.dot(a, b, preferred_element_type=F32)


def _dot_nt(a, b):
    return lax.dot_general(a, b, (((1,), (1,)), ((), ())), preferred_element_type=F32)


def _dot_tn(a, b):
    return lax.dot_general(a, b, (((0,), (0,)), ((), ())), preferred_element_type=F32)


def _sigmoid(x):
    return 1.0 / (1.0 + jnp.exp(-x))


def _log_sigmoid(x):
    return jnp.minimum(x, 0.0) - jnp.log1p(jnp.exp(-jnp.abs(x)))


def _head_norm(y):
    mu = jnp.mean(y, axis=-1, keepdims=True)
    yc = y - mu
    var = jnp.mean(yc * yc, axis=-1, keepdims=True)
    return yc * lax.rsqrt(var + NORM_EPS)


def _ada_kernel(ct_ref, w_ref, b_ref, o_ref, *, kc):
    d, nb = ct_ref.shape
    tn = w_ref.shape[2]

    def body(i, accs):
        r0 = pl.multiple_of(i * kc, kc)
        ck = ct_ref[pl.ds(r0, kc), :]
        ck = ck * _sigmoid(ck)
        wk = w_ref[0, pl.ds(r0, kc), :]
        return tuple(acc + jnp.sum(wk * ck[:, b:b + 1], axis=0, keepdims=True)
                     for b, acc in enumerate(accs))

    accs = lax.fori_loop(0, d // kc, body, tuple(jnp.zeros((1, tn), F32) for _ in range(nb)))
    o_ref[0] = jnp.concatenate(accs, axis=0) + b_ref[0]


def _ada_mod(c, ada_w, ada_b):
    depth, d, n = ada_w.shape
    nb = c.shape[0]
    tn = min(1024, n)
    kc = min(256, d)
    return pl.pallas_call(
        functools.partial(_ada_kernel, kc=kc),
        grid=(depth, n // tn),
        in_specs=[pl.BlockSpec((d, nb), lambda l, j: (0, 0)),
                  pl.BlockSpec((1, d, tn), lambda l, j: (l, 0, j)),
                  pl.BlockSpec((1, 1, tn), lambda l, j: (l, 0, j))],
        out_specs=pl.BlockSpec((1, nb, tn), lambda l, j: (l, 0, j)),
        out_shape=jax.ShapeDtypeStruct((depth, nb, n), F32),
        compiler_params=_params("arbitrary", "arbitrary"),
        name="ada_mod",
    )(c.T, ada_w, ada_b.reshape(depth, 1, n))


def _rope_kernel(pos_ref, invf_ref, sign_ref, cos_ref, sin_ref):
    ang = pos_ref[0] * invf_ref[...]
    cos_ref[0] = jnp.cos(ang)
    sin_ref[0] = jnp.sin(ang) * sign_ref[...]


def _rope_tables(positions):
    b, s = positions.shape
    ts = min(2048, s)
    half = HEAD_DIM // 2
    inv_freq = ROPE_BASE ** (-jnp.arange(0, HEAD_DIM, 2, dtype=F32) / HEAD_DIM)
    invf = jnp.concatenate([inv_freq, inv_freq]).reshape(1, HEAD_DIM)
    sign = jnp.concatenate([-jnp.ones((half,), F32), jnp.ones((half,), F32)]).reshape(1, HEAD_DIM)
    pos = positions.astype(F32).reshape(b, s, 1)
    out = jax.ShapeDtypeStruct((b, s, HEAD_DIM), F32)
    return pl.pallas_call(
        _rope_kernel,
        grid=(b, s // ts),
        in_specs=[pl.BlockSpec((1, ts, 1), lambda i, j: (i, j, 0)),
                  pl.BlockSpec((1, HEAD_DIM), lambda i, j: (0, 0)),
                  pl.BlockSpec((1, HEAD_DIM), lambda i, j: (0, 0))],
        out_specs=[pl.BlockSpec((1, ts, HEAD_DIM), lambda i, j: (i, j, 0))] * 2,
        out_shape=[out, out],
        compiler_params=_params("arbitrary", "arbitrary"),
        name="rope_tables",
    )(pos, invf, sign)


def _modulated_norm_rows(x_ref, nw_ref, sc_ref, sh_ref, h_scr):
    x = x_ref[...]
    inv = lax.rsqrt(jnp.mean(x * x, axis=-1, keepdims=True) + NORM_EPS)
    h_scr[...] = ((x_ref[...] * inv * nw_ref[...]) * (1.0 + sc_ref[0]) + sh_ref[0]).astype(BF16)


def _inproj_kernel(x_ref, nw_ref, sc_ref, sh_ref, w_ref, wg_ref, o_ref, g_ref, h_scr):
    @pl.when(pl.program_id(1) == 0)
    def _():
        _modulated_norm_rows(x_ref, nw_ref, sc_ref, sh_ref, h_scr)
        g_ref[...] = _dot(h_scr[...], wg_ref[...])

    o_ref[...] = _dot(h_scr[...], w_ref[...]).astype(BF16)


def _inproj(x2, nw, sc, sh, w_main, w_gate, seq):
    m, d = x2.shape
    nm = w_main.shape[1]
    tm = min(1024, seq)
    tn = 1024 if nm % 1024 == 0 else 512
    bpr = seq // tm
    return pl.pallas_call(
        _inproj_kernel,
        grid=(m // tm, nm // tn),
        in_specs=[pl.BlockSpec((tm, d), lambda i, j: (i, 0)),
                  pl.BlockSpec((1, d), lambda i, j: (0, 0)),
                  pl.BlockSpec((1, 1, d), lambda i, j: (i // bpr, 0, 0)),
                  pl.BlockSpec((1, 1, d), lambda i, j: (i // bpr, 0, 0)),
                  pl.BlockSpec((d, tn), lambda i, j: (0, j)),
                  pl.BlockSpec((d, GATE_LANES), lambda i, j: (0, 0))],
        out_specs=[pl.BlockSpec((tm, tn), lambda i, j: (i, j)),
                   pl.BlockSpec((tm, GATE_LANES), lambda i, j: (i, 0))],
        out_shape=[jax.ShapeDtypeStruct((m, nm), BF16),
                   jax.ShapeDtypeStruct((m, GATE_LANES), F32)],
        scratch_shapes=[pltpu.VMEM((tm, d), BF16)],
        compiler_params=_params("arbitrary", "arbitrary"),
        name="inproj",
    )(x2, nw, sc, sh, w_main, w_gate)


def _gates_kernel(g_ref, bias_ref, cum_ref, o_ref, carry):
    @pl.when(pl.program_id(1) == 0)
    def _():
        carry[...] = jnp.zeros_like(carry)

    g = g_ref[0] + bias_ref[...]
    ls = _log_sigmoid(g)
    ts = ls.shape[0]
    row = lax.broadcasted_iota(jnp.int32, (ts, ts), 0)
    col = lax.broadcasted_iota(jnp.int32, (ts, ts), 1)
    tri = jnp.where(row >= col, 1.0, 0.0).astype(BF16)
    hi = ls.astype(BF16)
    r1 = ls - hi.astype(F32)
    mid = r1.astype(BF16)
    lo = (r1 - mid.astype(F32)).astype(BF16)
    csum = (_dot(tri, hi) + _dot(tri, mid)) + _dot(tri, lo) + carry[...]
    carry[...] = csum[ts - 1:ts, :]
    o_ref[0] = jnp.where(cum_ref[...] > 0.5, csum, g)


def _gates(g3, bias, cum_mask):
    b, s, n = g3.shape
    ts = min(512, s)
    return pl.pallas_call(
        _gates_kernel,
        grid=(b, s // ts),
        in_specs=[pl.BlockSpec((1, ts, n), lambda i, j: (i, j, 0)),
                  pl.BlockSpec((1, n), lambda i, j: (0, 0)),
                  pl.BlockSpec((1, n), lambda i, j: (0, 0))],
        out_specs=pl.BlockSpec((1, ts, n), lambda i, j: (i, j, 0)),
        out_shape=jax.ShapeDtypeStruct((b, s, n), F32),
        scratch_shapes=[pltpu.VMEM((1, n), F32)],
        compiler_params=_params("arbitrary", "arbitrary"),
        name="gates",
    )(g3, bias, cum_mask)


def _retention_kernel(q_ref, k_ref, v_ref, g_ref, cos_ref, sin_ref, ms_ref, o_ref, state, decay, *, heads):
    n = cos_ref.shape[1]
    log_gamma = [math.log(1.0 - 2.0 ** (-5.0 - h)) for h in range(heads)]

    @pl.when(pl.program_id(1) == 0)
    def _():
        state[...] = jnp.zeros_like(state)
        row = lax.broadcasted_iota(jnp.int32, (n, n), 0)
        col = lax.broadcasted_iota(jnp.int32, (n, n), 1)
        rel = (row - col).astype(F32)
        for h in range(heads):
            decay[h] = jnp.where(rel >= 0, jnp.exp(jnp.maximum(rel, 0.0) * log_gamma[h]), 0.0)

    cos = cos_ref[0]
    sin = sin_ref[0]
    half = HEAD_DIM // 2
    idx = lax.broadcasted_iota(jnp.int32, (n, 1), 0).astype(F32)

    def rot(t):
        return t * cos + pltpu.roll(t, half, 1) * sin

    for h in range(heads):
        lg = log_gamma[h]
        cs = slice(h * HEAD_DIM, (h + 1) * HEAD_DIM)
        q = rot(q_ref[0, :, cs].astype(F32))
        k = rot(k_ref[0, :, cs].astype(F32)) * (HEAD_DIM ** -0.5)
        v = v_ref[0, :, cs]
        q_decay = jnp.exp((idx + 1.0) * lg)
        k_decay = jnp.exp((n - 1.0 - idx) * lg)
        chunk_decay = math.exp(n * lg)

        scores = _dot_nt(q.astype(BF16), k.astype(BF16)) * decay[h]
        intra = _dot(scores.astype(BF16), v)
        st = state[h]
        inter = _dot((q * q_decay).astype(BF16), st.astype(BF16))
        state[h] = st * chunk_decay + _dot_tn((k * k_decay).astype(BF16), v)

        y = _head_norm(intra + inter)
        gate = g_ref[0, :, cs].astype(F32)
        o_ref[0, :, cs] = (y * (gate * _sigmoid(gate)) * ms_ref[:, cs]).astype(BF16)


def _retention(proj, cos, sin, merge_scale, heads, col0, chunk):
    b, s, _ = proj.shape
    w = heads * HEAD_DIM
    assert (col0 * HEAD_DIM) % w == 0
    blk0 = col0 * HEAD_DIM // w
    grp = lambda base: pl.BlockSpec((1, chunk, w), lambda i, c: (i, c, blk0 + base))
    tok = pl.BlockSpec((1, chunk, HEAD_DIM), lambda i, c: (i, c, 0))
    return pl.pallas_call(
        functools.partial(_retention_kernel, heads=heads),
        grid=(b, s // chunk),
        in_specs=[grp(0), grp(1), grp(2), grp(3), tok, tok,
                  pl.BlockSpec((1, w), lambda i, c: (0, 0))],
        out_specs=pl.BlockSpec((1, chunk, w), lambda i, c: (i, c, 0)),
        out_shape=jax.ShapeDtypeStruct((b, s, w), BF16),
        scratch_shapes=[pltpu.VMEM((heads, HEAD_DIM, HEAD_DIM), F32),
                        pltpu.VMEM((heads, chunk, chunk), F32)],
        compiler_params=_params("arbitrary", "arbitrary"),
        name="retention",
    )(proj, proj, proj, proj, cos, sin, merge_scale)


def _fox_keys_kernel(k_ref, gp_ref, o_ref, *, tk, heads):
    ts = k_ref.shape[1]
    c = gp_ref[0]
    parts = []
    for i in range(ts // tk):
        blk = c[i * tk:(i + 1) * tk]
        parts.append((blk[0:1, :] - blk) * LOG2E)
    x = jnp.concatenate(parts, axis=0) if len(parts) > 1 else parts[0]
    hi = x.astype(BF16)
    r1 = x - hi.astype(F32)
    mid = r1.astype(BF16)
    lo = (r1 - mid.astype(F32)).astype(BF16)
    row = lax.broadcasted_iota(jnp.int32, (GATE_LANES, HEAD_DIM), 0)
    col = lax.broadcasted_iota(jnp.int32, (GATE_LANES, HEAD_DIM), 1)
    for h in range(heads):
        sel = lambda piece: jnp.where((row == h) & (col == piece), 1.0, 0.0).astype(BF16)
        bias = _dot(hi, sel(0)) + _dot(mid, sel(1)) + _dot(lo, sel(2))
        o_ref[0, h] = jnp.concatenate([k_ref[0, :, h * HEAD_DIM:(h + 1) * HEAD_DIM], bias.astype(BF16)], axis=1)


def _fox_keys(proj, gp, heads, k_col0, tk):
    b, s, _ = proj.shape
    ts = min(1024, s)
    w = heads * HEAD_DIM
    assert (k_col0 * HEAD_DIM) % w == 0 and ts % tk == 0
    return pl.pallas_call(
        functools.partial(_fox_keys_kernel, tk=tk, heads=heads),
        grid=(b, s // ts),
        in_specs=[pl.BlockSpec((1, ts, w), lambda i, j: (i, j, k_col0 * HEAD_DIM // w)),
                  pl.BlockSpec((1, ts, GATE_LANES), lambda i, j: (i, j, 0))],
        out_specs=pl.BlockSpec((1, heads, ts, 2 * HEAD_DIM), lambda i, j: (i, 0, j, 0)),
        out_shape=jax.ShapeDtypeStruct((b, heads, s, 2 * HEAD_DIM), BF16),
        compiler_params=_params("arbitrary", "arbitrary"),
        name="fox_keys",
    )(proj, gp)


def _fox_kernel(q_ref, k_ref, v_ref, c_ref, ms_ref, o_ref, s_scr, m_scr, l_scr, acc_scr,
                *, tq, tk, unroll):
    ndiag = tq // tk
    static_slots = unroll % 2 == 0 and ndiag % 2 == 0
    qi = pl.program_id(2)
    nfull = qi * ndiag
    q0 = pl.multiple_of(qi * tq, tq)
    q = (q_ref[0].astype(F32) * (HEAD_DIM ** -0.5 * LOG2E)).astype(BF16)
    lane = lax.broadcasted_iota(jnp.int32, (tq, HEAD_DIM), 1)
    q_aug = jnp.concatenate([q, jnp.where(lane < 3, 1.0, 0.0).astype(BF16)], axis=1)
    c_first = c_ref[0, :, pl.ds(q0, 128)][:, 0:1]

    m_scr[...] = jnp.full_like(m_scr, -jnp.inf)
    l_scr[...] = jnp.zeros_like(l_scr)
    acc_scr[...] = jnp.zeros_like(acc_scr)

    def scores(j, slot):
        k0 = pl.multiple_of(j * tk, tk)
        s_scr[slot] = _dot_nt(k_ref[0, 0, pl.ds(k0, tk), :], q_aug)

    def softmax(j, slot, diag):
        k0 = pl.multiple_of(j * tk, tk)
        delta = (c_first - c_ref[0, :, pl.ds(k0, 128)][:, 0:1]) * LOG2E
        t = s_scr[slot]
        if diag is not None:
            key = lax.broadcasted_iota(jnp.int32, (tk, tq), 0) + diag * tk
            qry = lax.broadcasted_iota(jnp.int32, (tk, tq), 1)
            t = jnp.where(qry >= key, t, -jnp.inf)
        m_prev = m_scr[...]
        m_new = jnp.maximum(m_prev, jnp.max(t, axis=0, keepdims=True) + delta)
        p = jnp.exp2(t - (m_new - delta)).astype(BF16)
        alpha = jnp.exp2(m_prev - m_new)
        half = p[:tk // 2] + p[tk // 2:]
        quarter = half[:tk // 4] + half[tk // 4:]
        l_scr[...] = alpha * l_scr[...] + jnp.sum(quarter.astype(F32), axis=0, keepdims=True)
        m_scr[...] = m_new
        acc_scr[...] = alpha * acc_scr[...] + _dot_tn(v_ref[0, pl.ds(k0, tk), :], p)

    def step(j, slot, diag, last):
        if not last:
            scores(j + 1, 1 - slot)
        softmax(j, slot, diag)

    scores(0, 0)

    def body(i, carry):
        for u in range(unroll):
            j = i * unroll + u
            step(j, u % 2 if static_slots else j & 1, None, False)
        return carry

    lax.fori_loop(0, nfull // unroll, body, 0)
    for d in range(ndiag):
        j = nfull + d
        slot = d % 2 if ndiag % 2 == 0 else j & 1
        step(j, slot, d, d == ndiag - 1)

    out = (acc_scr[...] / l_scr[...]).T
    o_ref[0] = (_head_norm(out) * ms_ref[...]).astype(BF16)


def _fox(proj, k_aug, c_rows, merge_scale, heads, col0, ms_col0, tq, tk, unroll):
    b, s, _ = proj.shape
    assert tq % tk == 0 and (tq // tk) % unroll == 0
    return pl.pallas_call(
        functools.partial(_fox_kernel, tq=tq, tk=tk, unroll=unroll),
        grid=(b, heads, s // tq),
        in_specs=[pl.BlockSpec((1, tq, HEAD_DIM), lambda i, h, q: (i, q, col0 + h)),
                  pl.BlockSpec((1, 1, s, 2 * HEAD_DIM), lambda i, h, q: (i, h, 0, 0)),
                  pl.BlockSpec((1, s, HEAD_DIM), lambda i, h, q: (i, 0, col0 + 2 * heads + h)),
                  pl.BlockSpec((1, 1, s), lambda i, h, q: (i * heads + h, 0, 0)),
                  pl.BlockSpec((1, HEAD_DIM), lambda i, h, q: (0, ms_col0 + h))],
        out_specs=pl.BlockSpec((1, tq, HEAD_DIM), lambda i, h, q: (i, q, h)),
        out_shape=jax.ShapeDtypeStruct((b, s, heads * HEAD_DIM), BF16),
        scratch_shapes=[pltpu.VMEM((2, tk, tq), F32),
                        pltpu.VMEM((1, tq), F32),
                        pltpu.VMEM((1, tq), F32),
                        pltpu.VMEM((HEAD_DIM, tq), F32)],
        compiler_params=_params("arbitrary", "arbitrary", "arbitrary"),
        name="fox_attention",
    )(proj, k_aug, proj, c_rows, merge_scale)


def _mlstm_kernel(q_ref, k_ref, v_ref, og_ref, w_ref, b_ref, gp_ref, ir_ref, gr_ref, ms_ref, o_ref,
                  c_scr, n_scr, m_scr, g0_scr, xq_scr, xk_scr, *, heads, i_lane0, f_lane0):
    n = q_ref.shape[1]
    w = heads * HEAD_DIM

    @pl.when(pl.program_id(1) == 0)
    def _():
        c_scr[...] = jnp.zeros_like(c_scr)
        n_scr[...] = jnp.zeros_like(n_scr)
        m_scr[...] = jnp.zeros_like(m_scr)
        g0_scr[...] = jnp.zeros_like(g0_scr)
        xq_scr[0:CONV_HALO, :] = jnp.zeros((CONV_HALO, w), F32)
        xk_scr[0:CONV_HALO, :] = jnp.zeros((CONV_HALO, w), F32)

    def conv_silu(x_ref, x_scr, cols):
        x_scr[CONV_HALO:CONV_HALO + n, :] = x_ref[0].astype(F32)
        y = b_ref[:, cols] + jnp.zeros((n, w), F32)
        for j in range(CONV_WIDTH):
            shift = CONV_WIDTH - 1 - j
            y = y + w_ref[j:j + 1, cols] * x_scr[CONV_HALO - shift:CONV_HALO - shift + n, :]
        x_scr[0:CONV_HALO, :] = x_scr[n:n + CONV_HALO, :]
        return y * _sigmoid(y)

    q_all = conv_silu(q_ref, xq_scr, slice(0, w))
    k_all = conv_silu(k_ref, xk_scr, slice(w, 2 * w)) * (HEAD_DIM ** -0.5)
    gp = gp_ref[0]
    row = lax.broadcasted_iota(jnp.int32, (n, n), 0)
    col = lax.broadcasted_iota(jnp.int32, (n, n), 1)
    causal = row >= col

    for h in range(heads):
        cs = slice(h * HEAD_DIM, (h + 1) * HEAD_DIM)
        q = q_all[:, cs]
        k = k_all[:, cs]
        v = v_ref[0, :, cs]
        ic = gp[:, i_lane0 + h:i_lane0 + h + 1]
        gc = gp[:, f_lane0 + h:f_lane0 + h + 1]
        ir = ir_ref[0, h:h + 1, :]
        gr = gr_ref[0, h:h + 1, :]

        g0 = g0_scr[h][:, 0:1]
        m_prev = m_scr[h][:, 0:1]
        b_col = gc - g0
        dmat = jnp.where(causal, gc + (ir - gr), -jnp.inf)
        inter_log = b_col + m_prev
        m_q = jnp.maximum(inter_log, jnp.max(dmat, axis=-1, keepdims=True))
        w_intra = jnp.exp(dmat - m_q)
        w_inter = jnp.exp(inter_log - m_q)

        qb = q.astype(BF16)
        s = _dot_nt(qb, k.astype(BF16)) * w_intra
        cst = c_scr[h]
        nst = n_scr[h]
        num = _dot(s.astype(BF16), v) + w_inter * _dot(qb, cst.astype(BF16))
        den = jnp.sum(s, axis=-1, keepdims=True) + w_inter * jnp.sum(q * nst, axis=-1, keepdims=True)
        hid = num / jnp.maximum(jnp.abs(den), jnp.exp(-m_q))

        b_last = b_col[n - 1:n, :]
        k_log = b_last - b_col + ic
        m_new = jnp.maximum(b_last + m_prev, jnp.max(k_log, axis=0, keepdims=True))
        wk = jnp.exp(k_log - m_new)
        carry_scale = jnp.exp(b_last + m_prev - m_new)
        kw = k * wk
        c_scr[h] = carry_scale * cst + _dot_tn(kw.astype(BF16), v)
        n_scr[h] = carry_scale * nst + jnp.sum(kw, axis=0, keepdims=True)
        m_scr[h] = jnp.broadcast_to(m_new, (1, HEAD_DIM))
        g0_scr[h] = jnp.broadcast_to(gc[n - 1:n, :], (1, HEAD_DIM))

        og = og_ref[0, :, cs].astype(F32)
        o_ref[0, :, cs] = (_head_norm(hid) * _sigmoid(og) * ms_ref[:, cs]).astype(BF16)


def _mlstm(proj, conv_w, conv_b, gp, i_rows, g_rows, merge_scale, heads, col0, ms_col0, i_lane0, f_lane0, chunk):
    b, s, _ = proj.shape
    w = heads * HEAD_DIM
    assert (col0 * HEAD_DIM) % w == 0 and (ms_col0 * HEAD_DIM) % w == 0
    blk0 = col0 * HEAD_DIM // w
    grp = lambda base: pl.BlockSpec((1, chunk, w), lambda i, c: (i, c, blk0 + base))
    row_spec = pl.BlockSpec((1, heads, chunk), lambda i, c: (i, 0, c))
    state = lambda r, c: pltpu.VMEM((heads, r, c), F32)
    return pl.pallas_call(
        functools.partial(_mlstm_kernel, heads=heads, i_lane0=i_lane0, f_lane0=f_lane0),
        grid=(b, s // chunk),
        in_specs=[grp(0), grp(1), grp(2), grp(3),
                  pl.BlockSpec(conv_w.shape, lambda i, c: (0, 0)),
                  pl.BlockSpec(conv_b.shape, lambda i, c: (0, 0)),
                  pl.BlockSpec((1, chunk, GATE_LANES), lambda i, c: (i, c, 0)),
                  row_spec, row_spec,
                  pl.BlockSpec((1, w), lambda i, c: (0, ms_col0 * HEAD_DIM // w))],
        out_specs=pl.BlockSpec((1, chunk, w), lambda i, c: (i, c, 0)),
        out_shape=jax.ShapeDtypeStruct((b, s, w), BF16),
        scratch_shapes=[state(HEAD_DIM, HEAD_DIM), state(1, HEAD_DIM), state(1, HEAD_DIM), state(1, HEAD_DIM),
                        pltpu.VMEM((chunk + CONV_HALO, w), F32),
                        pltpu.VMEM((chunk + CONV_HALO, w), F32)],
        compiler_params=_params("arbitrary", "arbitrary"),
        name="mlstm",
    )(proj, proj, proj, proj, conv_w, conv_b, gp, i_rows, g_rows, merge_scale)


def _outproj_kernel(yr_ref, yf_ref, ym_ref, w_ref, x_ref, ga_ref, o_ref):
    r = yr_ref.shape[1]
    f = yf_ref.shape[1]
    y = _dot(yr_ref[...], w_ref[0:r, :]) + _dot(yf_ref[...], w_ref[r:r + f, :])
    y = y + _dot(ym_ref[...], w_ref[r + f:, :])
    o_ref[...] = x_ref[...] + ga_ref[0] * y


def _outproj(y_ret, y_fox, y_m, w_out, x2, g_a, seq):
    m, d = x2.shape
    tm = min(512, seq)
    bpr = seq // tm
    rows = lambda w: pl.BlockSpec((tm, w), lambda i: (i, 0))
    return pl.pallas_call(
        _outproj_kernel,
        grid=(m // tm,),
        in_specs=[rows(y_ret.shape[1]), rows(y_fox.shape[1]), rows(y_m.shape[1]),
                  pl.BlockSpec(w_out.shape, lambda i: (0, 0)),
                  rows(d),
                  pl.BlockSpec((1, 1, d), lambda i: (i // bpr, 0, 0))],
        out_specs=rows(d),
        out_shape=jax.ShapeDtypeStruct((m, d), F32),
        compiler_params=_params("arbitrary"),
        name="outproj",
    )(y_ret, y_fox, y_m, w_out, x2, g_a)


def _mlp_kernel(x_ref, nw_ref, sc_ref, sh_ref, gm_ref, w1_ref, w2_ref, fw_ref, o_ref, h_scr, acc_scr,
                *, final_norm):
    f = pl.program_id(1)

    @pl.when(f == 0)
    def _():
        _modulated_norm_rows(x_ref, nw_ref, sc_ref, sh_ref, h_scr)
        acc_scr[...] = jnp.zeros_like(acc_scr)

    a = jnp.maximum(_dot(h_scr[...], w1_ref[...]), 0.0)
    acc_scr[...] += _dot((a * a).astype(BF16), w2_ref[...])

    @pl.when(f == pl.num_programs(1) - 1)
    def _():
        y = x_ref[...] + gm_ref[0] * acc_scr[...]
        if final_norm:
            ms = jnp.mean(y * y, axis=-1, keepdims=True)
            y = y * lax.rsqrt(ms + NORM_EPS) * fw_ref[...]
        o_ref[...] = y


def _mlp(x2, nw, sc, sh, g_m, w1, w2, final_w, seq, final_norm):
    m, d = x2.shape
    dff = w1.shape[1]
    tm = min(512, seq)
    tf = min(1024, dff)
    bpr = seq // tm
    mod = pl.BlockSpec((1, 1, d), lambda i, f: (i // bpr, 0, 0))
    vec = pl.BlockSpec((1, d), lambda i, f: (0, 0))
    return pl.pallas_call(
        functools.partial(_mlp_kernel, final_norm=final_norm),
        grid=(m // tm, dff // tf),
        in_specs=[pl.BlockSpec((tm, d), lambda i, f: (i, 0)),
                  vec, mod, mod, mod,
                  pl.BlockSpec((d, tf), lambda i, f: (0, f)),
                  pl.BlockSpec((tf, d), lambda i, f: (f, 0)),
                  vec],
        out_specs=pl.BlockSpec((tm, d), lambda i, f: (i, 0)),
        out_shape=jax.ShapeDtypeStruct((m, d), F32),
        scratch_shapes=[pltpu.VMEM((tm, d), BF16), pltpu.VMEM((tm, d), F32)],
        compiler_params=_params("arbitrary", "arbitrary"),
        name="mlp",
    )(x2, nw, sc, sh, g_m, w1, w2, final_w)


def kernel(x, c, positions, ada_w, ada_b, norm_mix_w, norm_mlp_w, w_in, conv_w, conv_b, fox_f_bias,
           mlstm_i_bias, mlstm_f_bias, merge_scale, w_out, w_ff1, w_ff2, final_norm_w):
    b, s, d = x.shape
    depth = ada_w.shape[0]
    ret_w, fox_w, ml_w = d // 4, d // 2, d // 4
    ret_h, fox_h, ml_h = ret_w // HEAD_DIM, fox_w // HEAD_DIM, ml_w // HEAD_DIM
    assert s % 128 == 0 and d % (4 * HEAD_DIM) == 0
    assert fox_h + 2 * ml_h <= GATE_LANES

    sizes = [ret_w] * 4 + [fox_w] * 3 + [fox_h] + [ml_w] * 4 + [ml_h, ml_h]
    offs = [0]
    for sz in sizes:
        offs.append(offs[-1] + sz)
    n_gate = fox_h + 2 * ml_h
    ret_col0 = 0
    fox_col0 = (4 * ret_w) // HEAD_DIM
    ml_col0 = (4 * ret_w + 3 * fox_w) // HEAD_DIM
    i_lane0, f_lane0 = fox_h, fox_h + ml_h

    ret_chunk = min(RET_CHUNK, s)
    ml_chunk = min(MLSTM_CHUNK, s)
    tq = min(FOX_TQ, s)
    tk = min(FOX_TK, tq)

    mod = _ada_mod(c, ada_w, ada_b)
    cos, sin = _rope_tables(positions)
    cum_mask = jnp.zeros((1, GATE_LANES), F32).at[0, :fox_h].set(1.0).at[0, f_lane0:f_lane0 + ml_h].set(1.0)

    x2 = x.reshape(b * s, d)
    for layer in range(depth):
        sh_a, sc_a, g_a, sh_m, sc_m, g_m = [t.reshape(b, 1, d) for t in jnp.split(mod[layer], N_MOD, axis=-1)]
        wl = w_in[layer]
        w_main = jnp.concatenate([wl[:, offs[0]:offs[7]], wl[:, offs[8]:offs[12]]], axis=1).astype(BF16)
        w_gate = jnp.concatenate([wl[:, offs[7]:offs[8]], wl[:, offs[12]:offs[14]],
                                  jnp.zeros((d, GATE_LANES - n_gate), F32)], axis=1).astype(BF16)
        gate_bias = jnp.pad(jnp.concatenate([fox_f_bias[layer], mlstm_i_bias[layer], mlstm_f_bias[layer]]),
                            (0, GATE_LANES - n_gate)).reshape(1, GATE_LANES)
        ms = merge_scale[layer].reshape(1, d)

        proj, gates = _inproj(x2, norm_mix_w[layer].reshape(1, d), sc_a, sh_a, w_main, w_gate, s)
        proj = proj.reshape(b, s, -1)
        gp = _gates(gates.reshape(b, s, GATE_LANES), gate_bias, cum_mask)
        gp_t = jnp.swapaxes(gp[:, :, :n_gate], 1, 2)
        c_rows = gp_t[:, :fox_h].reshape(b * fox_h, 1, s)
        i_rows = gp_t[:, i_lane0:i_lane0 + ml_h]
        g_rows = gp_t[:, f_lane0:f_lane0 + ml_h]

        y_ret = _retention(proj, cos, sin, ms, ret_h, ret_col0, ret_chunk)
        k_aug = _fox_keys(proj, gp, fox_h, fox_col0 + fox_h, tk)
        y_fox = _fox(proj, k_aug, c_rows, ms, fox_h, fox_col0, ret_h, tq, tk, FOX_UNROLL if tq // tk % FOX_UNROLL == 0 else 1)
        y_m = _mlstm(proj, conv_w[layer], conv_b[layer].reshape(1, -1), gp, i_rows, g_rows, ms,
                     ml_h, ml_col0, ret_h + fox_h, i_lane0, f_lane0, ml_chunk)

        x2 = _outproj(y_ret.reshape(b * s, -1), y_fox.reshape(b * s, -1), y_m.reshape(b * s, -1),
                      w_out[layer].astype(BF16), x2, g_a, s)
        x2 = _mlp(x2, norm_mlp_w[layer].reshape(1, d), sc_m, sh_m, g_m,
                  w_ff1[layer].astype(BF16), w_ff2[layer].astype(BF16),
                  final_norm_w.reshape(1, d), s, layer == depth - 1)
    return x2.reshape(b, s, d)
```

```python
import functools
import math

import jax
import jax.numpy as jnp
from jax import lax
from jax.experimental import pallas as pl
from jax.experimental.pallas import tpu as pltpu

HEAD_DIM = 128
CONV_WIDTH = 4
ROPE_BASE = 10000.0
NORM_EPS = 1e-6
N_MOD = 6
GATE_LANES = 128
LOG2E = math.log2(math.e)
FOX_TQ, FOX_TK, FOX_UNROLL = 1024, 512, 2
RET_CHUNK, MLSTM_CHUNK = 256, 512
CONV_HALO = 8

F32 = jnp.float32
BF16 = jnp.bfloat16

VMEM_LIMIT_BYTES = 56 * 1024 * 1024


def _params(*sem, flags=None):
    return pltpu.CompilerParams(dimension_semantics=sem, vmem_limit_bytes=VMEM_LIMIT_BYTES, flags=flags)


def _dot(a, b):
    return jnp.dot(a, b, preferred_element_type=F32)


def _dot_nt(a, b):
    return lax.dot_general(a, b, (((1,), (1,)), ((), ())), preferred_element_type=F32)


def _dot_tn(a, b):
    return lax.dot_general(a, b, (((0,), (0,)), ((), ())), preferred_element_type=F32)


def _sigmoid(x):
    return 1.0 / (1.0 + jnp.exp(-x))


def _log_sigmoid(x):
    return jnp.minimum(x, 0.0) - jnp.log1p(jnp.exp(-jnp.abs(x)))


def _head_norm(y):
    mu = jnp.mean(y, axis=-1, keepdims=True)
    yc = y - mu
    var = jnp.mean(yc * yc, axis=-1, keepdims=True)
    return yc * lax.rsqrt(var + NORM_EPS)


def _ada_kernel(ct_ref, w_ref, b_ref, o_ref, *, kc):
    d, nb = ct_ref.shape
    tn = w_ref.shape[2]

    def body(i, accs):
        r0 = pl.multiple_of(i * kc, kc)
        ck = ct_ref[pl.ds(r0, kc), :]
        ck = ck * _sigmoid(ck)
        wk = w_ref[0, pl.ds(r0, kc), :]
        return tuple(acc + jnp.sum(wk * ck[:, b:b + 1], axis=0, keepdims=True)
                     for b, acc in enumerate(accs))

    accs = lax.fori_loop(0, d // kc, body, tuple(jnp.zeros((1, tn), F32) for _ in range(nb)))
    o_ref[0] = jnp.concatenate(accs, axis=0) + b_ref[0]


def _ada_mod(c, ada_w, ada_b):
    depth, d, n = ada_w.shape
    nb = c.shape[0]
    tn = min(1024, n)
    kc = min(256, d)
    return pl.pallas_call(
        functools.partial(_ada_kernel, kc=kc),
        grid=(depth, n // tn),
        in_specs=[pl.BlockSpec((d, nb), lambda l, j: (0, 0)),
                  pl.BlockSpec((1, d, tn), lambda l, j: (l, 0, j)),
                  pl.BlockSpec((1, 1, tn), lambda l, j: (l, 0, j))],
        out_specs=pl.BlockSpec((1, nb, tn), lambda l, j: (l, 0, j)),
        out_shape=jax.ShapeDtypeStruct((depth, nb, n), F32),
        compiler_params=_params("arbitrary", "arbitrary"),
        name="ada_mod",
    )(c.T, ada_w, ada_b.reshape(depth, 1, n))


def _rope_kernel(pos_ref, invf_ref, sign_ref, cos_ref, sin_ref):
    ang = pos_ref[0] * invf_ref[...]
    cos_ref[0] = jnp.cos(ang)
    sin_ref[0] = jnp.sin(ang) * sign_ref[...]


def _rope_tables(positions):
    b, s = positions.shape
    ts = min(2048, s)
    half = HEAD_DIM // 2
    inv_freq = ROPE_BASE ** (-jnp.arange(0, HEAD_DIM, 2, dtype=F32) / HEAD_DIM)
    invf = jnp.concatenate([inv_freq, inv_freq]).reshape(1, HEAD_DIM)
    sign = jnp.concatenate([-jnp.ones((half,), F32), jnp.ones((half,), F32)]).reshape(1, HEAD_DIM)
    pos = positions.astype(F32).reshape(b, s, 1)
    out = jax.ShapeDtypeStruct((b, s, HEAD_DIM), F32)
    return pl.pallas_call(
        _rope_kernel,
        grid=(b, s // ts),
        in_specs=[pl.BlockSpec((1, ts, 1), lambda i, j: (i, j, 0)),
                  pl.BlockSpec((1, HEAD_DIM), lambda i, j: (0, 0)),
                  pl.BlockSpec((1, HEAD_DIM), lambda i, j: (0, 0))],
        out_specs=[pl.BlockSpec((1, ts, HEAD_DIM), lambda i, j: (i, j, 0))] * 2,
        out_shape=[out, out],
        compiler_params=_params("arbitrary", "arbitrary"),
        name="rope_tables",
    )(pos, invf, sign)


def _modulated_norm_rows(x_ref, nw_ref, sc_ref, sh_ref, h_scr):
    x = x_ref[...]
    inv = lax.rsqrt(jnp.mean(x * x, axis=-1, keepdims=True) + NORM_EPS)
    h_scr[...] = ((x_ref[...] * inv * nw_ref[...]) * (1.0 + sc_ref[0]) + sh_ref[0]).astype(BF16)


def _inproj_kernel(x_ref, nw_ref, sc_ref, sh_ref, w_ref, wg_ref, o_ref, g_ref, h_scr):
    @pl.when(pl.program_id(1) == 0)
    def _():
        _modulated_norm_rows(x_ref, nw_ref, sc_ref, sh_ref, h_scr)
        g_ref[...] = _dot(h_scr[...], wg_ref[...])

    o_ref[...] = _dot(h_scr[...], w_ref[...]).astype(BF16)


def _inproj(x2, nw, sc, sh, w_main, w_gate, seq):
    m, d = x2.shape
    nm = w_main.shape[1]
    tm = min(1024, seq)
    tn = 1024 if nm % 1024 == 0 else 512
    bpr = seq // tm
    return pl.pallas_call(
        _inproj_kernel,
        grid=(m // tm, nm // tn),
        in_specs=[pl.BlockSpec((tm, d), lambda i, j: (i, 0)),
                  pl.BlockSpec((1, d), lambda i, j: (0, 0)),
                  pl.BlockSpec((1, 1, d), lambda i, j: (i // bpr, 0, 0)),
                  pl.BlockSpec((1, 1, d), lambda i, j: (i // bpr, 0, 0)),
                  pl.BlockSpec((d, tn), lambda i, j: (0, j)),
                  pl.BlockSpec((d, GATE_LANES), lambda i, j: (0, 0))],
        out_specs=[pl.BlockSpec((tm, tn), lambda i, j: (i, j)),
                   pl.BlockSpec((tm, GATE_LANES), lambda i, j: (i, 0))],
        out_shape=[jax.ShapeDtypeStruct((m, nm), BF16),
                   jax.ShapeDtypeStruct((m, GATE_LANES), F32)],
        scratch_shapes=[pltpu.VMEM((tm, d), BF16)],
        compiler_params=_params("arbitrary", "arbitrary"),
        name="inproj",
    )(x2, nw, sc, sh, w_main, w_gate)


def _gates_kernel(g_ref, bias_ref, cum_ref, o_ref, carry):
    @pl.when(pl.program_id(1) == 0)
    def _():
        carry[...] = jnp.zeros_like(carry)

    g = g_ref[0] + bias_ref[...]
    ls = _log_sigmoid(g)
    ts = ls.shape[0]
    row = lax.broadcasted_iota(jnp.int32, (ts, ts), 0)
    col = lax.broadcasted_iota(jnp.int32, (ts, ts), 1)
    tri = jnp.where(row >= col, 1.0, 0.0).astype(BF16)
    hi = ls.astype(BF16)
    r1 = ls - hi.astype(F32)
    mid = r1.astype(BF16)
    lo = (r1 - mid.astype(F32)).astype(BF16)
    csum = (_dot(tri, hi) + _dot(tri, mid)) + _dot(tri, lo) + carry[...]
    carry[...] = csum[ts - 1:ts, :]
    o_ref[0] = jnp.where(cum_ref[...] > 0.5, csum, g)


def _gates(g3, bias, cum_mask):
    b, s, n = g3.shape
    ts = min(512, s)
    return pl.pallas_call(
        _gates_kernel,
        grid=(b, s // ts),
        in_specs=[pl.BlockSpec((1, ts, n), lambda i, j: (i, j, 0)),
                  pl.BlockSpec((1, n), lambda i, j: (0, 0)),
                  pl.BlockSpec((1, n), lambda i, j: (0, 0))],
        out_specs=pl.BlockSpec((1, ts, n), lambda i, j: (i, j, 0)),
        out_shape=jax.ShapeDtypeStruct((b, s, n), F32),
        scratch_shapes=[pltpu.VMEM((1, n), F32)],
        compiler_params=_params("arbitrary", "arbitrary"),
        name="gates",
    )(g3, bias, cum_mask)


def _retention_kernel(q_ref, k_ref, v_ref, g_ref, cos_ref, sin_ref, ms_ref, o_ref, state, decay, *, heads):
    n = cos_ref.shape[1]
    log_gamma = [math.log(1.0 - 2.0 ** (-5.0 - h)) for h in range(heads)]

    @pl.when(pl.program_id(1) == 0)
    def _():
        state[...] = jnp.zeros_like(state)
        row = lax.broadcasted_iota(jnp.int32, (n, n), 0)
        col = lax.broadcasted_iota(jnp.int32, (n, n), 1)
        rel = (row - col).astype(F32)
        for h in range(heads):
            decay[h] = jnp.where(rel >= 0, jnp.exp(jnp.maximum(rel, 0.0) * log_gamma[h]), 0.0)

    cos = cos_ref[0]
    sin = sin_ref[0]
    half = HEAD_DIM // 2
    idx = lax.broadcasted_iota(jnp.int32, (n, 1), 0).astype(F32)

    def rot(t):
        return t * cos + pltpu.roll(t, half, 1) * sin

    for h in range(heads):
        lg = log_gamma[h]
        cs = slice(h * HEAD_DIM, (h + 1) * HEAD_DIM)
        q = rot(q_ref[0, :, cs].astype(F32))
        k = rot(k_ref[0, :, cs].astype(F32)) * (HEAD_DIM ** -0.5)
        v = v_ref[0, :, cs]
        q_decay = jnp.exp((idx + 1.0) * lg)
        k_decay = jnp.exp((n - 1.0 - idx) * lg)
        chunk_decay = math.exp(n * lg)

        scores = _dot_nt(q.astype(BF16), k.astype(BF16)) * decay[h]
        intra = _dot(scores.astype(BF16), v)
        st = state[h]
        inter = _dot((q * q_decay).astype(BF16), st.astype(BF16))
        state[h] = st * chunk_decay + _dot_tn((k * k_decay).astype(BF16), v)

        y = _head_norm(intra + inter)
        gate = g_ref[0, :, cs].astype(F32)
        o_ref[0, :, cs] = (y * (gate * _sigmoid(gate)) * ms_ref[:, cs]).astype(BF16)


def _retention(proj, cos, sin, merge_scale, heads, col0, chunk):
    b, s, _ = proj.shape
    w = heads * HEAD_DIM
    assert (col0 * HEAD_DIM) % w == 0
    blk0 = col0 * HEAD_DIM // w
    grp = lambda base: pl.BlockSpec((1, chunk, w), lambda i, c: (i, c, blk0 + base))
    tok = pl.BlockSpec((1, chunk, HEAD_DIM), lambda i, c: (i, c, 0))
    return pl.pallas_call(
        functools.partial(_retention_kernel, heads=heads),
        grid=(b, s // chunk),
        in_specs=[grp(0), grp(1), grp(2), grp(3), tok, tok,
                  pl.BlockSpec((1, w), lambda i, c: (0, 0))],
        out_specs=pl.BlockSpec((1, chunk, w), lambda i, c: (i, c, 0)),
        out_shape=jax.ShapeDtypeStruct((b, s, w), BF16),
        scratch_shapes=[pltpu.VMEM((heads, HEAD_DIM, HEAD_DIM), F32),
                        pltpu.VMEM((heads, chunk, chunk), F32)],
        compiler_params=_params("arbitrary", "arbitrary"),
        name="retention",
    )(proj, proj, proj, proj, cos, sin, merge_scale)


def _fox_keys_kernel(k_ref, gp_ref, o_ref, *, tk, heads):
    ts = k_ref.shape[1]
    c = gp_ref[0]
    parts = []
    for i in range(ts // tk):
        blk = c[i * tk:(i + 1) * tk]
        parts.append((blk[0:1, :] - blk) * LOG2E)
    x = jnp.concatenate(parts, axis=0) if len(parts) > 1 else parts[0]
    hi = x.astype(BF16)
    r1 = x - hi.astype(F32)
    mid = r1.astype(BF16)
    lo = (r1 - mid.astype(F32)).astype(BF16)
    row = lax.broadcasted_iota(jnp.int32, (GATE_LANES, HEAD_DIM), 0)
    col = lax.broadcasted_iota(jnp.int32, (GATE_LANES, HEAD_DIM), 1)
    for h in range(heads):
        sel = lambda piece: jnp.where((row == h) & (col == piece), 1.0, 0.0).astype(BF16)
        bias = _dot(hi, sel(0)) + _dot(mid, sel(1)) + _dot(lo, sel(2))
        o_ref[0, h] = jnp.concatenate([k_ref[0, :, h * HEAD_DIM:(h + 1) * HEAD_DIM], bias.astype(BF16)], axis=1)


def _fox_keys(proj, gp, heads, k_col0, tk):
    b, s, _ = proj.shape
    ts = min(1024, s)
    w = heads * HEAD_DIM
    assert (k_col0 * HEAD_DIM) % w == 0 and ts % tk == 0
    return pl.pallas_call(
        functools.partial(_fox_keys_kernel, tk=tk, heads=heads),
        grid=(b, s // ts),
        in_specs=[pl.BlockSpec((1, ts, w), lambda i, j: (i, j, k_col0 * HEAD_DIM // w)),
                  pl.BlockSpec((1, ts, GATE_LANES), lambda i, j: (i, j, 0))],
        out_specs=pl.BlockSpec((1, heads, ts, 2 * HEAD_DIM), lambda i, j: (i, 0, j, 0)),
        out_shape=jax.ShapeDtypeStruct((b, heads, s, 2 * HEAD_DIM), BF16),
        compiler_params=_params("arbitrary", "arbitrary"),
        name="fox_keys",
    )(proj, gp)


def _fox_kernel(q_ref, k_ref, v_ref, c_ref, ms_ref, o_ref, s_scr, m_scr, l_scr, acc_scr,
                *, tq, tk, unroll):
    ndiag = tq // tk
    static_slots = unroll % 2 == 0 and ndiag % 2 == 0
    qi = pl.program_id(2)
    nfull = qi * ndiag
    q0 = pl.multiple_of(qi * tq, tq)
    q_t = (q_ref[0].astype(F32) * (HEAD_DIM ** -0.5 * LOG2E)).T.astype(BF16)
    row = lax.broadcasted_iota(jnp.int32, (HEAD_DIM, tq), 0)
    q_aug = jnp.concatenate([q_t, jnp.where(row < 3, 1.0, 0.0).astype(BF16)], axis=0)
    c_first = c_ref[0, :, pl.ds(q0, 128)][:, 0:1]

    m_scr[...] = jnp.full_like(m_scr, -jnp.inf)
    l_scr[...] = jnp.zeros_like(l_scr)
    acc_scr[...] = jnp.zeros_like(acc_scr)

    def scores(j, slot, c0):
        k0 = pl.multiple_of(j * tk, tk)
        s_scr[slot, :, c0:] = _dot(k_ref[0, 0, pl.ds(k0, tk), :], q_aug[:, c0:])

    def softmax(j, slot, diag):
        k0 = pl.multiple_of(j * tk, tk)
        c0 = 0 if diag is None else diag * tk
        delta = (c_first - c_ref[0, :, pl.ds(k0, 128)][:, 0:1]) * LOG2E
        t = s_scr[slot, :, c0:]
        if diag is not None:
            key = lax.broadcasted_iota(jnp.int32, t.shape, 0)
            qry = lax.broadcasted_iota(jnp.int32, t.shape, 1)
            t = jnp.where(qry >= key, t, -jnp.inf)
        m_prev = m_scr[:, c0:]
        m_new = jnp.maximum(m_prev, jnp.max(t, axis=0, keepdims=True) + delta)
        p = jnp.exp2(t - (m_new - delta)).astype(BF16)
        alpha = jnp.exp2(m_prev - m_new)
        half = p[:tk // 2] + p[tk // 2:]
        quarter = half[:tk // 4] + half[tk // 4:]
        l_scr[:, c0:] = alpha * l_scr[:, c0:] + jnp.sum(quarter.astype(F32), axis=0, keepdims=True)
        m_scr[:, c0:] = m_new
        acc_scr[:, c0:] = alpha * acc_scr[:, c0:] + _dot_tn(v_ref[0, pl.ds(k0, tk), :], p)

    def step(j, slot, diag, last):
        if not last:
            scores(j + 1, 1 - slot, 0 if diag is None else (diag + 1) * tk)
        softmax(j, slot, diag)

    scores(0, 0, 0)

    def body(i, carry):
        for u in range(unroll):
            j = i * unroll + u
            step(j, u % 2 if static_slots else j & 1, None, False)
        return carry

    lax.fori_loop(0, nfull // unroll, body, 0)
    for d in range(ndiag):
        j = nfull + d
        slot = d % 2 if ndiag % 2 == 0 else j & 1
        step(j, slot, d, d == ndiag - 1)

    out = (acc_scr[...] / l_scr[...]).T
    o_ref[0] = (_head_norm(out) * ms_ref[...]).astype(BF16)


def _fox(proj, k_aug, c_rows, merge_scale, heads, col0, ms_col0, tq, tk, unroll):
    b, s, _ = proj.shape
    assert tq % tk == 0 and (tq // tk) % unroll == 0
    return pl.pallas_call(
        functools.partial(_fox_kernel, tq=tq, tk=tk, unroll=unroll),
        grid=(b, heads, s // tq),
        in_specs=[pl.BlockSpec((1, tq, HEAD_DIM), lambda i, h, q: (i, q, col0 + h)),
                  pl.BlockSpec((1, 1, s, 2 * HEAD_DIM), lambda i, h, q: (i, h, 0, 0)),
                  pl.BlockSpec((1, s, HEAD_DIM), lambda i, h, q: (i, 0, col0 + 2 * heads + h)),
                  pl.BlockSpec((1, 1, s), lambda i, h, q: (i * heads + h, 0, 0)),
                  pl.BlockSpec((1, HEAD_DIM), lambda i, h, q: (0, ms_col0 + h))],
        out_specs=pl.BlockSpec((1, tq, HEAD_DIM), lambda i, h, q: (i, q, h)),
        out_shape=jax.ShapeDtypeStruct((b, s, heads * HEAD_DIM), BF16),
        scratch_shapes=[pltpu.VMEM((2, tk, tq), F32),
                        pltpu.VMEM((1, tq), F32),
                        pltpu.VMEM((1, tq), F32),
                        pltpu.VMEM((HEAD_DIM, tq), F32)],
        compiler_params=_params("arbitrary", "arbitrary", "arbitrary"),
        name="fox_attention",
    )(proj, k_aug, proj, c_rows, merge_scale)


def _mlstm_kernel(q_ref, k_ref, v_ref, og_ref, w_ref, b_ref, gp_ref, ir_ref, gr_ref, ms_ref, o_ref,
                  c_scr, n_scr, m_scr, g0_scr, xq_scr, xk_scr, *, heads, i_lane0, f_lane0):
    n = q_ref.shape[1]
    w = heads * HEAD_DIM

    @pl.when(pl.program_id(1) == 0)
    def _():
        c_scr[...] = jnp.zeros_like(c_scr)
        n_scr[...] = jnp.zeros_like(n_scr)
        m_scr[...] = jnp.zeros_like(m_scr)
        g0_scr[...] = jnp.zeros_like(g0_scr)
        xq_scr[0:CONV_HALO, :] = jnp.zeros((CONV_HALO, w), F32)
        xk_scr[0:CONV_HALO, :] = jnp.zeros((CONV_HALO, w), F32)

    def conv_silu(x_ref, x_scr, cols):
        x_scr[CONV_HALO:CONV_HALO + n, :] = x_ref[0].astype(F32)
        y = b_ref[:, cols] + jnp.zeros((n, w), F32)
        for j in range(CONV_WIDTH):
            shift = CONV_WIDTH - 1 - j
            y = y + w_ref[j:j + 1, cols] * x_scr[CONV_HALO - shift:CONV_HALO - shift + n, :]
        x_scr[0:CONV_HALO, :] = x_scr[n:n + CONV_HALO, :]
        return y * _sigmoid(y)

    q_all = conv_silu(q_ref, xq_scr, slice(0, w))
    k_all = conv_silu(k_ref, xk_scr, slice(w, 2 * w)) * (HEAD_DIM ** -0.5)
    gp = gp_ref[0]
    row = lax.broadcasted_iota(jnp.int32, (n, n), 0)
    col = lax.broadcasted_iota(jnp.int32, (n, n), 1)
    causal = row >= col

    for h in range(heads):
        cs = slice(h * HEAD_DIM, (h + 1) * HEAD_DIM)
        q = q_all[:, cs]
        k = k_all[:, cs]
        v = v_ref[0, :, cs]
        ic = gp[:, i_lane0 + h:i_lane0 + h + 1]
        gc = gp[:, f_lane0 + h:f_lane0 + h + 1]
        ir = ir_ref[0, h:h + 1, :]
        gr = gr_ref[0, h:h + 1, :]

        g0 = g0_scr[h][:, 0:1]
        m_prev = m_scr[h][:, 0:1]
        b_col = gc - g0
        dmat = jnp.where(causal, gc + (ir - gr), -jnp.inf)
        inter_log = b_col + m_prev
        m_q = jnp.maximum(inter_log, jnp.max(dmat, axis=-1, keepdims=True))
        w_intra = jnp.exp(dmat - m_q)
        w_inter = jnp.exp(inter_log - m_q)

        qb = q.astype(BF16)
        s = _dot_nt(qb, k.astype(BF16)) * w_intra
        cst = c_scr[h]
        nst = n_scr[h]
        num = _dot(s.astype(BF16), v) + w_inter * _dot(qb, cst.astype(BF16))
        den = jnp.sum(s, axis=-1, keepdims=True) + w_inter * jnp.sum(q * nst, axis=-1, keepdims=True)
        hid = num / jnp.maximum(jnp.abs(den), jnp.exp(-m_q))

        b_last = b_col[n - 1:n, :]
        k_log = b_last - b_col + ic
        m_new = jnp.maximum(b_last + m_prev, jnp.max(k_log, axis=0, keepdims=True))
        wk = jnp.exp(k_log - m_new)
        carry_scale = jnp.exp(b_last + m_prev - m_new)
        kw = k * wk
        c_scr[h] = carry_scale * cst + _dot_tn(kw.astype(BF16), v)
        n_scr[h] = carry_scale * nst + jnp.sum(kw, axis=0, keepdims=True)
        m_scr[h] = jnp.broadcast_to(m_new, (1, HEAD_DIM))
        g0_scr[h] = jnp.broadcast_to(gc[n - 1:n, :], (1, HEAD_DIM))

        og = og_ref[0, :, cs].astype(F32)
        o_ref[0, :, cs] = (_head_norm(hid) * _sigmoid(og) * ms_ref[:, cs]).astype(BF16)


def _mlstm(proj, conv_w, conv_b, gp, i_rows, g_rows, merge_scale, heads, col0, ms_col0, i_lane0, f_lane0, chunk):
    b, s, _ = proj.shape
    w = heads * HEAD_DIM
    assert (col0 * HEAD_DIM) % w == 0 and (ms_col0 * HEAD_DIM) % w == 0
    blk0 = col0 * HEAD_DIM // w
    grp = lambda base: pl.BlockSpec((1, chunk, w), lambda i, c: (i, c, blk0 + base))
    row_spec = pl.BlockSpec((1, heads, chunk), lambda i, c: (i, 0, c))
    state = lambda r, c: pltpu.VMEM((heads, r, c), F32)
    return pl.pallas_call(
        functools.partial(_mlstm_kernel, heads=heads, i_lane0=i_lane0, f_lane0=f_lane0),
        grid=(b, s // chunk),
        in_specs=[grp(0), grp(1), grp(2), grp(3),
                  pl.BlockSpec(conv_w.shape, lambda i, c: (0, 0)),
                  pl.BlockSpec(conv_b.shape, lambda i, c: (0, 0)),
                  pl.BlockSpec((1, chunk, GATE_LANES), lambda i, c: (i, c, 0)),
                  row_spec, row_spec,
                  pl.BlockSpec((1, w), lambda i, c: (0, ms_col0 * HEAD_DIM // w))],
        out_specs=pl.BlockSpec((1, chunk, w), lambda i, c: (i, c, 0)),
        out_shape=jax.ShapeDtypeStruct((b, s, w), BF16),
        scratch_shapes=[state(HEAD_DIM, HEAD_DIM), state(1, HEAD_DIM), state(1, HEAD_DIM), state(1, HEAD_DIM),
                        pltpu.VMEM((chunk + CONV_HALO, w), F32),
                        pltpu.VMEM((chunk + CONV_HALO, w), F32)],
        compiler_params=_params("arbitrary", "arbitrary"),
        name="mlstm",
    )(proj, proj, proj, proj, conv_w, conv_b, gp, i_rows, g_rows, merge_scale)


def _outproj_kernel(yr_ref, yf_ref, ym_ref, w_ref, x_ref, ga_ref, o_ref):
    r = yr_ref.shape[1]
    f = yf_ref.shape[1]
    y = _dot(yr_ref[...], w_ref[0:r, :]) + _dot(yf_ref[...], w_ref[r:r + f, :])
    y = y + _dot(ym_ref[...], w_ref[r + f:, :])
    o_ref[...] = x_ref[...] + ga_ref[0] * y


def _outproj(y_ret, y_fox, y_m, w_out, x2, g_a, seq):
    m, d = x2.shape
    tm = min(512, seq)
    bpr = seq // tm
    rows = lambda w: pl.BlockSpec((tm, w), lambda i: (i, 0))
    return pl.pallas_call(
        _outproj_kernel,
        grid=(m // tm,),
        in_specs=[rows(y_ret.shape[1]), rows(y_fox.shape[1]), rows(y_m.shape[1]),
                  pl.BlockSpec(w_out.shape, lambda i: (0, 0)),
                  rows(d),
                  pl.BlockSpec((1, 1, d), lambda i: (i // bpr, 0, 0))],
        out_specs=rows(d),
        out_shape=jax.ShapeDtypeStruct((m, d), F32),
        compiler_params=_params("arbitrary"),
        name="outproj",
    )(y_ret, y_fox, y_m, w_out, x2, g_a)


def _mlp_kernel(x_ref, xn_ref, nw_ref, sc_ref, sh_ref, scn_ref, shn_ref, gm_ref, w1_ref, w2_ref, fw_ref, o_ref,
                h0_scr, h1_scr, acc_scr, *, final_norm, rows):
    i = pl.program_id(0)
    f = pl.program_id(1)

    @pl.when((i == 0) & (f == 0))
    def _():
        _modulated_norm_rows(x_ref, nw_ref, sc_ref, sh_ref, h0_scr)

    @pl.when(f == 0)
    def _():
        acc_scr[...] = jnp.zeros_like(acc_scr)

    def step(h_cur, h_next):
        r0 = pl.multiple_of(f * rows, rows)
        xn = xn_ref[pl.ds(r0, rows), :]
        inv = lax.rsqrt(jnp.mean(xn * xn, axis=-1, keepdims=True) + NORM_EPS)
        h_next[pl.ds(r0, rows), :] = ((xn * inv * nw_ref[...]) * (1.0 + scn_ref[0]) + shn_ref[0]).astype(BF16)
        a = jnp.maximum(_dot(h_cur[...], w1_ref[...]), 0.0)
        acc_scr[...] += _dot((a * a).astype(BF16), w2_ref[...])

    @pl.when(i % 2 == 0)
    def _():
        step(h0_scr, h1_scr)

    @pl.when(i % 2 == 1)
    def _():
        step(h1_scr, h0_scr)

    @pl.when(f == pl.num_programs(1) - 1)
    def _():
        y = x_ref[...] + gm_ref[0] * acc_scr[...]
        if final_norm:
            ms = jnp.mean(y * y, axis=-1, keepdims=True)
            y = y * lax.rsqrt(ms + NORM_EPS) * fw_ref[...]
        o_ref[...] = y


def _mlp(x2, nw, sc, sh, g_m, w1, w2, final_w, seq, final_norm):
    m, d = x2.shape
    dff = w1.shape[1]
    tm = min(512, seq)
    tf = min(1024, dff)
    nf = dff // tf
    assert tm % (16 * nf) == 0
    bpr = seq // tm
    last = m // tm - 1
    nxt = lambda i: jnp.minimum(i + 1, last)
    mod = pl.BlockSpec((1, 1, d), lambda i, f: (i // bpr, 0, 0))
    mod_next = pl.BlockSpec((1, 1, d), lambda i, f: (nxt(i) // bpr, 0, 0))
    vec = pl.BlockSpec((1, d), lambda i, f: (0, 0))
    return pl.pallas_call(
        functools.partial(_mlp_kernel, final_norm=final_norm, rows=tm // nf),
        grid=(m // tm, nf),
        in_specs=[pl.BlockSpec((tm, d), lambda i, f: (i, 0)),
                  pl.BlockSpec((tm, d), lambda i, f: (nxt(i), 0)),
                  vec, mod, mod, mod_next, mod_next, mod,
                  pl.BlockSpec((d, tf), lambda i, f: (0, f)),
                  pl.BlockSpec((tf, d), lambda i, f: (f, 0)),
                  vec],
        out_specs=pl.BlockSpec((tm, d), lambda i, f: (i, 0)),
        out_shape=jax.ShapeDtypeStruct((m, d), F32),
        scratch_shapes=[pltpu.VMEM((tm, d), BF16), pltpu.VMEM((tm, d), BF16), pltpu.VMEM((tm, d), F32)],
        compiler_params=_params("arbitrary", "arbitrary"),
        name="mlp",
    )(x2, x2, nw, sc, sh, sc, sh, g_m, w1, w2, final_w)


def kernel(x, c, positions, ada_w, ada_b, norm_mix_w, norm_mlp_w, w_in, conv_w, conv_b, fox_f_bias,
           mlstm_i_bias, mlstm_f_bias, merge_scale, w_out, w_ff1, w_ff2, final_norm_w):
    b, s, d = x.shape
    depth = ada_w.shape[0]
    ret_w, fox_w, ml_w = d // 4, d // 2, d // 4
    ret_h, fox_h, ml_h = ret_w // HEAD_DIM, fox_w // HEAD_DIM, ml_w // HEAD_DIM
    assert s % 128 == 0 and d % (4 * HEAD_DIM) == 0
    assert fox_h + 2 * ml_h <= GATE_LANES

    sizes = [ret_w] * 4 + [fox_w] * 3 + [fox_h] + [ml_w] * 4 + [ml_h, ml_h]
    offs = [0]
    for sz in sizes:
        offs.append(offs[-1] + sz)
    n_gate = fox_h + 2 * ml_h
    ret_col0 = 0
    fox_col0 = (4 * ret_w) // HEAD_DIM
    ml_col0 = (4 * ret_w + 3 * fox_w) // HEAD_DIM
    i_lane0, f_lane0 = fox_h, fox_h + ml_h

    ret_chunk = min(RET_CHUNK, s)
    ml_chunk = min(MLSTM_CHUNK, s)
    tq = min(FOX_TQ, s)
    tk = min(FOX_TK, tq)

    mod = _ada_mod(c, ada_w, ada_b)
    cos, sin = _rope_tables(positions)
    cum_mask = jnp.zeros((1, GATE_LANES), F32).at[0, :fox_h].set(1.0).at[0, f_lane0:f_lane0 + ml_h].set(1.0)

    x2 = x.reshape(b * s, d)
    for layer in range(depth):
        sh_a, sc_a, g_a, sh_m, sc_m, g_m = [t.reshape(b, 1, d) for t in jnp.split(mod[layer], N_MOD, axis=-1)]
        wl = w_in[layer]
        w_main = jnp.concatenate([wl[:, offs[0]:offs[7]], wl[:, offs[8]:offs[12]]], axis=1).astype(BF16)
        w_gate = jnp.concatenate([wl[:, offs[7]:offs[8]], wl[:, offs[12]:offs[14]],
                                  jnp.zeros((d, GATE_LANES - n_gate), F32)], axis=1).astype(BF16)
        gate_bias = jnp.pad(jnp.concatenate([fox_f_bias[layer], mlstm_i_bias[layer], mlstm_f_bias[layer]]),
                            (0, GATE_LANES - n_gate)).reshape(1, GATE_LANES)
        ms = merge_scale[layer].reshape(1, d)

        proj, gates = _inproj(x2, norm_mix_w[layer].reshape(1, d), sc_a, sh_a, w_main, w_gate, s)
        proj = proj.reshape(b, s, -1)
        gp = _gates(gates.reshape(b, s, GATE_LANES), gate_bias, cum_mask)
        gp_t = jnp.swapaxes(gp[:, :, :n_gate], 1, 2)
        c_rows = gp_t[:, :fox_h].reshape(b * fox_h, 1, s)
        i_rows = gp_t[:, i_lane0:i_lane0 + ml_h]
        g_rows = gp_t[:, f_lane0:f_lane0 + ml_h]

        y_ret = _retention(proj, cos, sin, ms, ret_h, ret_col0, ret_chunk)
        k_aug = _fox_keys(proj, gp, fox_h, fox_col0 + fox_h, tk)
        y_fox = _fox(proj, k_aug, c_rows, ms, fox_h, fox_col0, ret_h, tq, tk, FOX_UNROLL if tq // tk % FOX_UNROLL == 0 else 1)
        y_m = _mlstm(proj, conv_w[layer], conv_b[layer].reshape(1, -1), gp, i_rows, g_rows, ms,
                     ml_h, ml_col0, ret_h + fox_h, i_lane0, f_lane0, ml_chunk)

        x2 = _outproj(y_ret.reshape(b * s, -1), y_fox.reshape(b * s, -1), y_m.reshape(b * s, -1),
                      w_out[layer].astype(BF16), x2, g_a, s)
        x2 = _mlp(x2, norm_mlp_w[layer].reshape(1, d), sc_m, sh_m, g_m,
                  w_ff1[layer].astype(BF16), w_ff2[layer].astype(BF16),
                  final_norm_w.reshape(1, d), s, layer == depth - 1)
    return x2.reshape(b, s, d)
```

```python
import functools
import math

import jax
import jax.numpy as jnp
from jax import lax
from jax.experimental import pallas as pl
from jax.experimental.pallas import tpu as pltpu

HEAD_DIM = 128
CONV_WIDTH = 4
ROPE_BASE = 10000.0
NORM_EPS = 1e-6
N_MOD = 6
GATE_LANES = 128
LOG2E = math.log2(math.e)
FOX_TQ, FOX_TK, FOX_UNROLL = 1024, 512, 2
RET_CHUNK, MLSTM_CHUNK = 256, 512
CONV_HALO = 8

F32 = jnp.float32
BF16 = jnp.bfloat16

VMEM_LIMIT_BYTES = 56 * 1024 * 1024


def _params(*sem, flags=None):
    return pltpu.CompilerParams(dimension_semantics=sem, vmem_limit_bytes=VMEM_LIMIT_BYTES, flags=flags)


def _dot(a, b):
    return jnp.dot(a, b, preferred_element_type=F32)


def _dot_nt(a, b):
    return lax.dot_general(a, b, (((1,), (1,)), ((), ())), preferred_element_type=F32)


def _dot_tn(a, b):
    return lax.dot_general(a, b, (((0,), (0,)), ((), ())), preferred_element_type=F32)


def _sigmoid(x):
    return 1.0 / (1.0 + jnp.exp(-x))


def _log_sigmoid(x):
    return jnp.minimum(x, 0.0) - jnp.log1p(jnp.exp(-jnp.abs(x)))


def _head_norm(y):
    mu = jnp.mean(y, axis=-1, keepdims=True)
    yc = y - mu
    var = jnp.mean(yc * yc, axis=-1, keepdims=True)
    return yc * lax.rsqrt(var + NORM_EPS)


def _ada_kernel(ct_ref, w_ref, b_ref, o_ref, *, kc):
    d, nb = ct_ref.shape
    tn = w_ref.shape[2]

    def body(i, accs):
        r0 = pl.multiple_of(i * kc, kc)
        ck = ct_ref[pl.ds(r0, kc), :]
        ck = ck * _sigmoid(ck)
        wk = w_ref[0, pl.ds(r0, kc), :]
        return tuple(acc + jnp.sum(wk * ck[:, b:b + 1], axis=0, keepdims=True)
                     for b, acc in enumerate(accs))

    accs = lax.fori_loop(0, d // kc, body, tuple(jnp.zeros((1, tn), F32) for _ in range(nb)))
    o_ref[0] = jnp.concatenate(accs, axis=0) + b_ref[0]


def _ada_mod(c, ada_w, ada_b):
    depth, d, n = ada_w.shape
    nb = c.shape[0]
    tn = min(1024, n)
    kc = min(256, d)
    return pl.pallas_call(
        functools.partial(_ada_kernel, kc=kc),
        grid=(depth, n // tn),
        in_specs=[pl.BlockSpec((d, nb), lambda l, j: (0, 0)),
                  pl.BlockSpec((1, d, tn), lambda l, j: (l, 0, j)),
                  pl.BlockSpec((1, 1, tn), lambda l, j: (l, 0, j))],
        out_specs=pl.BlockSpec((1, nb, tn), lambda l, j: (l, 0, j)),
        out_shape=jax.ShapeDtypeStruct((depth, nb, n), F32),
        compiler_params=_params("arbitrary", "arbitrary"),
        name="ada_mod",
    )(c.T, ada_w, ada_b.reshape(depth, 1, n))


def _rope_kernel(pos_ref, invf_ref, sign_ref, cos_ref, sin_ref):
    ang = pos_ref[0] * invf_ref[...]
    cos_ref[0] = jnp.cos(ang)
    sin_ref[0] = jnp.sin(ang) * sign_ref[...]


def _rope_tables(positions):
    b, s = positions.shape
    ts = min(2048, s)
    half = HEAD_DIM // 2
    inv_freq = ROPE_BASE ** (-jnp.arange(0, HEAD_DIM, 2, dtype=F32) / HEAD_DIM)
    invf = jnp.concatenate([inv_freq, inv_freq]).reshape(1, HEAD_DIM)
    sign = jnp.concatenate([-jnp.ones((half,), F32), jnp.ones((half,), F32)]).reshape(1, HEAD_DIM)
    pos = positions.astype(F32).reshape(b, s, 1)
    out = jax.ShapeDtypeStruct((b, s, HEAD_DIM), F32)
    return pl.pallas_call(
        _rope_kernel,
        grid=(b, s // ts),
        in_specs=[pl.BlockSpec((1, ts, 1), lambda i, j: (i, j, 0)),
                  pl.BlockSpec((1, HEAD_DIM), lambda i, j: (0, 0)),
                  pl.BlockSpec((1, HEAD_DIM), lambda i, j: (0, 0))],
        out_specs=[pl.BlockSpec((1, ts, HEAD_DIM), lambda i, j: (i, j, 0))] * 2,
        out_shape=[out, out],
        compiler_params=_params("arbitrary", "arbitrary"),
        name="rope_tables",
    )(pos, invf, sign)


def _modulated_norm_rows(x_ref, nw_ref, sc_ref, sh_ref, h_scr):
    x = x_ref[...]
    inv = lax.rsqrt(jnp.mean(x * x, axis=-1, keepdims=True) + NORM_EPS)
    h_scr[...] = ((x_ref[...] * inv * nw_ref[...]) * (1.0 + sc_ref[0]) + sh_ref[0]).astype(BF16)


def _inproj_kernel(x_ref, nw_ref, sc_ref, sh_ref, w_ref, wg_ref, o_ref, g_ref, h_scr):
    @pl.when(pl.program_id(1) == 0)
    def _():
        _modulated_norm_rows(x_ref, nw_ref, sc_ref, sh_ref, h_scr)
        g_ref[...] = _dot(h_scr[...], wg_ref[...])

    o_ref[...] = _dot(h_scr[...], w_ref[...]).astype(BF16)


def _inproj(x2, nw, sc, sh, w_main, w_gate, seq):
    m, d = x2.shape
    nm = w_main.shape[1]
    tm = min(1024, seq)
    tn = next(t for t in (1792, 1024, 512, 256, 128) if nm % t == 0)
    bpr = seq // tm
    return pl.pallas_call(
        _inproj_kernel,
        grid=(m // tm, nm // tn),
        in_specs=[pl.BlockSpec((tm, d), lambda i, j: (i, 0)),
                  pl.BlockSpec((1, d), lambda i, j: (0, 0)),
                  pl.BlockSpec((1, 1, d), lambda i, j: (i // bpr, 0, 0)),
                  pl.BlockSpec((1, 1, d), lambda i, j: (i // bpr, 0, 0)),
                  pl.BlockSpec((d, tn), lambda i, j: (0, j)),
                  pl.BlockSpec((d, GATE_LANES), lambda i, j: (0, 0))],
        out_specs=[pl.BlockSpec((tm, tn), lambda i, j: (i, j)),
                   pl.BlockSpec((tm, GATE_LANES), lambda i, j: (i, 0))],
        out_shape=[jax.ShapeDtypeStruct((m, nm), BF16),
                   jax.ShapeDtypeStruct((m, GATE_LANES), F32)],
        scratch_shapes=[pltpu.VMEM((tm, d), BF16)],
        compiler_params=_params("arbitrary", "arbitrary"),
        name="inproj",
    )(x2, nw, sc, sh, w_main, w_gate)


def _gates_kernel(g_ref, bias_ref, cum_ref, o_ref, carry):
    @pl.when(pl.program_id(1) == 0)
    def _():
        carry[...] = jnp.zeros_like(carry)

    g = g_ref[0] + bias_ref[...]
    ls = _log_sigmoid(g)
    ts = ls.shape[0]
    row = lax.broadcasted_iota(jnp.int32, (ts, ts), 0)
    col = lax.broadcasted_iota(jnp.int32, (ts, ts), 1)
    tri = jnp.where(row >= col, 1.0, 0.0).astype(BF16)
    hi = ls.astype(BF16)
    r1 = ls - hi.astype(F32)
    mid = r1.astype(BF16)
    lo = (r1 - mid.astype(F32)).astype(BF16)
    csum = (_dot(tri, hi) + _dot(tri, mid)) + _dot(tri, lo) + carry[...]
    carry[...] = csum[ts - 1:ts, :]
    o_ref[0] = jnp.where(cum_ref[...] > 0.5, csum, g)


def _gates(g3, bias, cum_mask):
    b, s, n = g3.shape
    ts = min(512, s)
    return pl.pallas_call(
        _gates_kernel,
        grid=(b, s // ts),
        in_specs=[pl.BlockSpec((1, ts, n), lambda i, j: (i, j, 0)),
                  pl.BlockSpec((1, n), lambda i, j: (0, 0)),
                  pl.BlockSpec((1, n), lambda i, j: (0, 0))],
        out_specs=pl.BlockSpec((1, ts, n), lambda i, j: (i, j, 0)),
        out_shape=jax.ShapeDtypeStruct((b, s, n), F32),
        scratch_shapes=[pltpu.VMEM((1, n), F32)],
        compiler_params=_params("arbitrary", "arbitrary"),
        name="gates",
    )(g3, bias, cum_mask)


def _retention_kernel(q_ref, k_ref, v_ref, g_ref, cos_ref, sin_ref, ms_ref, o_ref, state, decay, *, heads):
    n = cos_ref.shape[1]
    log_gamma = [math.log(1.0 - 2.0 ** (-5.0 - h)) for h in range(heads)]

    @pl.when(pl.program_id(1) == 0)
    def _():
        state[...] = jnp.zeros_like(state)
        row = lax.broadcasted_iota(jnp.int32, (n, n), 0)
        col = lax.broadcasted_iota(jnp.int32, (n, n), 1)
        rel = (row - col).astype(F32)
        for h in range(heads):
            decay[h] = jnp.where(rel >= 0, jnp.exp(jnp.maximum(rel, 0.0) * log_gamma[h]), 0.0)

    cos = cos_ref[0]
    sin = sin_ref[0]
    half = HEAD_DIM // 2
    idx = lax.broadcasted_iota(jnp.int32, (n, 1), 0).astype(F32)

    def rot(t):
        return t * cos + pltpu.roll(t, half, 1) * sin

    for h in range(heads):
        lg = log_gamma[h]
        cs = slice(h * HEAD_DIM, (h + 1) * HEAD_DIM)
        q = rot(q_ref[0, :, cs].astype(F32))
        k = rot(k_ref[0, :, cs].astype(F32)) * (HEAD_DIM ** -0.5)
        v = v_ref[0, :, cs]
        q_decay = jnp.exp((idx + 1.0) * lg)
        k_decay = jnp.exp((n - 1.0 - idx) * lg)
        chunk_decay = math.exp(n * lg)

        scores = _dot_nt(q.astype(BF16), k.astype(BF16)) * decay[h]
        intra = _dot(scores.astype(BF16), v)
        st = state[h]
        inter = _dot((q * q_decay).astype(BF16), st.astype(BF16))
        state[h] = st * chunk_decay + _dot_tn((k * k_decay).astype(BF16), v)

        y = _head_norm(intra + inter)
        gate = g_ref[0, :, cs].astype(F32)
        o_ref[0, :, cs] = (y * (gate * _sigmoid(gate)) * ms_ref[:, cs]).astype(BF16)


def _retention(proj, cos, sin, merge_scale, heads, col0, chunk):
    b, s, _ = proj.shape
    w = heads * HEAD_DIM
    assert (col0 * HEAD_DIM) % w == 0
    blk0 = col0 * HEAD_DIM // w
    grp = lambda base: pl.BlockSpec((1, chunk, w), lambda i, c: (i, c, blk0 + base))
    tok = pl.BlockSpec((1, chunk, HEAD_DIM), lambda i, c: (i, c, 0))
    return pl.pallas_call(
        functools.partial(_retention_kernel, heads=heads),
        grid=(b, s // chunk),
        in_specs=[grp(0), grp(1), grp(2), grp(3), tok, tok,
                  pl.BlockSpec((1, w), lambda i, c: (0, 0))],
        out_specs=pl.BlockSpec((1, chunk, w), lambda i, c: (i, c, 0)),
        out_shape=jax.ShapeDtypeStruct((b, s, w), BF16),
        scratch_shapes=[pltpu.VMEM((heads, HEAD_DIM, HEAD_DIM), F32),
                        pltpu.VMEM((heads, chunk, chunk), F32)],
        compiler_params=_params("arbitrary", "arbitrary"),
        name="retention",
    )(proj, proj, proj, proj, cos, sin, merge_scale)


def _fox_keys_kernel(k_ref, gp_ref, o_ref, *, tk, heads):
    ts = k_ref.shape[1]
    c = gp_ref[0]
    parts = []
    for i in range(ts // tk):
        blk = c[i * tk:(i + 1) * tk]
        parts.append((blk[0:1, :] - blk) * LOG2E)
    x = jnp.concatenate(parts, axis=0) if len(parts) > 1 else parts[0]
    hi = x.astype(BF16)
    r1 = x - hi.astype(F32)
    mid = r1.astype(BF16)
    lo = (r1 - mid.astype(F32)).astype(BF16)
    row = lax.broadcasted_iota(jnp.int32, (GATE_LANES, HEAD_DIM), 0)
    col = lax.broadcasted_iota(jnp.int32, (GATE_LANES, HEAD_DIM), 1)
    for h in range(heads):
        sel = lambda piece: jnp.where((row == h) & (col == piece), 1.0, 0.0).astype(BF16)
        bias = _dot(hi, sel(0)) + _dot(mid, sel(1)) + _dot(lo, sel(2))
        o_ref[0, h] = jnp.concatenate([k_ref[0, :, h * HEAD_DIM:(h + 1) * HEAD_DIM], bias.astype(BF16)], axis=1)


def _fox_keys(proj, gp, heads, k_col0, tk):
    b, s, _ = proj.shape
    ts = min(1024, s)
    w = heads * HEAD_DIM
    assert (k_col0 * HEAD_DIM) % w == 0 and ts % tk == 0
    return pl.pallas_call(
        functools.partial(_fox_keys_kernel, tk=tk, heads=heads),
        grid=(b, s // ts),
        in_specs=[pl.BlockSpec((1, ts, w), lambda i, j: (i, j, k_col0 * HEAD_DIM // w)),
                  pl.BlockSpec((1, ts, GATE_LANES), lambda i, j: (i, j, 0))],
        out_specs=pl.BlockSpec((1, heads, ts, 2 * HEAD_DIM), lambda i, j: (i, 0, j, 0)),
        out_shape=jax.ShapeDtypeStruct((b, heads, s, 2 * HEAD_DIM), BF16),
        compiler_params=_params("arbitrary", "arbitrary"),
        name="fox_keys",
    )(proj, gp)


def _fox_kernel(q_ref, k_ref, v_ref, c_ref, ms_ref, o_ref, s_scr, m_scr, l_scr, acc_scr,
                *, tq, tk, unroll):
    ndiag = tq // tk
    static_slots = unroll % 2 == 0 and ndiag % 2 == 0
    qi = pl.program_id(2)
    nfull = qi * ndiag
    q0 = pl.multiple_of(qi * tq, tq)
    def augmented_queries(start):
        q_t = (q_ref[0, pl.ds(start, tq), :].astype(F32) * (HEAD_DIM ** -0.5 * LOG2E)).T.astype(BF16)
        row = lax.broadcasted_iota(jnp.int32, (HEAD_DIM, tq), 0)
        return jnp.concatenate([q_t, jnp.where(row < 3, 1.0, 0.0).astype(BF16)], axis=0)

    q_aug = augmented_queries(q0)
    c_first = c_ref[0, :, pl.ds(q0, 128)][:, 0:1]

    m_scr[...] = jnp.full_like(m_scr, -jnp.inf)
    l_scr[...] = jnp.zeros_like(l_scr)
    acc_scr[...] = jnp.zeros_like(acc_scr)

    def scores(j, slot, c0):
        k0 = pl.multiple_of(j * tk, tk)
        s_scr[slot, :, c0:] = _dot(k_ref[0, 0, pl.ds(k0, tk), :], q_aug[:, c0:])

    def softmax(j, slot, diag):
        k0 = pl.multiple_of(j * tk, tk)
        c0 = 0 if diag is None else diag * tk
        delta = (c_first - c_ref[0, :, pl.ds(k0, 128)][:, 0:1]) * LOG2E
        t = s_scr[slot, :, c0:]
        if diag is not None:
            key = lax.broadcasted_iota(jnp.int32, t.shape, 0)
            qry = lax.broadcasted_iota(jnp.int32, t.shape, 1)
            t = jnp.where(qry >= key, t, -jnp.inf)
        m_prev = m_scr[:, c0:]
        m_new = jnp.maximum(m_prev, jnp.max(t, axis=0, keepdims=True) + delta)
        p = jnp.exp2(t - (m_new - delta)).astype(BF16)
        alpha = jnp.exp2(m_prev - m_new)
        half = p[:tk // 2] + p[tk // 2:]
        quarter = half[:tk // 4] + half[tk // 4:]
        l_scr[:, c0:] = alpha * l_scr[:, c0:] + jnp.sum(quarter.astype(F32), axis=0, keepdims=True)
        m_scr[:, c0:] = m_new
        acc_scr[:, c0:] = alpha * acc_scr[:, c0:] + _dot_tn(v_ref[0, pl.ds(k0, tk), :], p)

    def step(j, slot, diag, last):
        if not last:
            scores(j + 1, 1 - slot, 0 if diag is None else (diag + 1) * tk)
        softmax(j, slot, diag)

    @pl.when(qi == 0)
    def _():
        scores(0, 0, 0)

    def body(i, carry):
        for u in range(unroll):
            j = i * unroll + u
            step(j, u % 2 if static_slots else j & 1, None, False)
        return carry

    lax.fori_loop(0, nfull // unroll, body, 0)
    for d in range(ndiag):
        j = nfull + d
        slot = d % 2 if ndiag % 2 == 0 else j & 1
        step(j, slot, d, d == ndiag - 1)

    def finalize():
        out = (acc_scr[...] / l_scr[...]).T
        o_ref[0] = (_head_norm(out) * ms_ref[...]).astype(BF16)

    @pl.when(qi + 1 < pl.num_programs(2))
    def _():
        s_scr[0] = _dot(k_ref[0, 0, 0:tk, :], augmented_queries(pl.multiple_of(q0 + tq, tq)))
        finalize()

    @pl.when(qi + 1 == pl.num_programs(2))
    def _():
        finalize()


def _fox(proj, k_aug, c_rows, merge_scale, heads, col0, ms_col0, tq, tk, unroll):
    b, s, _ = proj.shape
    assert tq % tk == 0 and (tq // tk) % unroll == 0
    return pl.pallas_call(
        functools.partial(_fox_kernel, tq=tq, tk=tk, unroll=unroll),
        grid=(b, heads, s // tq),
        in_specs=[pl.BlockSpec((1, s, HEAD_DIM), lambda i, h, q: (i, 0, col0 + h)),
                  pl.BlockSpec((1, 1, s, 2 * HEAD_DIM), lambda i, h, q: (i, h, 0, 0)),
                  pl.BlockSpec((1, s, HEAD_DIM), lambda i, h, q: (i, 0, col0 + 2 * heads + h)),
                  pl.BlockSpec((1, 1, s), lambda i, h, q: (i * heads + h, 0, 0)),
                  pl.BlockSpec((1, HEAD_DIM), lambda i, h, q: (0, ms_col0 + h))],
        out_specs=pl.BlockSpec((1, tq, HEAD_DIM), lambda i, h, q: (i, q, h)),
        out_shape=jax.ShapeDtypeStruct((b, s, heads * HEAD_DIM), BF16),
        scratch_shapes=[pltpu.VMEM((2, tk, tq), F32),
                        pltpu.VMEM((1, tq), F32),
                        pltpu.VMEM((1, tq), F32),
                        pltpu.VMEM((HEAD_DIM, tq), F32)],
        compiler_params=_params("arbitrary", "arbitrary", "arbitrary"),
        name="fox_attention",
    )(proj, k_aug, proj, c_rows, merge_scale)


def _mlstm_kernel(q_ref, k_ref, v_ref, og_ref, w_ref, b_ref, gp_ref, ir_ref, gr_ref, ms_ref, o_ref,
                  c_scr, n_scr, m_scr, g0_scr, xq_scr, xk_scr, *, heads, i_lane0, f_lane0):
    n = q_ref.shape[1]
    w = heads * HEAD_DIM

    @pl.when(pl.program_id(1) == 0)
    def _():
        c_scr[...] = jnp.zeros_like(c_scr)
        n_scr[...] = jnp.zeros_like(n_scr)
        m_scr[...] = jnp.zeros_like(m_scr)
        g0_scr[...] = jnp.zeros_like(g0_scr)
        xq_scr[0:CONV_HALO, :] = jnp.zeros((CONV_HALO, w), F32)
        xk_scr[0:CONV_HALO, :] = jnp.zeros((CONV_HALO, w), F32)

    def conv_silu(x_ref, x_scr, cols):
        x_scr[CONV_HALO:CONV_HALO + n, :] = x_ref[0].astype(F32)
        y = b_ref[:, cols] + jnp.zeros((n, w), F32)
        for j in range(CONV_WIDTH):
            shift = CONV_WIDTH - 1 - j
            y = y + w_ref[j:j + 1, cols] * x_scr[CONV_HALO - shift:CONV_HALO - shift + n, :]
        x_scr[0:CONV_HALO, :] = x_scr[n:n + CONV_HALO, :]
        return y * _sigmoid(y)

    q_all = conv_silu(q_ref, xq_scr, slice(0, w))
    k_all = conv_silu(k_ref, xk_scr, slice(w, 2 * w)) * (HEAD_DIM ** -0.5)
    gp = gp_ref[0]
    row = lax.broadcasted_iota(jnp.int32, (n, n), 0)
    col = lax.broadcasted_iota(jnp.int32, (n, n), 1)
    causal = row >= col

    for h in range(heads):
        cs = slice(h * HEAD_DIM, (h + 1) * HEAD_DIM)
        q = q_all[:, cs]
        k = k_all[:, cs]
        v = v_ref[0, :, cs]
        ic = gp[:, i_lane0 + h:i_lane0 + h + 1]
        gc = gp[:, f_lane0 + h:f_lane0 + h + 1]
        ir = ir_ref[0, h:h + 1, :]
        gr = gr_ref[0, h:h + 1, :]

        g0 = g0_scr[h][:, 0:1]
        m_prev = m_scr[h][:, 0:1]
        b_col = gc - g0
        dmat = jnp.where(causal, gc + (ir - gr), -jnp.inf)
        inter_log = b_col + m_prev
        m_q = jnp.maximum(inter_log, jnp.max(dmat, axis=-1, keepdims=True))
        w_intra = jnp.exp(dmat - m_q)
        w_inter = jnp.exp(inter_log - m_q)

        qb = q.astype(BF16)
        s = _dot_nt(qb, k.astype(BF16)) * w_intra
        cst = c_scr[h]
        nst = n_scr[h]
        num = _dot(s.astype(BF16), v) + w_inter * _dot(qb, cst.astype(BF16))
        den = jnp.sum(s, axis=-1, keepdims=True) + w_inter * jnp.sum(q * nst, axis=-1, keepdims=True)
        hid = num / jnp.maximum(jnp.abs(den), jnp.exp(-m_q))

        b_last = b_col[n - 1:n, :]
        k_log = b_last - b_col + ic
        m_new = jnp.maximum(b_last + m_prev, jnp.max(k_log, axis=0, keepdims=True))
        wk = jnp.exp(k_log - m_new)
        carry_scale = jnp.exp(b_last + m_prev - m_new)
        kw = k * wk
        c_scr[h] = carry_scale * cst + _dot_tn(kw.astype(BF16), v)
        n_scr[h] = carry_scale * nst + jnp.sum(kw, axis=0, keepdims=True)
        m_scr[h] = jnp.broadcast_to(m_new, (1, HEAD_DIM))
        g0_scr[h] = jnp.broadcast_to(gc[n - 1:n, :], (1, HEAD_DIM))

        og = og_ref[0, :, cs].astype(F32)
        o_ref[0, :, cs] = (_head_norm(hid) * _sigmoid(og) * ms_ref[:, cs]).astype(BF16)


def _mlstm(proj, conv_w, conv_b, gp, i_rows, g_rows, merge_scale, heads, col0, ms_col0, i_lane0, f_lane0, chunk):
    b, s, _ = proj.shape
    w = heads * HEAD_DIM
    assert (col0 * HEAD_DIM) % w == 0 and (ms_col0 * HEAD_DIM) % w == 0
    blk0 = col0 * HEAD_DIM // w
    grp = lambda base: pl.BlockSpec((1, chunk, w), lambda i, c: (i, c, blk0 + base))
    row_spec = pl.BlockSpec((1, heads, chunk), lambda i, c: (i, 0, c))
    state = lambda r, c: pltpu.VMEM((heads, r, c), F32)
    return pl.pallas_call(
        functools.partial(_mlstm_kernel, heads=heads, i_lane0=i_lane0, f_lane0=f_lane0),
        grid=(b, s // chunk),
        in_specs=[grp(0), grp(1), grp(2), grp(3),
                  pl.BlockSpec(conv_w.shape, lambda i, c: (0, 0)),
                  pl.BlockSpec(conv_b.shape, lambda i, c: (0, 0)),
                  pl.BlockSpec((1, chunk, GATE_LANES), lambda i, c: (i, c, 0)),
                  row_spec, row_spec,
                  pl.BlockSpec((1, w), lambda i, c: (0, ms_col0 * HEAD_DIM // w))],
        out_specs=pl.BlockSpec((1, chunk, w), lambda i, c: (i, c, 0)),
        out_shape=jax.ShapeDtypeStruct((b, s, w), BF16),
        scratch_shapes=[state(HEAD_DIM, HEAD_DIM), state(1, HEAD_DIM), state(1, HEAD_DIM), state(1, HEAD_DIM),
                        pltpu.VMEM((chunk + CONV_HALO, w), F32),
                        pltpu.VMEM((chunk + CONV_HALO, w), F32)],
        compiler_params=_params("arbitrary", "arbitrary"),
        name="mlstm",
    )(proj, proj, proj, proj, conv_w, conv_b, gp, i_rows, g_rows, merge_scale)


def _outproj_kernel(yr_ref, yf_ref, ym_ref, w_ref, x_ref, ga_ref, o_ref):
    r = yr_ref.shape[1]
    f = yf_ref.shape[1]
    y = _dot(yr_ref[...], w_ref[0:r, :]) + _dot(yf_ref[...], w_ref[r:r + f, :])
    y = y + _dot(ym_ref[...], w_ref[r + f:, :])
    o_ref[...] = x_ref[...] + ga_ref[0] * y


def _outproj(y_ret, y_fox, y_m, w_out, x2, g_a, seq):
    m, d = x2.shape
    tm = min(512, seq)
    bpr = seq // tm
    rows = lambda w: pl.BlockSpec((tm, w), lambda i: (i, 0))
    return pl.pallas_call(
        _outproj_kernel,
        grid=(m // tm,),
        in_specs=[rows(y_ret.shape[1]), rows(y_fox.shape[1]), rows(y_m.shape[1]),
                  pl.BlockSpec(w_out.shape, lambda i: (0, 0)),
                  rows(d),
                  pl.BlockSpec((1, 1, d), lambda i: (i // bpr, 0, 0))],
        out_specs=rows(d),
        out_shape=jax.ShapeDtypeStruct((m, d), F32),
        compiler_params=_params("arbitrary"),
        name="outproj",
    )(y_ret, y_fox, y_m, w_out, x2, g_a)


def _mlp_kernel(x_ref, nw_ref, sc_ref, sh_ref, gm_ref, w1_hbm, w2_hbm, fw_ref, o_ref,
                w1_buf, w2_buf, sem, h_scr, acc_scr, *, final_norm, tf, nf):
    i = pl.program_id(0)
    n = pl.num_programs(0)

    def tile_copies(f, slot):
        c0 = pl.multiple_of(f * tf, tf)
        return (pltpu.make_async_copy(w1_hbm.at[:, pl.ds(c0, tf)], w1_buf.at[slot], sem.at[0, slot]),
                pltpu.make_async_copy(w2_hbm.at[pl.ds(c0, tf), :], w2_buf.at[slot], sem.at[1, slot]))

    @pl.when(i == 0)
    def _():
        for cp in tile_copies(0, 0):
            cp.start()

    _modulated_norm_rows(x_ref, nw_ref, sc_ref, sh_ref, h_scr)
    acc_scr[...] = jnp.zeros_like(acc_scr)

    def body(f, carry):
        slot = f % 2

        @pl.when((f + 1 < nf) | (i + 1 < n))
        def _():
            for cp in tile_copies((f + 1) % nf, 1 - slot):
                cp.start()

        for cp in tile_copies(f, slot):
            cp.wait()
        a = jnp.maximum(_dot(h_scr[...], w1_buf[slot]), 0.0)
        acc_scr[...] += _dot((a * a).astype(BF16), w2_buf[slot])
        return carry

    lax.fori_loop(0, nf, body, 0)

    y = x_ref[...] + gm_ref[0] * acc_scr[...]
    if final_norm:
        ms = jnp.mean(y * y, axis=-1, keepdims=True)
        y = y * lax.rsqrt(ms + NORM_EPS) * fw_ref[...]
    o_ref[...] = y


def _mlp(x2, nw, sc, sh, g_m, w1, w2, final_w, seq, final_norm):
    m, d = x2.shape
    dff = w1.shape[1]
    tm = min(512, seq)
    tf = min(1024, dff)
    nf = dff // tf
    assert nf % 2 == 0
    bpr = seq // tm
    mod = pl.BlockSpec((1, 1, d), lambda i: (i // bpr, 0, 0))
    vec = pl.BlockSpec((1, d), lambda i: (0, 0))
    hbm = pl.BlockSpec(memory_space=pl.ANY)
    return pl.pallas_call(
        functools.partial(_mlp_kernel, final_norm=final_norm, tf=tf, nf=nf),
        grid=(m // tm,),
        in_specs=[pl.BlockSpec((tm, d), lambda i: (i, 0)),
                  vec, mod, mod, mod, hbm, hbm, vec],
        out_specs=pl.BlockSpec((tm, d), lambda i: (i, 0)),
        out_shape=jax.ShapeDtypeStruct((m, d), F32),
        scratch_shapes=[pltpu.VMEM((2, d, tf), BF16),
                        pltpu.VMEM((2, tf, d), BF16),
                        pltpu.SemaphoreType.DMA((2, 2)),
                        pltpu.VMEM((tm, d), BF16),
                        pltpu.VMEM((tm, d), F32)],
        compiler_params=_params("arbitrary"),
        name="mlp",
    )(x2, nw, sc, sh, g_m, w1, w2, final_w)


def kernel(x, c, positions, ada_w, ada_b, norm_mix_w, norm_mlp_w, w_in, conv_w, conv_b, fox_f_bias,
           mlstm_i_bias, mlstm_f_bias, merge_scale, w_out, w_ff1, w_ff2, final_norm_w):
    b, s, d = x.shape
    depth = ada_w.shape[0]
    ret_w, fox_w, ml_w = d // 4, d // 2, d // 4
    ret_h, fox_h, ml_h = ret_w // HEAD_DIM, fox_w // HEAD_DIM, ml_w // HEAD_DIM
    assert s % 128 == 0 and d % (4 * HEAD_DIM) == 0
    assert fox_h + 2 * ml_h <= GATE_LANES

    sizes = [ret_w] * 4 + [fox_w] * 3 + [fox_h] + [ml_w] * 4 + [ml_h, ml_h]
    offs = [0]
    for sz in sizes:
        offs.append(offs[-1] + sz)
    n_gate = fox_h + 2 * ml_h
    ret_col0 = 0
    fox_col0 = (4 * ret_w) // HEAD_DIM
    ml_col0 = (4 * ret_w + 3 * fox_w) // HEAD_DIM
    i_lane0, f_lane0 = fox_h, fox_h + ml_h

    ret_chunk = min(RET_CHUNK, s)
    ml_chunk = min(MLSTM_CHUNK, s)
    tq = min(FOX_TQ, s)
    tk = min(FOX_TK, tq)

    mod = _ada_mod(c, ada_w, ada_b)
    cos, sin = _rope_tables(positions)
    cum_mask = jnp.zeros((1, GATE_LANES), F32).at[0, :fox_h].set(1.0).at[0, f_lane0:f_lane0 + ml_h].set(1.0)

    x2 = x.reshape(b * s, d)
    for layer in range(depth):
        sh_a, sc_a, g_a, sh_m, sc_m, g_m = [t.reshape(b, 1, d) for t in jnp.split(mod[layer], N_MOD, axis=-1)]
        wl = w_in[layer]
        w_main = jnp.concatenate([wl[:, offs[0]:offs[7]], wl[:, offs[8]:offs[12]]], axis=1).astype(BF16)
        w_gate = jnp.concatenate([wl[:, offs[7]:offs[8]], wl[:, offs[12]:offs[14]],
                                  jnp.zeros((d, GATE_LANES - n_gate), F32)], axis=1).astype(BF16)
        gate_bias = jnp.pad(jnp.concatenate([fox_f_bias[layer], mlstm_i_bias[layer], mlstm_f_bias[layer]]),
                            (0, GATE_LANES - n_gate)).reshape(1, GATE_LANES)
        ms = merge_scale[layer].reshape(1, d)

        proj, gates = _inproj(x2, norm_mix_w[layer].reshape(1, d), sc_a, sh_a, w_main, w_gate, s)
        proj = proj.reshape(b, s, -1)
        gp = _gates(gates.reshape(b, s, GATE_LANES), gate_bias, cum_mask)
        gp_t = jnp.swapaxes(gp[:, :, :n_gate], 1, 2)
        c_rows = gp_t[:, :fox_h].reshape(b * fox_h, 1, s)
        i_rows = gp_t[:, i_lane0:i_lane0 + ml_h]
        g_rows = gp_t[:, f_lane0:f_lane0 + ml_h]

        y_ret = _retention(proj, cos, sin, ms, ret_h, ret_col0, ret_chunk)
        k_aug = _fox_keys(proj, gp, fox_h, fox_col0 + fox_h, tk)
        y_fox = _fox(proj, k_aug, c_rows, ms, fox_h, fox_col0, ret_h, tq, tk, FOX_UNROLL if tq // tk % FOX_UNROLL == 0 else 1)
        y_m = _mlstm(proj, conv_w[layer], conv_b[layer].reshape(1, -1), gp, i_rows, g_rows, ms,
                     ml_h, ml_col0, ret_h + fox_h, i_lane0, f_lane0, ml_chunk)

        x2 = _outproj(y_ret.reshape(b * s, -1), y_fox.reshape(b * s, -1), y_m.reshape(b * s, -1),
                      w_out[layer].astype(BF16), x2, g_a, s)
        x2 = _mlp(x2, norm_mlp_w[layer].reshape(1, d), sc_m, sh_m, g_m,
                  w_ff1[layer].astype(BF16), w_ff2[layer].astype(BF16),
                  final_norm_w.reshape(1, d), s, layer == depth - 1)
    return x2.reshape(b, s, d)
```

```python
import functools
import math

import jax
import jax.numpy as jnp
from jax import lax
from jax.experimental import pallas as pl
from jax.experimental.pallas import tpu as pltpu

HEAD_DIM = 128
CONV_WIDTH = 4
ROPE_BASE = 10000.0
NORM_EPS = 1e-6
N_MOD = 6
GATE_LANES = 128
LOG2E = math.log2(math.e)
FOX_TQ, FOX_TK, FOX_UNROLL = 1024, 512, 2
RET_CHUNK, MLSTM_CHUNK = 256, 512
CONV_HALO = 8

F32 = jnp.float32
BF16 = jnp.bfloat16

VMEM_LIMIT_BYTES = 60 * 1024 * 1024


def _params(*sem, flags=None):
    return pltpu.CompilerParams(dimension_semantics=sem, vmem_limit_bytes=VMEM_LIMIT_BYTES, flags=flags)


def _dot(a, b):
    return jnp.dot(a, b, preferred_element_type=F32)


def _dot_nt(a, b):
    return lax.dot_general(a, b, (((1,), (1,)), ((), ())), preferred_element_type=F32)


def _dot_tn(a, b):
    return lax.dot_general(a, b, (((0,), (0,)), ((), ())), preferred_element_type=F32)


def _sigmoid(x):
    return 1.0 / (1.0 + jnp.exp(-x))


def _log_sigmoid(x):
    return jnp.minimum(x, 0.0) - jnp.log1p(jnp.exp(-jnp.abs(x)))


def _head_norm(y):
    mu = jnp.mean(y, axis=-1, keepdims=True)
    yc = y - mu
    var = jnp.mean(yc * yc, axis=-1, keepdims=True)
    return yc * lax.rsqrt(var + NORM_EPS)


def _ada_kernel(ct_ref, w_ref, b_ref, o_ref, *, kc):
    d, nb = ct_ref.shape
    tn = w_ref.shape[2]

    def body(i, accs):
        r0 = pl.multiple_of(i * kc, kc)
        ck = ct_ref[pl.ds(r0, kc), :]
        ck = ck * _sigmoid(ck)
        wk = w_ref[0, pl.ds(r0, kc), :]
        return tuple(acc + jnp.sum(wk * ck[:, b:b + 1], axis=0, keepdims=True)
                     for b, acc in enumerate(accs))

    accs = lax.fori_loop(0, d // kc, body, tuple(jnp.zeros((1, tn), F32) for _ in range(nb)))
    o_ref[0] = jnp.concatenate(accs, axis=0) + b_ref[0]


def _ada_mod(c, ada_w, ada_b):
    depth, d, n = ada_w.shape
    nb = c.shape[0]
    tn = min(1024, n)
    kc = min(256, d)
    return pl.pallas_call(
        functools.partial(_ada_kernel, kc=kc),
        grid=(depth, n // tn),
        in_specs=[pl.BlockSpec((d, nb), lambda l, j: (0, 0)),
                  pl.BlockSpec((1, d, tn), lambda l, j: (l, 0, j)),
                  pl.BlockSpec((1, 1, tn), lambda l, j: (l, 0, j))],
        out_specs=pl.BlockSpec((1, nb, tn), lambda l, j: (l, 0, j)),
        out_shape=jax.ShapeDtypeStruct((depth, nb, n), F32),
        compiler_params=_params("arbitrary", "arbitrary"),
        name="ada_mod",
    )(c.T, ada_w, ada_b.reshape(depth, 1, n))


def _rope_kernel(pos_ref, invf_ref, sign_ref, cos_ref, sin_ref):
    ang = pos_ref[0] * invf_ref[...]
    cos_ref[0] = jnp.cos(ang)
    sin_ref[0] = jnp.sin(ang) * sign_ref[...]


def _rope_tables(positions):
    b, s = positions.shape
    ts = min(2048, s)
    half = HEAD_DIM // 2
    inv_freq = ROPE_BASE ** (-jnp.arange(0, HEAD_DIM, 2, dtype=F32) / HEAD_DIM)
    invf = jnp.concatenate([inv_freq, inv_freq]).reshape(1, HEAD_DIM)
    sign = jnp.concatenate([-jnp.ones((half,), F32), jnp.ones((half,), F32)]).reshape(1, HEAD_DIM)
    pos = positions.astype(F32).reshape(b, s, 1)
    out = jax.ShapeDtypeStruct((b, s, HEAD_DIM), F32)
    return pl.pallas_call(
        _rope_kernel,
        grid=(b, s // ts),
        in_specs=[pl.BlockSpec((1, ts, 1), lambda i, j: (i, j, 0)),
                  pl.BlockSpec((1, HEAD_DIM), lambda i, j: (0, 0)),
                  pl.BlockSpec((1, HEAD_DIM), lambda i, j: (0, 0))],
        out_specs=[pl.BlockSpec((1, ts, HEAD_DIM), lambda i, j: (i, j, 0))] * 2,
        out_shape=[out, out],
        compiler_params=_params("arbitrary", "arbitrary"),
        name="rope_tables",
    )(pos, invf, sign)


def _modulated_norm_rows(x_ref, nw_ref, sc_ref, sh_ref, h_scr):
    x = x_ref[...]
    inv = lax.rsqrt(jnp.mean(x * x, axis=-1, keepdims=True) + NORM_EPS)
    h_scr[...] = ((x_ref[...] * inv * nw_ref[...]) * (1.0 + sc_ref[0]) + sh_ref[0]).astype(BF16)


def _modulated_norm_slice(xn_ref, r0, rows, nw_ref, sc_ref, sh_ref, h_scr):
    xn = xn_ref[pl.ds(r0, rows), :]
    inv = lax.rsqrt(jnp.mean(xn * xn, axis=-1, keepdims=True) + NORM_EPS)
    h_scr[pl.ds(r0, rows), :] = ((xn * inv * nw_ref[...]) * (1.0 + sc_ref[0]) + sh_ref[0]).astype(BF16)


def _inproj_kernel(x_ref, nw_ref, sc_ref, sh_ref, w_ref, wg_ref, o_ref, g_ref, h_scr):
    @pl.when(pl.program_id(1) == 0)
    def _():
        _modulated_norm_rows(x_ref, nw_ref, sc_ref, sh_ref, h_scr)
        g_ref[...] = _dot(h_scr[...], wg_ref[...])

    o_ref[...] = _dot(h_scr[...], w_ref[...]).astype(BF16)


def _inproj(x2, nw, sc, sh, w_main, w_gate, seq):
    m, d = x2.shape
    nm = w_main.shape[1]
    tm = min(1024, seq)
    tn = next(t for t in (1792, 1024, 512, 256, 128) if nm % t == 0)
    bpr = seq // tm
    return pl.pallas_call(
        _inproj_kernel,
        grid=(m // tm, nm // tn),
        in_specs=[pl.BlockSpec((tm, d), lambda i, j: (i, 0)),
                  pl.BlockSpec((1, d), lambda i, j: (0, 0)),
                  pl.BlockSpec((1, 1, d), lambda i, j: (i // bpr, 0, 0)),
                  pl.BlockSpec((1, 1, d), lambda i, j: (i // bpr, 0, 0)),
                  pl.BlockSpec((d, tn), lambda i, j: (0, j)),
                  pl.BlockSpec((d, GATE_LANES), lambda i, j: (0, 0))],
        out_specs=[pl.BlockSpec((tm, tn), lambda i, j: (i, j)),
                   pl.BlockSpec((tm, GATE_LANES), lambda i, j: (i, 0))],
        out_shape=[jax.ShapeDtypeStruct((m, nm), BF16),
                   jax.ShapeDtypeStruct((m, GATE_LANES), F32)],
        scratch_shapes=[pltpu.VMEM((tm, d), BF16)],
        compiler_params=_params("arbitrary", "arbitrary"),
        name="inproj",
    )(x2, nw, sc, sh, w_main, w_gate)


def _gates_kernel(g_ref, bias_ref, cum_ref, o_ref, carry):
    @pl.when(pl.program_id(1) == 0)
    def _():
        carry[...] = jnp.zeros_like(carry)

    g = g_ref[0] + bias_ref[...]
    ls = _log_sigmoid(g)
    ts = ls.shape[0]
    row = lax.broadcasted_iota(jnp.int32, (ts, ts), 0)
    col = lax.broadcasted_iota(jnp.int32, (ts, ts), 1)
    tri = jnp.where(row >= col, 1.0, 0.0).astype(BF16)
    hi = ls.astype(BF16)
    r1 = ls - hi.astype(F32)
    mid = r1.astype(BF16)
    lo = (r1 - mid.astype(F32)).astype(BF16)
    csum = (_dot(tri, hi) + _dot(tri, mid)) + _dot(tri, lo) + carry[...]
    carry[...] = csum[ts - 1:ts, :]
    o_ref[0] = jnp.where(cum_ref[...] > 0.5, csum, g)


def _gates(g3, bias, cum_mask):
    b, s, n = g3.shape
    ts = min(512, s)
    return pl.pallas_call(
        _gates_kernel,
        grid=(b, s // ts),
        in_specs=[pl.BlockSpec((1, ts, n), lambda i, j: (i, j, 0)),
                  pl.BlockSpec((1, n), lambda i, j: (0, 0)),
                  pl.BlockSpec((1, n), lambda i, j: (0, 0))],
        out_specs=pl.BlockSpec((1, ts, n), lambda i, j: (i, j, 0)),
        out_shape=jax.ShapeDtypeStruct((b, s, n), F32),
        scratch_shapes=[pltpu.VMEM((1, n), F32)],
        compiler_params=_params("arbitrary", "arbitrary"),
        name="gates",
    )(g3, bias, cum_mask)


def _retention_kernel(q_ref, k_ref, v_ref, g_ref, cos_ref, sin_ref, ms_ref, o_ref, state, decay, *, heads):
    n = cos_ref.shape[1]
    log_gamma = [math.log(1.0 - 2.0 ** (-5.0 - h)) for h in range(heads)]

    @pl.when(pl.program_id(1) == 0)
    def _():
        state[...] = jnp.zeros_like(state)
        row = lax.broadcasted_iota(jnp.int32, (n, n), 0)
        col = lax.broadcasted_iota(jnp.int32, (n, n), 1)
        rel = (row - col).astype(F32)
        for h in range(heads):
            decay[h] = jnp.where(rel >= 0, jnp.exp(jnp.maximum(rel, 0.0) * log_gamma[h]), 0.0)

    cos = cos_ref[0]
    sin = sin_ref[0]
    half = HEAD_DIM // 2
    idx = lax.broadcasted_iota(jnp.int32, (n, 1), 0).astype(F32)

    def rot(t):
        return t * cos + pltpu.roll(t, half, 1) * sin

    for h in range(heads):
        lg = log_gamma[h]
        cs = slice(h * HEAD_DIM, (h + 1) * HEAD_DIM)
        q = rot(q_ref[0, :, cs].astype(F32))
        k = rot(k_ref[0, :, cs].astype(F32)) * (HEAD_DIM ** -0.5)
        v = v_ref[0, :, cs]
        q_decay = jnp.exp((idx + 1.0) * lg)
        k_decay = jnp.exp((n - 1.0 - idx) * lg)
        chunk_decay = math.exp(n * lg)

        scores = _dot_nt(q.astype(BF16), k.astype(BF16)) * decay[h]
        intra = _dot(scores.astype(BF16), v)
        st = state[h]
        inter = _dot((q * q_decay).astype(BF16), st.astype(BF16))
        state[h] = st * chunk_decay + _dot_tn((k * k_decay).astype(BF16), v)

        y = _head_norm(intra + inter)
        gate = g_ref[0, :, cs].astype(F32)
        o_ref[0, :, cs] = (y * (gate * _sigmoid(gate)) * ms_ref[:, cs]).astype(BF16)


def _retention(proj, cos, sin, merge_scale, heads, col0, chunk):
    b, s, _ = proj.shape
    w = heads * HEAD_DIM
    assert (col0 * HEAD_DIM) % w == 0
    blk0 = col0 * HEAD_DIM // w
    grp = lambda base: pl.BlockSpec((1, chunk, w), lambda i, c: (i, c, blk0 + base))
    tok = pl.BlockSpec((1, chunk, HEAD_DIM), lambda i, c: (i, c, 0))
    return pl.pallas_call(
        functools.partial(_retention_kernel, heads=heads),
        grid=(b, s // chunk),
        in_specs=[grp(0), grp(1), grp(2), grp(3), tok, tok,
                  pl.BlockSpec((1, w), lambda i, c: (0, 0))],
        out_specs=pl.BlockSpec((1, chunk, w), lambda i, c: (i, c, 0)),
        out_shape=jax.ShapeDtypeStruct((b, s, w), BF16),
        scratch_shapes=[pltpu.VMEM((heads, HEAD_DIM, HEAD_DIM), F32),
                        pltpu.VMEM((heads, chunk, chunk), F32)],
        compiler_params=_params("arbitrary", "arbitrary"),
        name="retention",
    )(proj, proj, proj, proj, cos, sin, merge_scale)


def _fox_keys_kernel(k_ref, gp_ref, o_ref, *, tk, heads):
    ts = k_ref.shape[1]
    c = gp_ref[0]
    parts = []
    for i in range(ts // tk):
        blk = c[i * tk:(i + 1) * tk]
        parts.append((blk[0:1, :] - blk) * LOG2E)
    x = jnp.concatenate(parts, axis=0) if len(parts) > 1 else parts[0]
    hi = x.astype(BF16)
    r1 = x - hi.astype(F32)
    mid = r1.astype(BF16)
    lo = (r1 - mid.astype(F32)).astype(BF16)
    row = lax.broadcasted_iota(jnp.int32, (GATE_LANES, HEAD_DIM), 0)
    col = lax.broadcasted_iota(jnp.int32, (GATE_LANES, HEAD_DIM), 1)
    for h in range(heads):
        sel = lambda piece: jnp.where((row == h) & (col == piece), 1.0, 0.0).astype(BF16)
        bias = _dot(hi, sel(0)) + _dot(mid, sel(1)) + _dot(lo, sel(2))
        o_ref[0, h] = jnp.concatenate([k_ref[0, :, h * HEAD_DIM:(h + 1) * HEAD_DIM], bias.astype(BF16)], axis=1)


def _fox_keys(proj, gp, heads, k_col0, tk):
    b, s, _ = proj.shape
    ts = min(1024, s)
    w = heads * HEAD_DIM
    assert (k_col0 * HEAD_DIM) % w == 0 and ts % tk == 0
    return pl.pallas_call(
        functools.partial(_fox_keys_kernel, tk=tk, heads=heads),
        grid=(b, s // ts),
        in_specs=[pl.BlockSpec((1, ts, w), lambda i, j: (i, j, k_col0 * HEAD_DIM // w)),
                  pl.BlockSpec((1, ts, GATE_LANES), lambda i, j: (i, j, 0))],
        out_specs=pl.BlockSpec((1, heads, ts, 2 * HEAD_DIM), lambda i, j: (i, 0, j, 0)),
        out_shape=jax.ShapeDtypeStruct((b, heads, s, 2 * HEAD_DIM), BF16),
        compiler_params=_params("arbitrary", "arbitrary"),
        name="fox_keys",
    )(proj, gp)


def _fox_kernel(q_ref, k_ref, v_ref, c_ref, ms_ref, o_ref, s_scr, m_scr, l_scr, acc_scr,
                *, tq, tk, unroll):
    ndiag = tq // tk
    static_slots = unroll % 2 == 0 and ndiag % 2 == 0
    qi = pl.program_id(2)
    nfull = qi * ndiag
    q0 = pl.multiple_of(qi * tq, tq)
    def augmented_queries(start):
        q_t = (q_ref[0, pl.ds(start, tq), :].astype(F32) * (HEAD_DIM ** -0.5 * LOG2E)).T.astype(BF16)
        row = lax.broadcasted_iota(jnp.int32, (HEAD_DIM, tq), 0)
        return jnp.concatenate([q_t, jnp.where(row < 3, 1.0, 0.0).astype(BF16)], axis=0)

    q_aug = augmented_queries(q0)
    c_first = c_ref[0, :, pl.ds(q0, 128)][:, 0:1]

    m_scr[...] = jnp.full_like(m_scr, -jnp.inf)
    l_scr[...] = jnp.zeros_like(l_scr)
    acc_scr[...] = jnp.zeros_like(acc_scr)

    def scores(j, slot, c0):
        k0 = pl.multiple_of(j * tk, tk)
        s_scr[slot, :, c0:] = _dot(k_ref[0, 0, pl.ds(k0, tk), :], q_aug[:, c0:])

    def softmax(j, slot, diag):
        k0 = pl.multiple_of(j * tk, tk)
        c0 = 0 if diag is None else diag * tk
        delta = (c_first - c_ref[0, :, pl.ds(k0, 128)][:, 0:1]) * LOG2E
        t = s_scr[slot, :, c0:]
        if diag is not None:
            key = lax.broadcasted_iota(jnp.int32, t.shape, 0)
            qry = lax.broadcasted_iota(jnp.int32, t.shape, 1)
            t = jnp.where(qry >= key, t, -jnp.inf)
        m_prev = m_scr[:, c0:]
        m_new = jnp.maximum(m_prev, jnp.max(t, axis=0, keepdims=True) + delta)
        p = jnp.exp2(t - (m_new - delta)).astype(BF16)
        alpha = jnp.exp2(m_prev - m_new)
        half = p[:tk // 2] + p[tk // 2:]
        quarter = half[:tk // 4] + half[tk // 4:]
        l_scr[:, c0:] = alpha * l_scr[:, c0:] + jnp.sum(quarter.astype(F32), axis=0, keepdims=True)
        m_scr[:, c0:] = m_new
        acc_scr[:, c0:] = alpha * acc_scr[:, c0:] + _dot_tn(v_ref[0, pl.ds(k0, tk), :], p)

    def step(j, slot, diag, last):
        if not last:
            scores(j + 1, 1 - slot, 0 if diag is None else (diag + 1) * tk)
        softmax(j, slot, diag)

    @pl.when(qi == 0)
    def _():
        scores(0, 0, 0)

    def body(i, carry):
        for u in range(unroll):
            j = i * unroll + u
            step(j, u % 2 if static_slots else j & 1, None, False)
        return carry

    lax.fori_loop(0, nfull // unroll, body, 0)
    for d in range(ndiag):
        j = nfull + d
        slot = d % 2 if ndiag % 2 == 0 else j & 1
        step(j, slot, d, d == ndiag - 1)

    def finalize():
        out = (acc_scr[...] / l_scr[...]).T
        o_ref[0] = (_head_norm(out) * ms_ref[...]).astype(BF16)

    @pl.when(qi + 1 < pl.num_programs(2))
    def _():
        s_scr[0] = _dot(k_ref[0, 0, 0:tk, :], augmented_queries(pl.multiple_of(q0 + tq, tq)))
        finalize()

    @pl.when(qi + 1 == pl.num_programs(2))
    def _():
        finalize()


def _fox(proj, k_aug, c_rows, merge_scale, heads, col0, ms_col0, tq, tk, unroll):
    b, s, _ = proj.shape
    assert tq % tk == 0 and (tq // tk) % unroll == 0
    return pl.pallas_call(
        functools.partial(_fox_kernel, tq=tq, tk=tk, unroll=unroll),
        grid=(b, heads, s // tq),
        in_specs=[pl.BlockSpec((1, s, HEAD_DIM), lambda i, h, q: (i, 0, col0 + h)),
                  pl.BlockSpec((1, 1, s, 2 * HEAD_DIM), lambda i, h, q: (i, h, 0, 0)),
                  pl.BlockSpec((1, s, HEAD_DIM), lambda i, h, q: (i, 0, col0 + 2 * heads + h)),
                  pl.BlockSpec((1, 1, s), lambda i, h, q: (i * heads + h, 0, 0)),
                  pl.BlockSpec((1, HEAD_DIM), lambda i, h, q: (0, ms_col0 + h))],
        out_specs=pl.BlockSpec((1, tq, HEAD_DIM), lambda i, h, q: (i, q, h)),
        out_shape=jax.ShapeDtypeStruct((b, s, heads * HEAD_DIM), BF16),
        scratch_shapes=[pltpu.VMEM((2, tk, tq), F32),
                        pltpu.VMEM((1, tq), F32),
                        pltpu.VMEM((1, tq), F32),
                        pltpu.VMEM((HEAD_DIM, tq), F32)],
        compiler_params=_params("arbitrary", "arbitrary", "arbitrary"),
        name="fox_attention",
    )(proj, k_aug, proj, c_rows, merge_scale)


def _mlstm_kernel(q_ref, k_ref, v_ref, og_ref, w_ref, b_ref, gp_ref, ir_ref, gr_ref, ms_ref, o_ref,
                  c_scr, n_scr, m_scr, g0_scr, xq_scr, xk_scr, *, heads, i_lane0, f_lane0):
    n = q_ref.shape[1]
    w = heads * HEAD_DIM

    @pl.when(pl.program_id(1) == 0)
    def _():
        c_scr[...] = jnp.zeros_like(c_scr)
        n_scr[...] = jnp.zeros_like(n_scr)
        m_scr[...] = jnp.zeros_like(m_scr)
        g0_scr[...] = jnp.zeros_like(g0_scr)
        xq_scr[0:CONV_HALO, :] = jnp.zeros((CONV_HALO, w), F32)
        xk_scr[0:CONV_HALO, :] = jnp.zeros((CONV_HALO, w), F32)

    def conv_silu(x_ref, x_scr, cols):
        x_scr[CONV_HALO:CONV_HALO + n, :] = x_ref[0].astype(F32)
        y = b_ref[:, cols] + jnp.zeros((n, w), F32)
        for j in range(CONV_WIDTH):
            shift = CONV_WIDTH - 1 - j
            y = y + w_ref[j:j + 1, cols] * x_scr[CONV_HALO - shift:CONV_HALO - shift + n, :]
        x_scr[0:CONV_HALO, :] = x_scr[n:n + CONV_HALO, :]
        return y * _sigmoid(y)

    q_all = conv_silu(q_ref, xq_scr, slice(0, w))
    k_all = conv_silu(k_ref, xk_scr, slice(w, 2 * w)) * (HEAD_DIM ** -0.5)
    gp = gp_ref[0]
    row = lax.broadcasted_iota(jnp.int32, (n, n), 0)
    col = lax.broadcasted_iota(jnp.int32, (n, n), 1)
    causal = row >= col

    for h in range(heads):
        cs = slice(h * HEAD_DIM, (h + 1) * HEAD_DIM)
        q = q_all[:, cs]
        k = k_all[:, cs]
        v = v_ref[0, :, cs]
        ic = gp[:, i_lane0 + h:i_lane0 + h + 1]
        gc = gp[:, f_lane0 + h:f_lane0 + h + 1]
        ir = ir_ref[0, h:h + 1, :]
        gr = gr_ref[0, h:h + 1, :]

        g0 = g0_scr[h][:, 0:1]
        m_prev = m_scr[h][:, 0:1]
        b_col = gc - g0
        dmat = jnp.where(causal, gc + (ir - gr), -jnp.inf)
        inter_log = b_col + m_prev
        m_q = jnp.maximum(inter_log, jnp.max(dmat, axis=-1, keepdims=True))
        w_intra = jnp.exp(dmat - m_q)
        w_inter = jnp.exp(inter_log - m_q)

        qb = q.astype(BF16)
        s = _dot_nt(qb, k.astype(BF16)) * w_intra
        cst = c_scr[h]
        nst = n_scr[h]
        num = _dot(s.astype(BF16), v) + w_inter * _dot(qb, cst.astype(BF16))
        den = jnp.sum(s, axis=-1, keepdims=True) + w_inter * jnp.sum(q * nst, axis=-1, keepdims=True)
        hid = num / jnp.maximum(jnp.abs(den), jnp.exp(-m_q))

        b_last = b_col[n - 1:n, :]
        k_log = b_last - b_col + ic
        m_new = jnp.maximum(b_last + m_prev, jnp.max(k_log, axis=0, keepdims=True))
        wk = jnp.exp(k_log - m_new)
        carry_scale = jnp.exp(b_last + m_prev - m_new)
        kw = k * wk
        c_scr[h] = carry_scale * cst + _dot_tn(kw.astype(BF16), v)
        n_scr[h] = carry_scale * nst + jnp.sum(kw, axis=0, keepdims=True)
        m_scr[h] = jnp.broadcast_to(m_new, (1, HEAD_DIM))
        g0_scr[h] = jnp.broadcast_to(gc[n - 1:n, :], (1, HEAD_DIM))

        og = og_ref[0, :, cs].astype(F32)
        o_ref[0, :, cs] = (_head_norm(hid) * _sigmoid(og) * ms_ref[:, cs]).astype(BF16)


def _mlstm(proj, conv_w, conv_b, gp, i_rows, g_rows, merge_scale, heads, col0, ms_col0, i_lane0, f_lane0, chunk):
    b, s, _ = proj.shape
    w = heads * HEAD_DIM
    assert (col0 * HEAD_DIM) % w == 0 and (ms_col0 * HEAD_DIM) % w == 0
    blk0 = col0 * HEAD_DIM // w
    grp = lambda base: pl.BlockSpec((1, chunk, w), lambda i, c: (i, c, blk0 + base))
    row_spec = pl.BlockSpec((1, heads, chunk), lambda i, c: (i, 0, c))
    state = lambda r, c: pltpu.VMEM((heads, r, c), F32)
    return pl.pallas_call(
        functools.partial(_mlstm_kernel, heads=heads, i_lane0=i_lane0, f_lane0=f_lane0),
        grid=(b, s // chunk),
        in_specs=[grp(0), grp(1), grp(2), grp(3),
                  pl.BlockSpec(conv_w.shape, lambda i, c: (0, 0)),
                  pl.BlockSpec(conv_b.shape, lambda i, c: (0, 0)),
                  pl.BlockSpec((1, chunk, GATE_LANES), lambda i, c: (i, c, 0)),
                  row_spec, row_spec,
                  pl.BlockSpec((1, w), lambda i, c: (0, ms_col0 * HEAD_DIM // w))],
        out_specs=pl.BlockSpec((1, chunk, w), lambda i, c: (i, c, 0)),
        out_shape=jax.ShapeDtypeStruct((b, s, w), BF16),
        scratch_shapes=[state(HEAD_DIM, HEAD_DIM), state(1, HEAD_DIM), state(1, HEAD_DIM), state(1, HEAD_DIM),
                        pltpu.VMEM((chunk + CONV_HALO, w), F32),
                        pltpu.VMEM((chunk + CONV_HALO, w), F32)],
        compiler_params=_params("arbitrary", "arbitrary"),
        name="mlstm",
    )(proj, proj, proj, proj, conv_w, conv_b, gp, i_rows, g_rows, merge_scale)


def _outproj_kernel(yr_ref, yf_ref, ym_ref, w_ref, x_ref, ga_ref, o_ref):
    r = yr_ref.shape[1]
    f = yf_ref.shape[1]
    y = _dot(yr_ref[...], w_ref[0:r, :]) + _dot(yf_ref[...], w_ref[r:r + f, :])
    y = y + _dot(ym_ref[...], w_ref[r + f:, :])
    o_ref[...] = x_ref[...] + ga_ref[0] * y


def _outproj(y_ret, y_fox, y_m, w_out, x2, g_a, seq):
    m, d = x2.shape
    tm = min(512, seq)
    bpr = seq // tm
    rows = lambda w: pl.BlockSpec((tm, w), lambda i: (i, 0))
    return pl.pallas_call(
        _outproj_kernel,
        grid=(m // tm,),
        in_specs=[rows(y_ret.shape[1]), rows(y_fox.shape[1]), rows(y_m.shape[1]),
                  pl.BlockSpec(w_out.shape, lambda i: (0, 0)),
                  rows(d),
                  pl.BlockSpec((1, 1, d), lambda i: (i // bpr, 0, 0))],
        out_specs=rows(d),
        out_shape=jax.ShapeDtypeStruct((m, d), F32),
        compiler_params=_params("arbitrary"),
        name="outproj",
    )(y_ret, y_fox, y_m, w_out, x2, g_a)


def _mlp_kernel(x_ref, xn_ref, nw_ref, sc_ref, sh_ref, scn_ref, shn_ref, gm_ref, w1_hbm, w2_hbm, fw_ref, o_ref,
                w1_buf, w2_buf, sem, h0_scr, h1_scr, acc_scr, *, final_norm, nf, rows):
    i = pl.program_id(0)
    n = pl.num_programs(0)

    def tile_copies(f, slot):
        return (pltpu.make_async_copy(w1_hbm.at[f], w1_buf.at[slot], sem.at[0, slot]),
                pltpu.make_async_copy(w2_hbm.at[f], w2_buf.at[slot], sem.at[1, slot]))

    @pl.when(i == 0)
    def _():
        for cp in tile_copies(0, 0):
            cp.start()
        _modulated_norm_rows(x_ref, nw_ref, sc_ref, sh_ref, h0_scr)

    acc_scr[...] = jnp.zeros_like(acc_scr)

    def sweep(h_cur, h_next):
        def body(f, carry):
            slot = f % 2

            @pl.when((f + 1 < nf) | (i + 1 < n))
            def _():
                for cp in tile_copies((f + 1) % nf, 1 - slot):
                    cp.start()

            for cp in tile_copies(f, slot):
                cp.wait()
            _modulated_norm_slice(xn_ref, pl.multiple_of(f * rows, rows), rows, nw_ref, scn_ref, shn_ref, h_next)
            a = jnp.maximum(_dot(h_cur[...], w1_buf[slot]), 0.0)
            acc_scr[...] += _dot((a * a).astype(BF16), w2_buf[slot])
            return carry

        lax.fori_loop(0, nf, body, 0)

    @pl.when(i % 2 == 0)
    def _():
        sweep(h0_scr, h1_scr)

    @pl.when(i % 2 == 1)
    def _():
        sweep(h1_scr, h0_scr)

    y = x_ref[...] + gm_ref[0] * acc_scr[...]
    if final_norm:
        ms = jnp.mean(y * y, axis=-1, keepdims=True)
        y = y * lax.rsqrt(ms + NORM_EPS) * fw_ref[...]
    o_ref[...] = y


def _mlp(x2, nw, sc, sh, g_m, w1, w2, final_w, seq, final_norm):
    m, d = x2.shape
    dff = w1.shape[1]
    tm = min(512, seq)
    tf = min(1024, dff)
    nf = dff // tf
    assert nf % 2 == 0 and tm % (16 * nf) == 0
    bpr = seq // tm
    last = m // tm - 1
    nxt = lambda i: jnp.minimum(i + 1, last)
    mod = pl.BlockSpec((1, 1, d), lambda i: (i // bpr, 0, 0))
    mod_next = pl.BlockSpec((1, 1, d), lambda i: (nxt(i) // bpr, 0, 0))
    vec = pl.BlockSpec((1, d), lambda i: (0, 0))
    hbm = pl.BlockSpec(memory_space=pl.ANY)
    w1_tiles = w1.reshape(d, nf, tf).transpose(1, 0, 2)
    w2_tiles = w2.reshape(nf, tf, d)
    return pl.pallas_call(
        functools.partial(_mlp_kernel, final_norm=final_norm, nf=nf, rows=tm // nf),
        grid=(m // tm,),
        in_specs=[pl.BlockSpec((tm, d), lambda i: (i, 0)),
                  pl.BlockSpec((tm, d), lambda i: (nxt(i), 0)),
                  vec, mod, mod, mod_next, mod_next, mod, hbm, hbm, vec],
        out_specs=pl.BlockSpec((tm, d), lambda i: (i, 0)),
        out_shape=jax.ShapeDtypeStruct((m, d), F32),
        scratch_shapes=[pltpu.VMEM((2, d, tf), BF16),
                        pltpu.VMEM((2, tf, d), BF16),
                        pltpu.SemaphoreType.DMA((2, 2)),
                        pltpu.VMEM((tm, d), BF16),
                        pltpu.VMEM((tm, d), BF16),
                        pltpu.VMEM((tm, d), F32)],
        compiler_params=_params("arbitrary"),
        name="mlp",
    )(x2, x2, nw, sc, sh, sc, sh, g_m, w1_tiles, w2_tiles, final_w)


def kernel(x, c, positions, ada_w, ada_b, norm_mix_w, norm_mlp_w, w_in, conv_w, conv_b, fox_f_bias,
           mlstm_i_bias, mlstm_f_bias, merge_scale, w_out, w_ff1, w_ff2, final_norm_w):
    b, s, d = x.shape
    depth = ada_w.shape[0]
    ret_w, fox_w, ml_w = d // 4, d // 2, d // 4
    ret_h, fox_h, ml_h = ret_w // HEAD_DIM, fox_w // HEAD_DIM, ml_w // HEAD_DIM
    assert s % 128 == 0 and d % (4 * HEAD_DIM) == 0
    assert fox_h + 2 * ml_h <= GATE_LANES

    sizes = [ret_w] * 4 + [fox_w] * 3 + [fox_h] + [ml_w] * 4 + [ml_h, ml_h]
    offs = [0]
    for sz in sizes:
        offs.append(offs[-1] + sz)
    n_gate = fox_h + 2 * ml_h
    ret_col0 = 0
    fox_col0 = (4 * ret_w) // HEAD_DIM
    ml_col0 = (4 * ret_w + 3 * fox_w) // HEAD_DIM
    i_lane0, f_lane0 = fox_h, fox_h + ml_h

    ret_chunk = min(RET_CHUNK, s)
    ml_chunk = min(MLSTM_CHUNK, s)
    tq = min(FOX_TQ, s)
    tk = min(FOX_TK, tq)

    mod = _ada_mod(c, ada_w, ada_b)
    cos, sin = _rope_tables(positions)
    cum_mask = jnp.zeros((1, GATE_LANES), F32).at[0, :fox_h].set(1.0).at[0, f_lane0:f_lane0 + ml_h].set(1.0)

    x2 = x.reshape(b * s, d)
    for layer in range(depth):
        sh_a, sc_a, g_a, sh_m, sc_m, g_m = [t.reshape(b, 1, d) for t in jnp.split(mod[layer], N_MOD, axis=-1)]
        wl = w_in[layer]
        w_main = jnp.concatenate([wl[:, offs[0]:offs[7]], wl[:, offs[8]:offs[12]]], axis=1).astype(BF16)
        w_gate = jnp.concatenate([wl[:, offs[7]:offs[8]], wl[:, offs[12]:offs[14]],
                                  jnp.zeros((d, GATE_LANES - n_gate), F32)], axis=1).astype(BF16)
        gate_bias = jnp.pad(jnp.concatenate([fox_f_bias[layer], mlstm_i_bias[layer], mlstm_f_bias[layer]]),
                            (0, GATE_LANES - n_gate)).reshape(1, GATE_LANES)
        ms = merge_scale[layer].reshape(1, d)

        proj, gates = _inproj(x2, norm_mix_w[layer].reshape(1, d), sc_a, sh_a, w_main, w_gate, s)
        proj = proj.reshape(b, s, -1)
        gp = _gates(gates.reshape(b, s, GATE_LANES), gate_bias, cum_mask)
        gp_t = jnp.swapaxes(gp[:, :, :n_gate], 1, 2)
        c_rows = gp_t[:, :fox_h].reshape(b * fox_h, 1, s)
        i_rows = gp_t[:, i_lane0:i_lane0 + ml_h]
        g_rows = gp_t[:, f_lane0:f_lane0 + ml_h]

        y_ret = _retention(proj, cos, sin, ms, ret_h, ret_col0, ret_chunk)
        k_aug = _fox_keys(proj, gp, fox_h, fox_col0 + fox_h, tk)
        y_fox = _fox(proj, k_aug, c_rows, ms, fox_h, fox_col0, ret_h, tq, tk, FOX_UNROLL if tq // tk % FOX_UNROLL == 0 else 1)
        y_m = _mlstm(proj, conv_w[layer], conv_b[layer].reshape(1, -1), gp, i_rows, g_rows, ms,
                     ml_h, ml_col0, ret_h + fox_h, i_lane0, f_lane0, ml_chunk)

        x2 = _outproj(y_ret.reshape(b * s, -1), y_fox.reshape(b * s, -1), y_m.reshape(b * s, -1),
                      w_out[layer].astype(BF16), x2, g_a, s)
        x2 = _mlp(x2, norm_mlp_w[layer].reshape(1, d), sc_m, sh_m, g_m,
                  w_ff1[layer].astype(BF16), w_ff2[layer].astype(BF16),
                  final_norm_w.reshape(1, d), s, layer == depth - 1)
    return x2.reshape(b, s, d)
```

```python
import functools
import math

import jax
import jax.numpy as jnp
from jax import lax
from jax.experimental import pallas as pl
from jax.experimental.pallas import tpu as pltpu

HEAD_DIM = 128
CONV_WIDTH = 4
ROPE_BASE = 10000.0
NORM_EPS = 1e-6
N_MOD = 6
GATE_LANES = 128
LOG2E = math.log2(math.e)
FOX_TQ, FOX_TK, FOX_UNROLL = 1024, 512, 2
RET_CHUNK, MLSTM_CHUNK = 256, 512
CONV_HALO = 8

F32 = jnp.float32
BF16 = jnp.bfloat16

VMEM_LIMIT_BYTES = 56 * 1024 * 1024


def _params(*sem, flags=None):
    return pltpu.CompilerParams(dimension_semantics=sem, vmem_limit_bytes=VMEM_LIMIT_BYTES, flags=flags)


def _dot(a, b):
    return jnp.dot(a, b, preferred_element_type=F32)


def _dot_nt(a, b):
    return lax.dot_general(a, b, (((1,), (1,)), ((), ())), preferred_element_type=F32)


def _dot_tn(a, b):
    return lax.dot_general(a, b, (((0,), (0,)), ((), ())), preferred_element_type=F32)


def _sigmoid(x):
    return 1.0 / (1.0 + jnp.exp(-x))


def _log_sigmoid(x):
    return jnp.minimum(x, 0.0) - jnp.log1p(jnp.exp(-jnp.abs(x)))


def _head_norm(y):
    mu = jnp.mean(y, axis=-1, keepdims=True)
    yc = y - mu
    var = jnp.mean(yc * yc, axis=-1, keepdims=True)
    return yc * lax.rsqrt(var + NORM_EPS)


def _ada_kernel(ct_ref, w_ref, b_ref, o_ref, *, kc):
    d, nb = ct_ref.shape
    tn = w_ref.shape[2]

    def body(i, accs):
        r0 = pl.multiple_of(i * kc, kc)
        ck = ct_ref[pl.ds(r0, kc), :]
        ck = ck * _sigmoid(ck)
        wk = w_ref[0, pl.ds(r0, kc), :]
        return tuple(acc + jnp.sum(wk * ck[:, b:b + 1], axis=0, keepdims=True)
                     for b, acc in enumerate(accs))

    accs = lax.fori_loop(0, d // kc, body, tuple(jnp.zeros((1, tn), F32) for _ in range(nb)))
    o_ref[0] = jnp.concatenate(accs, axis=0) + b_ref[0]


def _ada_mod(c, ada_w, ada_b):
    depth, d, n = ada_w.shape
    nb = c.shape[0]
    tn = min(1024, n)
    kc = min(256, d)
    return pl.pallas_call(
        functools.partial(_ada_kernel, kc=kc),
        grid=(depth, n // tn),
        in_specs=[pl.BlockSpec((d, nb), lambda l, j: (0, 0)),
                  pl.BlockSpec((1, d, tn), lambda l, j: (l, 0, j)),
                  pl.BlockSpec((1, 1, tn), lambda l, j: (l, 0, j))],
        out_specs=pl.BlockSpec((1, nb, tn), lambda l, j: (l, 0, j)),
        out_shape=jax.ShapeDtypeStruct((depth, nb, n), F32),
        compiler_params=_params("arbitrary", "arbitrary"),
        name="ada_mod",
    )(c.T, ada_w, ada_b.reshape(depth, 1, n))


def _rope_kernel(pos_ref, invf_ref, sign_ref, cos_ref, sin_ref):
    ang = pos_ref[0] * invf_ref[...]
    cos_ref[0] = jnp.cos(ang)
    sin_ref[0] = jnp.sin(ang) * sign_ref[...]


def _rope_tables(positions):
    b, s = positions.shape
    ts = min(2048, s)
    half = HEAD_DIM // 2
    inv_freq = ROPE_BASE ** (-jnp.arange(0, HEAD_DIM, 2, dtype=F32) / HEAD_DIM)
    invf = jnp.concatenate([inv_freq, inv_freq]).reshape(1, HEAD_DIM)
    sign = jnp.concatenate([-jnp.ones((half,), F32), jnp.ones((half,), F32)]).reshape(1, HEAD_DIM)
    pos = positions.astype(F32).reshape(b, s, 1)
    out = jax.ShapeDtypeStruct((b, s, HEAD_DIM), F32)
    return pl.pallas_call(
        _rope_kernel,
        grid=(b, s // ts),
        in_specs=[pl.BlockSpec((1, ts, 1), lambda i, j: (i, j, 0)),
                  pl.BlockSpec((1, HEAD_DIM), lambda i, j: (0, 0)),
                  pl.BlockSpec((1, HEAD_DIM), lambda i, j: (0, 0))],
        out_specs=[pl.BlockSpec((1, ts, HEAD_DIM), lambda i, j: (i, j, 0))] * 2,
        out_shape=[out, out],
        compiler_params=_params("arbitrary", "arbitrary"),
        name="rope_tables",
    )(pos, invf, sign)


def _modulated_norm_rows(x_ref, nw_ref, sc_ref, sh_ref, h_scr):
    x = x_ref[...]
    inv = lax.rsqrt(jnp.mean(x * x, axis=-1, keepdims=True) + NORM_EPS)
    h_scr[...] = ((x_ref[...] * inv * nw_ref[...]) * (1.0 + sc_ref[0]) + sh_ref[0]).astype(BF16)


def _inproj_kernel(x_ref, nw_ref, sc_ref, sh_ref, w_ref, wg_ref, o_ref, g_ref, h_scr):
    @pl.when(pl.program_id(1) == 0)
    def _():
        _modulated_norm_rows(x_ref, nw_ref, sc_ref, sh_ref, h_scr)
        g_ref[...] = _dot(h_scr[...], wg_ref[...])

    o_ref[...] = _dot(h_scr[...], w_ref[...]).astype(BF16)


def _inproj(x2, nw, sc, sh, w_main, w_gate, seq):
    m, d = x2.shape
    nm = w_main.shape[1]
    tm = min(1024, seq)
    tn = next(t for t in (1792, 1024, 512, 256, 128) if nm % t == 0)
    bpr = seq // tm
    return pl.pallas_call(
        _inproj_kernel,
        grid=(m // tm, nm // tn),
        in_specs=[pl.BlockSpec((tm, d), lambda i, j: (i, 0)),
                  pl.BlockSpec((1, d), lambda i, j: (0, 0)),
                  pl.BlockSpec((1, 1, d), lambda i, j: (i // bpr, 0, 0)),
                  pl.BlockSpec((1, 1, d), lambda i, j: (i // bpr, 0, 0)),
                  pl.BlockSpec((d, tn), lambda i, j: (0, j)),
                  pl.BlockSpec((d, GATE_LANES), lambda i, j: (0, 0))],
        out_specs=[pl.BlockSpec((tm, tn), lambda i, j: (i, j)),
                   pl.BlockSpec((tm, GATE_LANES), lambda i, j: (i, 0))],
        out_shape=[jax.ShapeDtypeStruct((m, nm), BF16),
                   jax.ShapeDtypeStruct((m, GATE_LANES), F32)],
        scratch_shapes=[pltpu.VMEM((tm, d), BF16)],
        compiler_params=_params("arbitrary", "arbitrary"),
        name="inproj",
    )(x2, nw, sc, sh, w_main, w_gate)


def _gates_kernel(g_ref, bias_ref, cum_ref, o_ref, carry):
    @pl.when(pl.program_id(1) == 0)
    def _():
        carry[...] = jnp.zeros_like(carry)

    g = g_ref[0] + bias_ref[...]
    ls = _log_sigmoid(g)
    ts = ls.shape[0]
    row = lax.broadcasted_iota(jnp.int32, (ts, ts), 0)
    col = lax.broadcasted_iota(jnp.int32, (ts, ts), 1)
    tri = jnp.where(row >= col, 1.0, 0.0).astype(BF16)
    hi = ls.astype(BF16)
    r1 = ls - hi.astype(F32)
    mid = r1.astype(BF16)
    lo = (r1 - mid.astype(F32)).astype(BF16)
    csum = (_dot(tri, hi) + _dot(tri, mid)) + _dot(tri, lo) + carry[...]
    carry[...] = csum[ts - 1:ts, :]
    o_ref[0] = jnp.where(cum_ref[...] > 0.5, csum, g)


def _gates(g3, bias, cum_mask):
    b, s, n = g3.shape
    ts = min(512, s)
    return pl.pallas_call(
        _gates_kernel,
        grid=(b, s // ts),
        in_specs=[pl.BlockSpec((1, ts, n), lambda i, j: (i, j, 0)),
                  pl.BlockSpec((1, n), lambda i, j: (0, 0)),
                  pl.BlockSpec((1, n), lambda i, j: (0, 0))],
        out_specs=pl.BlockSpec((1, ts, n), lambda i, j: (i, j, 0)),
        out_shape=jax.ShapeDtypeStruct((b, s, n), F32),
        scratch_shapes=[pltpu.VMEM((1, n), F32)],
        compiler_params=_params("arbitrary", "arbitrary"),
        name="gates",
    )(g3, bias, cum_mask)


def _retention_kernel(q_ref, k_ref, v_ref, g_ref, cos_ref, sin_ref, ms_ref, o_ref, state, decay, *, heads):
    n = cos_ref.shape[1]
    log_gamma = [math.log(1.0 - 2.0 ** (-5.0 - h)) for h in range(heads)]

    @pl.when(pl.program_id(1) == 0)
    def _():
        state[...] = jnp.zeros_like(state)
        row = lax.broadcasted_iota(jnp.int32, (n, n), 0)
        col = lax.broadcasted_iota(jnp.int32, (n, n), 1)
        rel = (row - col).astype(F32)
        for h in range(heads):
            decay[h] = jnp.where(rel >= 0, jnp.exp(jnp.maximum(rel, 0.0) * log_gamma[h]), 0.0)

    cos = cos_ref[0]
    sin = sin_ref[0]
    half = HEAD_DIM // 2
    idx = lax.broadcasted_iota(jnp.int32, (n, 1), 0).astype(F32)

    def rot(t):
        return t * cos + pltpu.roll(t, half, 1) * sin

    for h in range(heads):
        lg = log_gamma[h]
        cs = slice(h * HEAD_DIM, (h + 1) * HEAD_DIM)
        q = rot(q_ref[0, :, cs].astype(F32))
        k = rot(k_ref[0, :, cs].astype(F32)) * (HEAD_DIM ** -0.5)
        v = v_ref[0, :, cs]
        q_decay = jnp.exp((idx + 1.0) * lg)
        k_decay = jnp.exp((n - 1.0 - idx) * lg)
        chunk_decay = math.exp(n * lg)

        scores = _dot_nt(q.astype(BF16), k.astype(BF16)) * decay[h]
        intra = _dot(scores.astype(BF16), v)
        st = state[h]
        inter = _dot((q * q_decay).astype(BF16), st.astype(BF16))
        state[h] = st * chunk_decay + _dot_tn((k * k_decay).astype(BF16), v)

        y = _head_norm(intra + inter)
        gate = g_ref[0, :, cs].astype(F32)
        o_ref[0, :, cs] = (y * (gate * _sigmoid(gate)) * ms_ref[:, cs]).astype(BF16)


def _retention(proj, cos, sin, merge_scale, heads, col0, chunk):
    b, s, _ = proj.shape
    w = heads * HEAD_DIM
    assert (col0 * HEAD_DIM) % w == 0
    blk0 = col0 * HEAD_DIM // w
    grp = lambda base: pl.BlockSpec((1, chunk, w), lambda i, c: (i, c, blk0 + base))
    tok = pl.BlockSpec((1, chunk, HEAD_DIM), lambda i, c: (i, c, 0))
    return pl.pallas_call(
        functools.partial(_retention_kernel, heads=heads),
        grid=(b, s // chunk),
        in_specs=[grp(0), grp(1), grp(2), grp(3), tok, tok,
                  pl.BlockSpec((1, w), lambda i, c: (0, 0))],
        out_specs=pl.BlockSpec((1, chunk, w), lambda i, c: (i, c, 0)),
        out_shape=jax.ShapeDtypeStruct((b, s, w), BF16),
        scratch_shapes=[pltpu.VMEM((heads, HEAD_DIM, HEAD_DIM), F32),
                        pltpu.VMEM((heads, chunk, chunk), F32)],
        compiler_params=_params("arbitrary", "arbitrary"),
        name="retention",
    )(proj, proj, proj, proj, cos, sin, merge_scale)


def _fox_keys_kernel(k_ref, gp_ref, o_ref, *, tk, heads):
    ts = k_ref.shape[1]
    c = gp_ref[0]
    parts = []
    for i in range(ts // tk):
        blk = c[i * tk:(i + 1) * tk]
        parts.append((blk[0:1, :] - blk) * LOG2E)
    x = jnp.concatenate(parts, axis=0) if len(parts) > 1 else parts[0]
    hi = x.astype(BF16)
    r1 = x - hi.astype(F32)
    mid = r1.astype(BF16)
    lo = (r1 - mid.astype(F32)).astype(BF16)
    row = lax.broadcasted_iota(jnp.int32, (GATE_LANES, HEAD_DIM), 0)
    col = lax.broadcasted_iota(jnp.int32, (GATE_LANES, HEAD_DIM), 1)
    for h in range(heads):
        sel = lambda piece: jnp.where((row == h) & (col == piece), 1.0, 0.0).astype(BF16)
        bias = _dot(hi, sel(0)) + _dot(mid, sel(1)) + _dot(lo, sel(2))
        o_ref[0, h] = jnp.concatenate([k_ref[0, :, h * HEAD_DIM:(h + 1) * HEAD_DIM], bias.astype(BF16)], axis=1)


def _fox_keys(proj, gp, heads, k_col0, tk):
    b, s, _ = proj.shape
    ts = min(1024, s)
    w = heads * HEAD_DIM
    assert (k_col0 * HEAD_DIM) % w == 0 and ts % tk == 0
    return pl.pallas_call(
        functools.partial(_fox_keys_kernel, tk=tk, heads=heads),
        grid=(b, s // ts),
        in_specs=[pl.BlockSpec((1, ts, w), lambda i, j: (i, j, k_col0 * HEAD_DIM // w)),
                  pl.BlockSpec((1, ts, GATE_LANES), lambda i, j: (i, j, 0))],
        out_specs=pl.BlockSpec((1, heads, ts, 2 * HEAD_DIM), lambda i, j: (i, 0, j, 0)),
        out_shape=jax.ShapeDtypeStruct((b, heads, s, 2 * HEAD_DIM), BF16),
        compiler_params=_params("arbitrary", "arbitrary"),
        name="fox_keys",
    )(proj, gp)


def _fox_kernel(q_ref, k_ref, v_ref, c_ref, ms_ref, o_ref, s_scr, m_scr, l_scr, acc_scr,
                *, tq, tk, unroll):
    ndiag = tq // tk
    static_slots = unroll % 2 == 0 and ndiag % 2 == 0
    qi = pl.program_id(2)
    nfull = qi * ndiag
    q0 = pl.multiple_of(qi * tq, tq)
    def augmented_queries(start):
        q_t = (q_ref[0, pl.ds(start, tq), :].astype(F32) * (HEAD_DIM ** -0.5 * LOG2E)).T.astype(BF16)
        row = lax.broadcasted_iota(jnp.int32, (HEAD_DIM, tq), 0)
        return jnp.concatenate([q_t, jnp.where(row < 3, 1.0, 0.0).astype(BF16)], axis=0)

    q_aug = augmented_queries(q0)
    c_first = c_ref[0, :, pl.ds(q0, 128)][:, 0:1]

    m_scr[...] = jnp.full_like(m_scr, -jnp.inf)
    l_scr[...] = jnp.zeros_like(l_scr)
    acc_scr[...] = jnp.zeros_like(acc_scr)

    def scores(j, slot, c0):
        k0 = pl.multiple_of(j * tk, tk)
        s_scr[slot, :, c0:] = _dot(k_ref[0, 0, pl.ds(k0, tk), :], q_aug[:, c0:])

    def softmax(j, slot, diag):
        k0 = pl.multiple_of(j * tk, tk)
        c0 = 0 if diag is None else diag * tk
        delta = (c_first - c_ref[0, :, pl.ds(k0, 128)][:, 0:1]) * LOG2E
        t = s_scr[slot, :, c0:]
        if diag is not None:
            key = lax.broadcasted_iota(jnp.int32, t.shape, 0)
            qry = lax.broadcasted_iota(jnp.int32, t.shape, 1)
            t = jnp.where(qry >= key, t, -jnp.inf)
        m_prev = m_scr[:, c0:]
        m_new = jnp.maximum(m_prev, jnp.max(t, axis=0, keepdims=True) + delta)
        p = jnp.exp2(t - (m_new - delta)).astype(BF16)
        alpha = jnp.exp2(m_prev - m_new)
        half = p[:tk // 2] + p[tk // 2:]
        quarter = half[:tk // 4] + half[tk // 4:]
        l_scr[:, c0:] = alpha * l_scr[:, c0:] + jnp.sum(quarter.astype(F32), axis=0, keepdims=True)
        m_scr[:, c0:] = m_new
        acc_scr[:, c0:] = alpha * acc_scr[:, c0:] + _dot_tn(v_ref[0, pl.ds(k0, tk), :], p)

    def step(j, slot, diag, last):
        if not last:
            scores(j + 1, 1 - slot, 0 if diag is None else (diag + 1) * tk)
        softmax(j, slot, diag)

    @pl.when(qi == 0)
    def _():
        scores(0, 0, 0)

    def body(i, carry):
        for u in range(unroll):
            j = i * unroll + u
            step(j, u % 2 if static_slots else j & 1, None, False)
        return carry

    lax.fori_loop(0, nfull // unroll, body, 0)
    for d in range(ndiag):
        j = nfull + d
        slot = d % 2 if ndiag % 2 == 0 else j & 1
        step(j, slot, d, d == ndiag - 1)

    def finalize():
        out = (acc_scr[...] / l_scr[...]).T
        o_ref[0] = (_head_norm(out) * ms_ref[...]).astype(BF16)

    @pl.when(qi + 1 < pl.num_programs(2))
    def _():
        s_scr[0] = _dot(k_ref[0, 0, 0:tk, :], augmented_queries(pl.multiple_of(q0 + tq, tq)))
        finalize()

    @pl.when(qi + 1 == pl.num_programs(2))
    def _():
        finalize()


def _fox(proj, k_aug, c_rows, merge_scale, heads, col0, ms_col0, tq, tk, unroll):
    b, s, _ = proj.shape
    assert tq % tk == 0 and (tq // tk) % unroll == 0
    return pl.pallas_call(
        functools.partial(_fox_kernel, tq=tq, tk=tk, unroll=unroll),
        grid=(b, heads, s // tq),
        in_specs=[pl.BlockSpec((1, s, HEAD_DIM), lambda i, h, q: (i, 0, col0 + h)),
                  pl.BlockSpec((1, 1, s, 2 * HEAD_DIM), lambda i, h, q: (i, h, 0, 0)),
                  pl.BlockSpec((1, s, HEAD_DIM), lambda i, h, q: (i, 0, col0 + 2 * heads + h)),
                  pl.BlockSpec((1, 1, s), lambda i, h, q: (i * heads + h, 0, 0)),
                  pl.BlockSpec((1, HEAD_DIM), lambda i, h, q: (0, ms_col0 + h))],
        out_specs=pl.BlockSpec((1, tq, HEAD_DIM), lambda i, h, q: (i, q, h)),
        out_shape=jax.ShapeDtypeStruct((b, s, heads * HEAD_DIM), BF16),
        scratch_shapes=[pltpu.VMEM((2, tk, tq), F32),
                        pltpu.VMEM((1, tq), F32),
                        pltpu.VMEM((1, tq), F32),
                        pltpu.VMEM((HEAD_DIM, tq), F32)],
        compiler_params=_params("arbitrary", "arbitrary", "arbitrary"),
        name="fox_attention",
    )(proj, k_aug, proj, c_rows, merge_scale)


def _mlstm_kernel(q_ref, k_ref, v_ref, og_ref, w_ref, b_ref, gp_ref, ir_ref, gr_ref, ms_ref, o_ref,
                  c_scr, n_scr, m_scr, g0_scr, xq_scr, xk_scr, *, heads, i_lane0, f_lane0):
    n = q_ref.shape[1]
    w = heads * HEAD_DIM

    @pl.when(pl.program_id(1) == 0)
    def _():
        c_scr[...] = jnp.zeros_like(c_scr)
        n_scr[...] = jnp.zeros_like(n_scr)
        m_scr[...] = jnp.zeros_like(m_scr)
        g0_scr[...] = jnp.zeros_like(g0_scr)
        xq_scr[0:CONV_HALO, :] = jnp.zeros((CONV_HALO, w), F32)
        xk_scr[0:CONV_HALO, :] = jnp.zeros((CONV_HALO, w), F32)

    def conv_silu(x_ref, x_scr, cols):
        x_scr[CONV_HALO:CONV_HALO + n, :] = x_ref[0].astype(F32)
        y = b_ref[:, cols] + jnp.zeros((n, w), F32)
        for j in range(CONV_WIDTH):
            shift = CONV_WIDTH - 1 - j
            y = y + w_ref[j:j + 1, cols] * x_scr[CONV_HALO - shift:CONV_HALO - shift + n, :]
        x_scr[0:CONV_HALO, :] = x_scr[n:n + CONV_HALO, :]
        return y * _sigmoid(y)

    q_all = conv_silu(q_ref, xq_scr, slice(0, w))
    k_all = conv_silu(k_ref, xk_scr, slice(w, 2 * w)) * (HEAD_DIM ** -0.5)
    gp = gp_ref[0]
    row = lax.broadcasted_iota(jnp.int32, (n, n), 0)
    col = lax.broadcasted_iota(jnp.int32, (n, n), 1)
    causal = row >= col

    for h in range(heads):
        cs = slice(h * HEAD_DIM, (h + 1) * HEAD_DIM)
        q = q_all[:, cs]
        k = k_all[:, cs]
        v = v_ref[0, :, cs]
        ic = gp[:, i_lane0 + h:i_lane0 + h + 1]
        gc = gp[:, f_lane0 + h:f_lane0 + h + 1]
        ir = ir_ref[0, h:h + 1, :]
        gr = gr_ref[0, h:h + 1, :]

        g0 = g0_scr[h][:, 0:1]
        m_prev = m_scr[h][:, 0:1]
        b_col = gc - g0
        dmat = jnp.where(causal, gc + (ir - gr), -jnp.inf)
        inter_log = b_col + m_prev
        m_q = jnp.maximum(inter_log, jnp.max(dmat, axis=-1, keepdims=True))
        w_intra = jnp.exp(dmat - m_q)
        w_inter = jnp.exp(inter_log - m_q)

        qb = q.astype(BF16)
        s = _dot_nt(qb, k.astype(BF16)) * w_intra
        cst = c_scr[h]
        nst = n_scr[h]
        num = _dot(s.astype(BF16), v) + w_inter * _dot(qb, cst.astype(BF16))
        den = jnp.sum(s, axis=-1, keepdims=True) + w_inter * jnp.sum(q * nst, axis=-1, keepdims=True)
        hid = num / jnp.maximum(jnp.abs(den), jnp.exp(-m_q))

        b_last = b_col[n - 1:n, :]
        k_log = b_last - b_col + ic
        m_new = jnp.maximum(b_last + m_prev, jnp.max(k_log, axis=0, keepdims=True))
        wk = jnp.exp(k_log - m_new)
        carry_scale = jnp.exp(b_last + m_prev - m_new)
        kw = k * wk
        c_scr[h] = carry_scale * cst + _dot_tn(kw.astype(BF16), v)
        n_scr[h] = carry_scale * nst + jnp.sum(kw, axis=0, keepdims=True)
        m_scr[h] = jnp.broadcast_to(m_new, (1, HEAD_DIM))
        g0_scr[h] = jnp.broadcast_to(gc[n - 1:n, :], (1, HEAD_DIM))

        og = og_ref[0, :, cs].astype(F32)
        o_ref[0, :, cs] = (_head_norm(hid) * _sigmoid(og) * ms_ref[:, cs]).astype(BF16)


def _mlstm(proj, conv_w, conv_b, gp, i_rows, g_rows, merge_scale, heads, col0, ms_col0, i_lane0, f_lane0, chunk):
    b, s, _ = proj.shape
    w = heads * HEAD_DIM
    assert (col0 * HEAD_DIM) % w == 0 and (ms_col0 * HEAD_DIM) % w == 0
    blk0 = col0 * HEAD_DIM // w
    grp = lambda base: pl.BlockSpec((1, chunk, w), lambda i, c: (i, c, blk0 + base))
    row_spec = pl.BlockSpec((1, heads, chunk), lambda i, c: (i, 0, c))
    state = lambda r, c: pltpu.VMEM((heads, r, c), F32)
    return pl.pallas_call(
        functools.partial(_mlstm_kernel, heads=heads, i_lane0=i_lane0, f_lane0=f_lane0),
        grid=(b, s // chunk),
        in_specs=[grp(0), grp(1), grp(2), grp(3),
                  pl.BlockSpec(conv_w.shape, lambda i, c: (0, 0)),
                  pl.BlockSpec(conv_b.shape, lambda i, c: (0, 0)),
                  pl.BlockSpec((1, chunk, GATE_LANES), lambda i, c: (i, c, 0)),
                  row_spec, row_spec,
                  pl.BlockSpec((1, w), lambda i, c: (0, ms_col0 * HEAD_DIM // w))],
        out_specs=pl.BlockSpec((1, chunk, w), lambda i, c: (i, c, 0)),
        out_shape=jax.ShapeDtypeStruct((b, s, w), BF16),
        scratch_shapes=[state(HEAD_DIM, HEAD_DIM), state(1, HEAD_DIM), state(1, HEAD_DIM), state(1, HEAD_DIM),
                        pltpu.VMEM((chunk + CONV_HALO, w), F32),
                        pltpu.VMEM((chunk + CONV_HALO, w), F32)],
        compiler_params=_params("arbitrary", "arbitrary"),
        name="mlstm",
    )(proj, proj, proj, proj, conv_w, conv_b, gp, i_rows, g_rows, merge_scale)


def _outproj_kernel(yr_ref, yf_ref, ym_ref, w_ref, x_ref, ga_ref, o_ref):
    r = yr_ref.shape[1]
    f = yf_ref.shape[1]
    y = _dot(yr_ref[...], w_ref[0:r, :]) + _dot(yf_ref[...], w_ref[r:r + f, :])
    y = y + _dot(ym_ref[...], w_ref[r + f:, :])
    o_ref[...] = x_ref[...] + ga_ref[0] * y


def _outproj(y_ret, y_fox, y_m, w_out, x2, g_a, seq):
    m, d = x2.shape
    tm = min(512, seq)
    bpr = seq // tm
    rows = lambda w: pl.BlockSpec((tm, w), lambda i: (i, 0))
    return pl.pallas_call(
        _outproj_kernel,
        grid=(m // tm,),
        in_specs=[rows(y_ret.shape[1]), rows(y_fox.shape[1]), rows(y_m.shape[1]),
                  pl.BlockSpec(w_out.shape, lambda i: (0, 0)),
                  rows(d),
                  pl.BlockSpec((1, 1, d), lambda i: (i // bpr, 0, 0))],
        out_specs=rows(d),
        out_shape=jax.ShapeDtypeStruct((m, d), F32),
        compiler_params=_params("arbitrary"),
        name="outproj",
    )(y_ret, y_fox, y_m, w_out, x2, g_a)


def _mlp_kernel(x_ref, nw_ref, sc_ref, sh_ref, gm_ref, w1_hbm, w2_hbm, fw_ref, o_ref,
                w1_buf, w2_buf, sem, h_scr, *, final_norm, tf, nf):
    i = pl.program_id(0)
    n = pl.num_programs(0)

    def tile_copies(f, slot):
        c0 = pl.multiple_of(f * tf, tf)
        return (pltpu.make_async_copy(w1_hbm.at[:, pl.ds(c0, tf)], w1_buf.at[slot], sem.at[0, slot]),
                pltpu.make_async_copy(w2_hbm.at[pl.ds(c0, tf), :], w2_buf.at[slot], sem.at[1, slot]))

    @pl.when(i == 0)
    def _():
        for cp in tile_copies(0, 0):
            cp.start()

    _modulated_norm_rows(x_ref, nw_ref, sc_ref, sh_ref, h_scr)
    o_ref[...] = jnp.zeros_like(o_ref)

    def body(f, carry):
        slot = f % 2

        @pl.when((f + 1 < nf) | (i + 1 < n))
        def _():
            for cp in tile_copies((f + 1) % nf, 1 - slot):
                cp.start()

        for cp in tile_copies(f, slot):
            cp.wait()
        a = jnp.maximum(_dot(h_scr[...], w1_buf[slot]), 0.0)
        o_ref[...] += _dot((a * a).astype(BF16), w2_buf[slot])
        return carry

    lax.fori_loop(0, nf, body, 0)

    y = x_ref[...] + gm_ref[0] * o_ref[...]
    if final_norm:
        ms = jnp.mean(y * y, axis=-1, keepdims=True)
        y = y * lax.rsqrt(ms + NORM_EPS) * fw_ref[...]
    o_ref[...] = y


def _mlp(x2, nw, sc, sh, g_m, w1, w2, final_w, seq, final_norm):
    m, d = x2.shape
    dff = w1.shape[1]
    tm = min(1024, seq)
    tf = min(512, dff)
    nf = dff // tf
    assert nf % 2 == 0
    bpr = seq // tm
    mod = pl.BlockSpec((1, 1, d), lambda i: (i // bpr, 0, 0))
    vec = pl.BlockSpec((1, d), lambda i: (0, 0))
    hbm = pl.BlockSpec(memory_space=pl.ANY)
    return pl.pallas_call(
        functools.partial(_mlp_kernel, final_norm=final_norm, tf=tf, nf=nf),
        grid=(m // tm,),
        in_specs=[pl.BlockSpec((tm, d), lambda i: (i, 0)),
                  vec, mod, mod, mod, hbm, hbm, vec],
        out_specs=pl.BlockSpec((tm, d), lambda i: (i, 0)),
        out_shape=jax.ShapeDtypeStruct((m, d), F32),
        scratch_shapes=[pltpu.VMEM((2, d, tf), BF16),
                        pltpu.VMEM((2, tf, d), BF16),
                        pltpu.SemaphoreType.DMA((2, 2)),
                        pltpu.VMEM((tm, d), BF16)],
        compiler_params=_params("arbitrary"),
        name="mlp",
    )(x2, nw, sc, sh, g_m, w1, w2, final_w)


def kernel(x, c, positions, ada_w, ada_b, norm_mix_w, norm_mlp_w, w_in, conv_w, conv_b, fox_f_bias,
           mlstm_i_bias, mlstm_f_bias, merge_scale, w_out, w_ff1, w_ff2, final_norm_w):
    b, s, d = x.shape
    depth = ada_w.shape[0]
    ret_w, fox_w, ml_w = d // 4, d // 2, d // 4
    ret_h, fox_h, ml_h = ret_w // HEAD_DIM, fox_w // HEAD_DIM, ml_w // HEAD_DIM
    assert s % 128 == 0 and d % (4 * HEAD_DIM) == 0
    assert fox_h + 2 * ml_h <= GATE_LANES

    sizes = [ret_w] * 4 + [fox_w] * 3 + [fox_h] + [ml_w] * 4 + [ml_h, ml_h]
    offs = [0]
    for sz in sizes:
        offs.append(offs[-1] + sz)
    n_gate = fox_h + 2 * ml_h
    ret_col0 = 0
    fox_col0 = (4 * ret_w) // HEAD_DIM
    ml_col0 = (4 * ret_w + 3 * fox_w) // HEAD_DIM
    i_lane0, f_lane0 = fox_h, fox_h + ml_h

    ret_chunk = min(RET_CHUNK, s)
    ml_chunk = min(MLSTM_CHUNK, s)
    tq = min(FOX_TQ, s)
    tk = min(FOX_TK, tq)

    mod = _ada_mod(c, ada_w, ada_b)
    cos, sin = _rope_tables(positions)
    cum_mask = jnp.zeros((1, GATE_LANES), F32).at[0, :fox_h].set(1.0).at[0, f_lane0:f_lane0 + ml_h].set(1.0)

    x2 = x.reshape(b * s, d)
    for layer in range(depth):
        sh_a, sc_a, g_a, sh_m, sc_m, g_m = [t.reshape(b, 1, d) for t in jnp.split(mod[layer], N_MOD, axis=-1)]
        wl = w_in[layer]
        w_main = jnp.concatenate([wl[:, offs[0]:offs[7]], wl[:, offs[8]:offs[12]]], axis=1).astype(BF16)
        w_gate = jnp.concatenate([wl[:, offs[7]:offs[8]], wl[:, offs[12]:offs[14]],
                                  jnp.zeros((d, GATE_LANES - n_gate), F32)], axis=1).astype(BF16)
        gate_bias = jnp.pad(jnp.concatenate([fox_f_bias[layer], mlstm_i_bias[layer], mlstm_f_bias[layer]]),
                            (0, GATE_LANES - n_gate)).reshape(1, GATE_LANES)
        ms = merge_scale[layer].reshape(1, d)

        proj, gates = _inproj(x2, norm_mix_w[layer].reshape(1, d), sc_a, sh_a, w_main, w_gate, s)
        proj = proj.reshape(b, s, -1)
        gp = _gates(gates.reshape(b, s, GATE_LANES), gate_bias, cum_mask)
        gp_t = jnp.swapaxes(gp[:, :, :n_gate], 1, 2)
        c_rows = gp_t[:, :fox_h].reshape(b * fox_h, 1, s)
        i_rows = gp_t[:, i_lane0:i_lane0 + ml_h]
        g_rows = gp_t[:, f_lane0:f_lane0 + ml_h]

        y_ret = _retention(proj, cos, sin, ms, ret_h, ret_col0, ret_chunk)
        k_aug = _fox_keys(proj, gp, fox_h, fox_col0 + fox_h, tk)
        y_fox = _fox(proj, k_aug, c_rows, ms, fox_h, fox_col0, ret_h, tq, tk, FOX_UNROLL if tq // tk % FOX_UNROLL == 0 else 1)
        y_m = _mlstm(proj, conv_w[layer], conv_b[layer].reshape(1, -1), gp, i_rows, g_rows, ms,
                     ml_h, ml_col0, ret_h + fox_h, i_lane0, f_lane0, ml_chunk)

        x2 = _outproj(y_ret.reshape(b * s, -1), y_fox.reshape(b * s, -1), y_m.reshape(b * s, -1),
                      w_out[layer].astype(BF16), x2, g_a, s)
        x2 = _mlp(x2, norm_mlp_w[layer].reshape(1, d), sc_m, sh_m, g_m,
                  w_ff1[layer].astype(BF16), w_ff2[layer].astype(BF16),
                  final_norm_w.reshape(1, d), s, layer == depth - 1)
    return x2.reshape(b, s, d)
```

```python
import functools
import math

import jax
import jax.numpy as jnp
from jax import lax
from jax.experimental import pallas as pl
from jax.experimental.pallas import tpu as pltpu

HEAD_DIM = 128
CONV_WIDTH = 4
ROPE_BASE = 10000.0
NORM_EPS = 1e-6
N_MOD = 6
GATE_LANES = 128
LOG2E = math.log2(math.e)
BIAS_PIECES = 3
CONV_HALO = 8

ADA_TN, ADA_KC = 1024, 256
ROPE_TS = 2048
INPROJ_TM, INPROJ_TN = 1024, (1792, 1024, 512, 256, 128)
GATES_TS = 512
RET_CHUNK, MLSTM_CHUNK = 256, 512
FOX_KEYS_TS = 1024
FOX_TQ, FOX_TK, FOX_UNROLL = 1024, 512, 2
OUTPROJ_TM = 512
MLP_TM, MLP_TF = 1024, 512

F32 = jnp.float32
BF16 = jnp.bfloat16

VMEM_LIMIT_BYTES = 56 * 1024 * 1024


def _params(*sem):
    return pltpu.CompilerParams(dimension_semantics=sem, vmem_limit_bytes=VMEM_LIMIT_BYTES)


def _dot(a, b):
    return jnp.dot(a, b, preferred_element_type=F32)


def _dot_nt(a, b):
    return lax.dot_general(a, b, (((1,), (1,)), ((), ())), preferred_element_type=F32)


def _dot_tn(a, b):
    return lax.dot_general(a, b, (((0,), (0,)), ((), ())), preferred_element_type=F32)


def _sigmoid(x):
    return 1.0 / (1.0 + jnp.exp(-x))


def _log_sigmoid(x):
    return jnp.minimum(x, 0.0) - jnp.log1p(jnp.exp(-jnp.abs(x)))


def _head_norm(y):
    mu = jnp.mean(y, axis=-1, keepdims=True)
    yc = y - mu
    var = jnp.mean(yc * yc, axis=-1, keepdims=True)
    return yc * lax.rsqrt(var + NORM_EPS)


def _ada_kernel(ct_ref, w_ref, b_ref, o_ref, *, kc):
    d, nb = ct_ref.shape
    tn = w_ref.shape[2]

    def body(i, accs):
        r0 = pl.multiple_of(i * kc, kc)
        ck = ct_ref[pl.ds(r0, kc), :]
        ck = ck * _sigmoid(ck)
        wk = w_ref[0, pl.ds(r0, kc), :]
        return tuple(acc + jnp.sum(wk * ck[:, b:b + 1], axis=0, keepdims=True)
                     for b, acc in enumerate(accs))

    accs = lax.fori_loop(0, d // kc, body, tuple(jnp.zeros((1, tn), F32) for _ in range(nb)))
    o_ref[0] = jnp.concatenate(accs, axis=0) + b_ref[0]


def _ada_mod(c, ada_w, ada_b):
    depth, d, n = ada_w.shape
    nb = c.shape[0]
    tn = min(ADA_TN, n)
    kc = min(ADA_KC, d)
    return pl.pallas_call(
        functools.partial(_ada_kernel, kc=kc),
        grid=(depth, n // tn),
        in_specs=[pl.BlockSpec((d, nb), lambda l, j: (0, 0)),
                  pl.BlockSpec((1, d, tn), lambda l, j: (l, 0, j)),
                  pl.BlockSpec((1, 1, tn), lambda l, j: (l, 0, j))],
        out_specs=pl.BlockSpec((1, nb, tn), lambda l, j: (l, 0, j)),
        out_shape=jax.ShapeDtypeStruct((depth, nb, n), F32),
        compiler_params=_params("arbitrary", "arbitrary"),
        name="ada_mod",
    )(c.T, ada_w, ada_b.reshape(depth, 1, n))


def _rope_kernel(pos_ref, invf_ref, sign_ref, cos_ref, sin_ref):
    ang = pos_ref[0] * invf_ref[...]
    cos_ref[0] = jnp.cos(ang)
    sin_ref[0] = jnp.sin(ang) * sign_ref[...]


def _rope_tables(positions):
    b, s = positions.shape
    ts = min(ROPE_TS, s)
    half = HEAD_DIM // 2
    inv_freq = ROPE_BASE ** (-jnp.arange(0, HEAD_DIM, 2, dtype=F32) / HEAD_DIM)
    invf = jnp.concatenate([inv_freq, inv_freq]).reshape(1, HEAD_DIM)
    sign = jnp.concatenate([-jnp.ones((half,), F32), jnp.ones((half,), F32)]).reshape(1, HEAD_DIM)
    pos = positions.astype(F32).reshape(b, s, 1)
    out = jax.ShapeDtypeStruct((b, s, HEAD_DIM), F32)
    return pl.pallas_call(
        _rope_kernel,
        grid=(b, s // ts),
        in_specs=[pl.BlockSpec((1, ts, 1), lambda i, j: (i, j, 0)),
                  pl.BlockSpec((1, HEAD_DIM), lambda i, j: (0, 0)),
                  pl.BlockSpec((1, HEAD_DIM), lambda i, j: (0, 0))],
        out_specs=[pl.BlockSpec((1, ts, HEAD_DIM), lambda i, j: (i, j, 0))] * 2,
        out_shape=[out, out],
        compiler_params=_params("arbitrary", "arbitrary"),
        name="rope_tables",
    )(pos, invf, sign)


def _modulated_norm_rows(x_ref, nw_ref, sc_ref, sh_ref, h_scr):
    x = x_ref[...]
    inv = lax.rsqrt(jnp.mean(x * x, axis=-1, keepdims=True) + NORM_EPS)
    h_scr[...] = ((x_ref[...] * inv * nw_ref[...]) * (1.0 + sc_ref[0]) + sh_ref[0]).astype(BF16)


def _inproj_kernel(x_ref, nw_ref, sc_ref, sh_ref, w_ref, wg_ref, o_ref, g_ref, h_scr):
    @pl.when(pl.program_id(1) == 0)
    def _():
        _modulated_norm_rows(x_ref, nw_ref, sc_ref, sh_ref, h_scr)
        g_ref[...] = _dot(h_scr[...], wg_ref[...])

    o_ref[...] = _dot(h_scr[...], w_ref[...]).astype(BF16)


def _inproj(x2, nw, sc, sh, w_main, w_gate, seq):
    m, d = x2.shape
    nm = w_main.shape[1]
    tm = min(INPROJ_TM, seq)
    tn = next(t for t in INPROJ_TN if nm % t == 0)
    bpr = seq // tm
    return pl.pallas_call(
        _inproj_kernel,
        grid=(m // tm, nm // tn),
        in_specs=[pl.BlockSpec((tm, d), lambda i, j: (i, 0)),
                  pl.BlockSpec((1, d), lambda i, j: (0, 0)),
                  pl.BlockSpec((1, 1, d), lambda i, j: (i // bpr, 0, 0)),
                  pl.BlockSpec((1, 1, d), lambda i, j: (i // bpr, 0, 0)),
                  pl.BlockSpec((d, tn), lambda i, j: (0, j)),
                  pl.BlockSpec((d, GATE_LANES), lambda i, j: (0, 0))],
        out_specs=[pl.BlockSpec((tm, tn), lambda i, j: (i, j)),
                   pl.BlockSpec((tm, GATE_LANES), lambda i, j: (i, 0))],
        out_shape=[jax.ShapeDtypeStruct((m, nm), BF16),
                   jax.ShapeDtypeStruct((m, GATE_LANES), F32)],
        scratch_shapes=[pltpu.VMEM((tm, d), BF16)],
        compiler_params=_params("arbitrary", "arbitrary"),
        name="inproj",
    )(x2, nw, sc, sh, w_main, w_gate)


def _gates_kernel(g_ref, bias_ref, cum_ref, o_ref, carry):
    @pl.when(pl.program_id(1) == 0)
    def _():
        carry[...] = jnp.zeros_like(carry)

    g = g_ref[0] + bias_ref[...]
    ls = _log_sigmoid(g)
    ts = ls.shape[0]
    row = lax.broadcasted_iota(jnp.int32, (ts, ts), 0)
    col = lax.broadcasted_iota(jnp.int32, (ts, ts), 1)
    tri = jnp.where(row >= col, 1.0, 0.0).astype(BF16)
    hi = ls.astype(BF16)
    r1 = ls - hi.astype(F32)
    mid = r1.astype(BF16)
    lo = (r1 - mid.astype(F32)).astype(BF16)
    csum = (_dot(tri, hi) + _dot(tri, mid)) + _dot(tri, lo) + carry[...]
    carry[...] = csum[ts - 1:ts, :]
    o_ref[0] = jnp.where(cum_ref[...] > 0.5, csum, g)


def _gates(g3, bias, cum_mask):
    b, s, n = g3.shape
    ts = min(GATES_TS, s)
    return pl.pallas_call(
        _gates_kernel,
        grid=(b, s // ts),
        in_specs=[pl.BlockSpec((1, ts, n), lambda i, j: (i, j, 0)),
                  pl.BlockSpec((1, n), lambda i, j: (0, 0)),
                  pl.BlockSpec((1, n), lambda i, j: (0, 0))],
        out_specs=pl.BlockSpec((1, ts, n), lambda i, j: (i, j, 0)),
        out_shape=jax.ShapeDtypeStruct((b, s, n), F32),
        scratch_shapes=[pltpu.VMEM((1, n), F32)],
        compiler_params=_params("arbitrary", "arbitrary"),
        name="gates",
    )(g3, bias, cum_mask)


def _retention_kernel(q_ref, k_ref, v_ref, g_ref, cos_ref, sin_ref, ms_ref, o_ref, state, decay, *, heads):
    n = cos_ref.shape[1]
    log_gamma = [math.log(1.0 - 2.0 ** (-5.0 - h)) for h in range(heads)]

    @pl.when(pl.program_id(1) == 0)
    def _():
        state[...] = jnp.zeros_like(state)
        row = lax.broadcasted_iota(jnp.int32, (n, n), 0)
        col = lax.broadcasted_iota(jnp.int32, (n, n), 1)
        rel = (row - col).astype(F32)
        for h in range(heads):
            decay[h] = jnp.where(rel >= 0, jnp.exp(jnp.maximum(rel, 0.0) * log_gamma[h]), 0.0)

    cos = cos_ref[0]
    sin = sin_ref[0]
    half = HEAD_DIM // 2
    idx = lax.broadcasted_iota(jnp.int32, (n, 1), 0).astype(F32)

    def rot(t):
        return t * cos + pltpu.roll(t, half, 1) * sin

    for h in range(heads):
        lg = log_gamma[h]
        cs = slice(h * HEAD_DIM, (h + 1) * HEAD_DIM)
        q = rot(q_ref[0, :, cs].astype(F32))
        k = rot(k_ref[0, :, cs].astype(F32)) * (HEAD_DIM ** -0.5)
        v = v_ref[0, :, cs]
        q_decay = jnp.exp((idx + 1.0) * lg)
        k_decay = jnp.exp((n - 1.0 - idx) * lg)
        chunk_decay = math.exp(n * lg)

        scores = _dot_nt(q.astype(BF16), k.astype(BF16)) * decay[h]
        intra = _dot(scores.astype(BF16), v)
        st = state[h]
        inter = _dot((q * q_decay).astype(BF16), st.astype(BF16))
        state[h] = st * chunk_decay + _dot_tn((k * k_decay).astype(BF16), v)

        y = _head_norm(intra + inter)
        gate = g_ref[0, :, cs].astype(F32)
        o_ref[0, :, cs] = (y * (gate * _sigmoid(gate)) * ms_ref[:, cs]).astype(BF16)


def _retention(proj, cos, sin, merge_scale, heads, col0, chunk):
    b, s, _ = proj.shape
    w = heads * HEAD_DIM
    assert (col0 * HEAD_DIM) % w == 0
    blk0 = col0 * HEAD_DIM // w
    grp = lambda base: pl.BlockSpec((1, chunk, w), lambda i, c: (i, c, blk0 + base))
    tok = pl.BlockSpec((1, chunk, HEAD_DIM), lambda i, c: (i, c, 0))
    return pl.pallas_call(
        functools.partial(_retention_kernel, heads=heads),
        grid=(b, s // chunk),
        in_specs=[grp(0), grp(1), grp(2), grp(3), tok, tok,
                  pl.BlockSpec((1, w), lambda i, c: (0, 0))],
        out_specs=pl.BlockSpec((1, chunk, w), lambda i, c: (i, c, 0)),
        out_shape=jax.ShapeDtypeStruct((b, s, w), BF16),
        scratch_shapes=[pltpu.VMEM((heads, HEAD_DIM, HEAD_DIM), F32),
                        pltpu.VMEM((heads, chunk, chunk), F32)],
        compiler_params=_params("arbitrary", "arbitrary"),
        name="retention",
    )(proj, proj, proj, proj, cos, sin, merge_scale)


def _fox_keys_kernel(k_ref, gp_ref, o_ref, *, tk, heads):
    ts = k_ref.shape[1]
    c = gp_ref[0]
    parts = []
    for i in range(ts // tk):
        blk = c[i * tk:(i + 1) * tk]
        parts.append((blk[0:1, :] - blk) * LOG2E)
    x = jnp.concatenate(parts, axis=0) if len(parts) > 1 else parts[0]
    hi = x.astype(BF16)
    r1 = x - hi.astype(F32)
    mid = r1.astype(BF16)
    lo = (r1 - mid.astype(F32)).astype(BF16)
    row = lax.broadcasted_iota(jnp.int32, (GATE_LANES, HEAD_DIM), 0)
    col = lax.broadcasted_iota(jnp.int32, (GATE_LANES, HEAD_DIM), 1)
    for h in range(heads):
        sel = lambda piece: jnp.where((row == h) & (col == piece), 1.0, 0.0).astype(BF16)
        bias = _dot(hi, sel(0)) + _dot(mid, sel(1)) + _dot(lo, sel(2))
        o_ref[0, h] = jnp.concatenate([k_ref[0, :, h * HEAD_DIM:(h + 1) * HEAD_DIM], bias.astype(BF16)], axis=1)


def _fox_keys(proj, gp, heads, k_col0, tk):
    b, s, _ = proj.shape
    ts = min(FOX_KEYS_TS, s)
    w = heads * HEAD_DIM
    assert (k_col0 * HEAD_DIM) % w == 0 and ts % tk == 0
    return pl.pallas_call(
        functools.partial(_fox_keys_kernel, tk=tk, heads=heads),
        grid=(b, s // ts),
        in_specs=[pl.BlockSpec((1, ts, w), lambda i, j: (i, j, k_col0 * HEAD_DIM // w)),
                  pl.BlockSpec((1, ts, GATE_LANES), lambda i, j: (i, j, 0))],
        out_specs=pl.BlockSpec((1, heads, ts, 2 * HEAD_DIM), lambda i, j: (i, 0, j, 0)),
        out_shape=jax.ShapeDtypeStruct((b, heads, s, 2 * HEAD_DIM), BF16),
        compiler_params=_params("arbitrary", "arbitrary"),
        name="fox_keys",
    )(proj, gp)


def _fox_kernel(q_ref, k_ref, v_ref, c_ref, ms_ref, o_ref, s_scr, m_scr, l_scr, acc_scr,
                *, tq, tk, unroll):
    ndiag = tq // tk
    static_slots = unroll % 2 == 0 and ndiag % 2 == 0
    qi = pl.program_id(2)
    nfull = qi * ndiag
    q0 = pl.multiple_of(qi * tq, tq)

    def augmented_queries(start):
        q_t = (q_ref[0, pl.ds(start, tq), :].astype(F32) * (HEAD_DIM ** -0.5 * LOG2E)).T.astype(BF16)
        row = lax.broadcasted_iota(jnp.int32, (HEAD_DIM, tq), 0)
        return jnp.concatenate([q_t, jnp.where(row < BIAS_PIECES, 1.0, 0.0).astype(BF16)], axis=0)

    q_aug = augmented_queries(q0)
    c_first = c_ref[0, :, pl.ds(q0, HEAD_DIM)][:, 0:1]

    m_scr[...] = jnp.full_like(m_scr, -jnp.inf)
    l_scr[...] = jnp.zeros_like(l_scr)
    acc_scr[...] = jnp.zeros_like(acc_scr)

    def scores(j, slot, c0):
        k0 = pl.multiple_of(j * tk, tk)
        s_scr[slot, :, c0:] = _dot(k_ref[0, 0, pl.ds(k0, tk), :], q_aug[:, c0:])

    def softmax(j, slot, diag):
        k0 = pl.multiple_of(j * tk, tk)
        c0 = 0 if diag is None else diag * tk
        delta = (c_first - c_ref[0, :, pl.ds(k0, HEAD_DIM)][:, 0:1]) * LOG2E
        t = s_scr[slot, :, c0:]
        if diag is not None:
            key = lax.broadcasted_iota(jnp.int32, t.shape, 0)
            qry = lax.broadcasted_iota(jnp.int32, t.shape, 1)
            t = jnp.where(qry >= key, t, -jnp.inf)
        m_prev = m_scr[:, c0:]
        m_new = jnp.maximum(m_prev, jnp.max(t, axis=0, keepdims=True) + delta)
        p = jnp.exp2(t - (m_new - delta)).astype(BF16)
        alpha = jnp.exp2(m_prev - m_new)
        half = p[:tk // 2] + p[tk // 2:]
        quarter = half[:tk // 4] + half[tk // 4:]
        l_scr[:, c0:] = alpha * l_scr[:, c0:] + jnp.sum(quarter.astype(F32), axis=0, keepdims=True)
        m_scr[:, c0:] = m_new
        acc_scr[:, c0:] = alpha * acc_scr[:, c0:] + _dot_tn(v_ref[0, pl.ds(k0, tk), :], p)

    def step(j, slot, diag, last):
        if not last:
            scores(j + 1, 1 - slot, 0 if diag is None else (diag + 1) * tk)
        softmax(j, slot, diag)

    @pl.when(qi == 0)
    def _():
        scores(0, 0, 0)

    def body(i, carry):
        for u in range(unroll):
            j = i * unroll + u
            step(j, u % 2 if static_slots else j & 1, None, False)
        return carry

    lax.fori_loop(0, nfull // unroll, body, 0)
    for d in range(ndiag):
        j = nfull + d
        slot = d % 2 if ndiag % 2 == 0 else j & 1
        step(j, slot, d, d == ndiag - 1)

    def finalize():
        out = (acc_scr[...] / l_scr[...]).T
        o_ref[0] = (_head_norm(out) * ms_ref[...]).astype(BF16)

    @pl.when(qi + 1 < pl.num_programs(2))
    def _():
        s_scr[0] = _dot(k_ref[0, 0, 0:tk, :], augmented_queries(pl.multiple_of(q0 + tq, tq)))
        finalize()

    @pl.when(qi + 1 == pl.num_programs(2))
    def _():
        finalize()


def _fox(proj, k_aug, c_rows, merge_scale, heads, col0, ms_col0, tq, tk, unroll):
    b, s, _ = proj.shape
    assert tq % tk == 0 and (tq // tk) % unroll == 0
    return pl.pallas_call(
        functools.partial(_fox_kernel, tq=tq, tk=tk, unroll=unroll),
        grid=(b, heads, s // tq),
        in_specs=[pl.BlockSpec((1, s, HEAD_DIM), lambda i, h, q: (i, 0, col0 + h)),
                  pl.BlockSpec((1, 1, s, 2 * HEAD_DIM), lambda i, h, q: (i, h, 0, 0)),
                  pl.BlockSpec((1, s, HEAD_DIM), lambda i, h, q: (i, 0, col0 + 2 * heads + h)),
                  pl.BlockSpec((1, 1, s), lambda i, h, q: (i * heads + h, 0, 0)),
                  pl.BlockSpec((1, HEAD_DIM), lambda i, h, q: (0, ms_col0 + h))],
        out_specs=pl.BlockSpec((1, tq, HEAD_DIM), lambda i, h, q: (i, q, h)),
        out_shape=jax.ShapeDtypeStruct((b, s, heads * HEAD_DIM), BF16),
        scratch_shapes=[pltpu.VMEM((2, tk, tq), F32),
                        pltpu.VMEM((1, tq), F32),
                        pltpu.VMEM((1, tq), F32),
                        pltpu.VMEM((HEAD_DIM, tq), F32)],
        compiler_params=_params("arbitrary", "arbitrary", "arbitrary"),
        name="fox_attention",
    )(proj, k_aug, proj, c_rows, merge_scale)


def _mlstm_kernel(q_ref, k_ref, v_ref, og_ref, w_ref, b_ref, gp_ref, ir_ref, gr_ref, ms_ref, o_ref,
                  c_scr, n_scr, m_scr, g0_scr, xq_scr, xk_scr, *, heads, i_lane0, f_lane0):
    n = q_ref.shape[1]
    w = heads * HEAD_DIM

    @pl.when(pl.program_id(1) == 0)
    def _():
        c_scr[...] = jnp.zeros_like(c_scr)
        n_scr[...] = jnp.zeros_like(n_scr)
        m_scr[...] = jnp.zeros_like(m_scr)
        g0_scr[...] = jnp.zeros_like(g0_scr)
        xq_scr[0:CONV_HALO, :] = jnp.zeros((CONV_HALO, w), F32)
        xk_scr[0:CONV_HALO, :] = jnp.zeros((CONV_HALO, w), F32)

    def conv_silu(x_ref, x_scr, cols):
        x_scr[CONV_HALO:CONV_HALO + n, :] = x_ref[0].astype(F32)
        y = b_ref[:, cols] + jnp.zeros((n, w), F32)
        for j in range(CONV_WIDTH):
            shift = CONV_WIDTH - 1 - j
            y = y + w_ref[j:j + 1, cols] * x_scr[CONV_HALO - shift:CONV_HALO - shift + n, :]
        x_scr[0:CONV_HALO, :] = x_scr[n:n + CONV_HALO, :]
        return y * _sigmoid(y)

    q_all = conv_silu(q_ref, xq_scr, slice(0, w))
    k_all = conv_silu(k_ref, xk_scr, slice(w, 2 * w)) * (HEAD_DIM ** -0.5)
    gp = gp_ref[0]
    row = lax.broadcasted_iota(jnp.int32, (n, n), 0)
    col = lax.broadcasted_iota(jnp.int32, (n, n), 1)
    causal = row >= col

    for h in range(heads):
        cs = slice(h * HEAD_DIM, (h + 1) * HEAD_DIM)
        q = q_all[:, cs]
        k = k_all[:, cs]
        v = v_ref[0, :, cs]
        ic = gp[:, i_lane0 + h:i_lane0 + h + 1]
        gc = gp[:, f_lane0 + h:f_lane0 + h + 1]
        ir = ir_ref[0, h:h + 1, :]
        gr = gr_ref[0, h:h + 1, :]

        g0 = g0_scr[h][:, 0:1]
        m_prev = m_scr[h][:, 0:1]
        b_col = gc - g0
        dmat = jnp.where(causal, gc + (ir - gr), -jnp.inf)
        inter_log = b_col + m_prev
        m_q = jnp.maximum(inter_log, jnp.max(dmat, axis=-1, keepdims=True))
        w_intra = jnp.exp(dmat - m_q)
        w_inter = jnp.exp(inter_log - m_q)

        qb = q.astype(BF16)
        s = _dot_nt(qb, k.astype(BF16)) * w_intra
        cst = c_scr[h]
        nst = n_scr[h]
        num = _dot(s.astype(BF16), v) + w_inter * _dot(qb, cst.astype(BF16))
        den = jnp.sum(s, axis=-1, keepdims=True) + w_inter * jnp.sum(q * nst, axis=-1, keepdims=True)
        hid = num / jnp.maximum(jnp.abs(den), jnp.exp(-m_q))

        b_last = b_col[n - 1:n, :]
        k_log = b_last - b_col + ic
        m_new = jnp.maximum(b_last + m_prev, jnp.max(k_log, axis=0, keepdims=True))
        wk = jnp.exp(k_log - m_new)
        carry_scale = jnp.exp(b_last + m_prev - m_new)
        kw = k * wk
        c_scr[h] = carry_scale * cst + _dot_tn(kw.astype(BF16), v)
        n_scr[h] = carry_scale * nst + jnp.sum(kw, axis=0, keepdims=True)
        m_scr[h] = jnp.broadcast_to(m_new, (1, HEAD_DIM))
        g0_scr[h] = jnp.broadcast_to(gc[n - 1:n, :], (1, HEAD_DIM))

        og = og_ref[0, :, cs].astype(F32)
        o_ref[0, :, cs] = (_head_norm(hid) * _sigmoid(og) * ms_ref[:, cs]).astype(BF16)


def _mlstm(proj, conv_w, conv_b, gp, i_rows, g_rows, merge_scale, heads, col0, ms_col0, i_lane0, f_lane0, chunk):
    b, s, _ = proj.shape
    w = heads * HEAD_DIM
    assert (col0 * HEAD_DIM) % w == 0 and (ms_col0 * HEAD_DIM) % w == 0
    blk0 = col0 * HEAD_DIM // w
    grp = lambda base: pl.BlockSpec((1, chunk, w), lambda i, c: (i, c, blk0 + base))
    row_spec = pl.BlockSpec((1, heads, chunk), lambda i, c: (i, 0, c))
    state = lambda r, c: pltpu.VMEM((heads, r, c), F32)
    return pl.pallas_call(
        functools.partial(_mlstm_kernel, heads=heads, i_lane0=i_lane0, f_lane0=f_lane0),
        grid=(b, s // chunk),
        in_specs=[grp(0), grp(1), grp(2), grp(3),
                  pl.BlockSpec(conv_w.shape, lambda i, c: (0, 0)),
                  pl.BlockSpec(conv_b.shape, lambda i, c: (0, 0)),
                  pl.BlockSpec((1, chunk, GATE_LANES), lambda i, c: (i, c, 0)),
                  row_spec, row_spec,
                  pl.BlockSpec((1, w), lambda i, c: (0, ms_col0 * HEAD_DIM // w))],
        out_specs=pl.BlockSpec((1, chunk, w), lambda i, c: (i, c, 0)),
        out_shape=jax.ShapeDtypeStruct((b, s, w), BF16),
        scratch_shapes=[state(HEAD_DIM, HEAD_DIM), state(1, HEAD_DIM), state(1, HEAD_DIM), state(1, HEAD_DIM),
                        pltpu.VMEM((chunk + CONV_HALO, w), F32),
                        pltpu.VMEM((chunk + CONV_HALO, w), F32)],
        compiler_params=_params("arbitrary", "arbitrary"),
        name="mlstm",
    )(proj, proj, proj, proj, conv_w, conv_b, gp, i_rows, g_rows, merge_scale)


def _outproj_kernel(yr_ref, yf_ref, ym_ref, w_ref, x_ref, ga_ref, o_ref):
    r = yr_ref.shape[1]
    f = yf_ref.shape[1]
    y = _dot(yr_ref[...], w_ref[0:r, :]) + _dot(yf_ref[...], w_ref[r:r + f, :])
    y = y + _dot(ym_ref[...], w_ref[r + f:, :])
    o_ref[...] = x_ref[...] + ga_ref[0] * y


def _outproj(y_ret, y_fox, y_m, w_out, x2, g_a, seq):
    m, d = x2.shape
    tm = min(OUTPROJ_TM, seq)
    bpr = seq // tm
    rows = lambda w: pl.BlockSpec((tm, w), lambda i: (i, 0))
    return pl.pallas_call(
        _outproj_kernel,
        grid=(m // tm,),
        in_specs=[rows(y_ret.shape[1]), rows(y_fox.shape[1]), rows(y_m.shape[1]),
                  pl.BlockSpec(w_out.shape, lambda i: (0, 0)),
                  rows(d),
                  pl.BlockSpec((1, 1, d), lambda i: (i // bpr, 0, 0))],
        out_specs=rows(d),
        out_shape=jax.ShapeDtypeStruct((m, d), F32),
        compiler_params=_params("arbitrary"),
        name="outproj",
    )(y_ret, y_fox, y_m, w_out, x2, g_a)


def _mlp_kernel(x_ref, nw_ref, sc_ref, sh_ref, gm_ref, w1_hbm, w2_hbm, fw_ref, o_ref,
                w1_buf, w2_buf, sem, h_scr, *, final_norm, tf, nf):
    i = pl.program_id(0)
    n = pl.num_programs(0)

    def tile_copies(f, slot):
        c0 = pl.multiple_of(f * tf, tf)
        return (pltpu.make_async_copy(w1_hbm.at[:, pl.ds(c0, tf)], w1_buf.at[slot], sem.at[0, slot]),
                pltpu.make_async_copy(w2_hbm.at[pl.ds(c0, tf), :], w2_buf.at[slot], sem.at[1, slot]))

    @pl.when(i == 0)
    def _():
        for cp in tile_copies(0, 0):
            cp.start()

    _modulated_norm_rows(x_ref, nw_ref, sc_ref, sh_ref, h_scr)
    o_ref[...] = jnp.zeros_like(o_ref)

    def body(f, carry):
        slot = f % 2

        @pl.when((f + 1 < nf) | (i + 1 < n))
        def _():
            for cp in tile_copies((f + 1) % nf, 1 - slot):
                cp.start()

        for cp in tile_copies(f, slot):
            cp.wait()
        a = jnp.maximum(_dot(h_scr[...], w1_buf[slot]), 0.0)
        o_ref[...] += _dot((a * a).astype(BF16), w2_buf[slot])
        return carry

    lax.fori_loop(0, nf, body, 0)

    y = x_ref[...] + gm_ref[0] * o_ref[...]
    if final_norm:
        ms = jnp.mean(y * y, axis=-1, keepdims=True)
        y = y * lax.rsqrt(ms + NORM_EPS) * fw_ref[...]
    o_ref[...] = y


def _mlp(x2, nw, sc, sh, g_m, w1, w2, final_w, seq, final_norm):
    m, d = x2.shape
    dff = w1.shape[1]
    tm = min(MLP_TM, seq)
    tf = min(MLP_TF, dff)
    nf = dff // tf
    assert nf % 2 == 0
    bpr = seq // tm
    mod = pl.BlockSpec((1, 1, d), lambda i: (i // bpr, 0, 0))
    vec = pl.BlockSpec((1, d), lambda i: (0, 0))
    hbm = pl.BlockSpec(memory_space=pl.ANY)
    return pl.pallas_call(
        functools.partial(_mlp_kernel, final_norm=final_norm, tf=tf, nf=nf),
        grid=(m // tm,),
        in_specs=[pl.BlockSpec((tm, d), lambda i: (i, 0)),
                  vec, mod, mod, mod, hbm, hbm, vec],
        out_specs=pl.BlockSpec((tm, d), lambda i: (i, 0)),
        out_shape=jax.ShapeDtypeStruct((m, d), F32),
        scratch_shapes=[pltpu.VMEM((2, d, tf), BF16),
                        pltpu.VMEM((2, tf, d), BF16),
                        pltpu.SemaphoreType.DMA((2, 2)),
                        pltpu.VMEM((tm, d), BF16)],
        compiler_params=_params("arbitrary"),
        name="mlp",
    )(x2, nw, sc, sh, g_m, w1, w2, final_w)


def kernel(x, c, positions, ada_w, ada_b, norm_mix_w, norm_mlp_w, w_in, conv_w, conv_b, fox_f_bias,
           mlstm_i_bias, mlstm_f_bias, merge_scale, w_out, w_ff1, w_ff2, final_norm_w):
    b, s, d = x.shape
    depth = ada_w.shape[0]
    ret_w, fox_w, ml_w = d // 4, d // 2, d // 4
    ret_h, fox_h, ml_h = ret_w // HEAD_DIM, fox_w // HEAD_DIM, ml_w // HEAD_DIM
    assert s % 128 == 0 and d % (4 * HEAD_DIM) == 0
    assert fox_h + 2 * ml_h <= GATE_LANES

    sizes = [ret_w] * 4 + [fox_w] * 3 + [fox_h] + [ml_w] * 4 + [ml_h, ml_h]
    offs = [0]
    for sz in sizes:
        offs.append(offs[-1] + sz)
    n_gate = fox_h + 2 * ml_h
    ret_col0 = 0
    fox_col0 = (4 * ret_w) // HEAD_DIM
    ml_col0 = (4 * ret_w + 3 * fox_w) // HEAD_DIM
    i_lane0, f_lane0 = fox_h, fox_h + ml_h

    ret_chunk = min(RET_CHUNK, s)
    ml_chunk = min(MLSTM_CHUNK, s)
    tq = min(FOX_TQ, s)
    tk = min(FOX_TK, tq)

    mod = _ada_mod(c, ada_w, ada_b)
    cos, sin = _rope_tables(positions)
    cum_mask = jnp.zeros((1, GATE_LANES), F32).at[0, :fox_h].set(1.0).at[0, f_lane0:f_lane0 + ml_h].set(1.0)

    x2 = x.reshape(b * s, d)
    for layer in range(depth):
        sh_a, sc_a, g_a, sh_m, sc_m, g_m = [t.reshape(b, 1, d) for t in jnp.split(mod[layer], N_MOD, axis=-1)]
        wl = w_in[layer]
        w_main = jnp.concatenate([wl[:, offs[0]:offs[7]], wl[:, offs[8]:offs[12]]], axis=1).astype(BF16)
        w_gate = jnp.concatenate([wl[:, offs[7]:offs[8]], wl[:, offs[12]:offs[14]],
                                  jnp.zeros((d, GATE_LANES - n_gate), F32)], axis=1).astype(BF16)
        gate_bias = jnp.pad(jnp.concatenate([fox_f_bias[layer], mlstm_i_bias[layer], mlstm_f_bias[layer]]),
                            (0, GATE_LANES - n_gate)).reshape(1, GATE_LANES)
        ms = merge_scale[layer].reshape(1, d)

        proj, gates = _inproj(x2, norm_mix_w[layer].reshape(1, d), sc_a, sh_a, w_main, w_gate, s)
        proj = proj.reshape(b, s, -1)
        gp = _gates(gates.reshape(b, s, GATE_LANES), gate_bias, cum_mask)
        gp_t = jnp.swapaxes(gp[:, :, :n_gate], 1, 2)
        c_rows = gp_t[:, :fox_h].reshape(b * fox_h, 1, s)
        i_rows = gp_t[:, i_lane0:i_lane0 + ml_h]
        g_rows = gp_t[:, f_lane0:f_lane0 + ml_h]

        y_ret = _retention(proj, cos, sin, ms, ret_h, ret_col0, ret_chunk)
        k_aug = _fox_keys(proj, gp, fox_h, fox_col0 + fox_h, tk)
        y_fox = _fox(proj, k_aug, c_rows, ms, fox_h, fox_col0, ret_h, tq, tk, FOX_UNROLL if tq // tk % FOX_UNROLL == 0 else 1)
        y_m = _mlstm(proj, conv_w[layer], conv_b[layer].reshape(1, -1), gp, i_rows, g_rows, ms,
                     ml_h, ml_col0, ret_h + fox_h, i_lane0, f_lane0, ml_chunk)

        x2 = _outproj(y_ret.reshape(b * s, -1), y_fox.reshape(b * s, -1), y_m.reshape(b * s, -1),
                      w_out[layer].astype(BF16), x2, g_a, s)
        x2 = _mlp(x2, norm_mlp_w[layer].reshape(1, d), sc_m, sh_m, g_m,
                  w_ff1[layer].astype(BF16), w_ff2[layer].astype(BF16),
                  final_norm_w.reshape(1, d), s, layer == depth - 1)
    return x2.reshape(b, s, d)
```

```python
import functools
import math

import jax
import jax.numpy as jnp
from jax import lax
from jax.experimental import pallas as pl
from jax.experimental.pallas import tpu as pltpu

HEAD_DIM = 128
CONV_WIDTH = 4
ROPE_BASE = 10000.0
NORM_EPS = 1e-6
N_MOD = 6
GATE_LANES = 128
LOG2E = math.log2(math.e)
BIAS_PIECES = 3
CONV_HALO = 8

ADA_TN, ADA_KC = 1024, 256
ROPE_TS = 2048
INPROJ_TM, INPROJ_TN = 1024, (1792, 1024, 512, 256, 128)
GATES_TS = 512
RET_CHUNK, MLSTM_CHUNK = 256, 512
FOX_KEYS_TS = 1024
FOX_TQ, FOX_TK, FOX_UNROLL = 1024, 512, 2
OUTPROJ_TM = 512
MLP_TM, MLP_TF = 1024, 512

F32 = jnp.float32
BF16 = jnp.bfloat16

VMEM_LIMIT_BYTES = 56 * 1024 * 1024


def _params(*sem):
    return pltpu.CompilerParams(dimension_semantics=sem, vmem_limit_bytes=VMEM_LIMIT_BYTES)


def _dot(a, b):
    return jnp.dot(a, b, preferred_element_type=F32)


def _dot_nt(a, b):
    return lax.dot_general(a, b, (((1,), (1,)), ((), ())), preferred_element_type=F32)


def _dot_tn(a, b):
    return lax.dot_general(a, b, (((0,), (0,)), ((), ())), preferred_element_type=F32)


def _sigmoid(x):
    return 1.0 / (1.0 + jnp.exp(-x))


def _log_sigmoid(x):
    return jnp.minimum(x, 0.0) - jnp.log1p(jnp.exp(-jnp.abs(x)))


def _head_norm(y):
    mu = jnp.mean(y, axis=-1, keepdims=True)
    yc = y - mu
    var = jnp.mean(yc * yc, axis=-1, keepdims=True)
    return yc * lax.rsqrt(var + NORM_EPS)


def _ada_kernel(ct_ref, w_ref, b_ref, o_ref, *, kc):
    d, nb = ct_ref.shape
    tn = w_ref.shape[2]

    def body(i, accs):
        r0 = pl.multiple_of(i * kc, kc)
        ck = ct_ref[pl.ds(r0, kc), :]
        ck = ck * _sigmoid(ck)
        wk = w_ref[0, pl.ds(r0, kc), :]
        return tuple(acc + jnp.sum(wk * ck[:, b:b + 1], axis=0, keepdims=True)
                     for b, acc in enumerate(accs))

    accs = lax.fori_loop(0, d // kc, body, tuple(jnp.zeros((1, tn), F32) for _ in range(nb)))
    o_ref[0] = jnp.concatenate(accs, axis=0) + b_ref[0]


def _ada_mod(c, ada_w, ada_b):
    depth, d, n = ada_w.shape
    nb = c.shape[0]
    tn = min(ADA_TN, n)
    kc = min(ADA_KC, d)
    return pl.pallas_call(
        functools.partial(_ada_kernel, kc=kc),
        grid=(depth, n // tn),
        in_specs=[pl.BlockSpec((d, nb), lambda l, j: (0, 0)),
                  pl.BlockSpec((1, d, tn), lambda l, j: (l, 0, j)),
                  pl.BlockSpec((1, 1, tn), lambda l, j: (l, 0, j))],
        out_specs=pl.BlockSpec((1, nb, tn), lambda l, j: (l, 0, j)),
        out_shape=jax.ShapeDtypeStruct((depth, nb, n), F32),
        compiler_params=_params("arbitrary", "arbitrary"),
        name="ada_mod",
    )(c.T, ada_w, ada_b.reshape(depth, 1, n))


def _rope_kernel(pos_ref, invf_ref, sign_ref, cos_ref, sin_ref):
    ang = pos_ref[0] * invf_ref[...]
    cos_ref[0] = jnp.cos(ang)
    sin_ref[0] = jnp.sin(ang) * sign_ref[...]


def _rope_tables(positions):
    b, s = positions.shape
    ts = min(ROPE_TS, s)
    half = HEAD_DIM // 2
    inv_freq = ROPE_BASE ** (-jnp.arange(0, HEAD_DIM, 2, dtype=F32) / HEAD_DIM)
    invf = jnp.concatenate([inv_freq, inv_freq]).reshape(1, HEAD_DIM)
    sign = jnp.concatenate([-jnp.ones((half,), F32), jnp.ones((half,), F32)]).reshape(1, HEAD_DIM)
    pos = positions.astype(F32).reshape(b, s, 1)
    out = jax.ShapeDtypeStruct((b, s, HEAD_DIM), F32)
    return pl.pallas_call(
        _rope_kernel,
        grid=(b, s // ts),
        in_specs=[pl.BlockSpec((1, ts, 1), lambda i, j: (i, j, 0)),
                  pl.BlockSpec((1, HEAD_DIM), lambda i, j: (0, 0)),
                  pl.BlockSpec((1, HEAD_DIM), lambda i, j: (0, 0))],
        out_specs=[pl.BlockSpec((1, ts, HEAD_DIM), lambda i, j: (i, j, 0))] * 2,
        out_shape=[out, out],
        compiler_params=_params("arbitrary", "arbitrary"),
        name="rope_tables",
    )(pos, invf, sign)


def _modulated_norm_rows(x_ref, nw_ref, sc_ref, sh_ref, h_scr):
    x = x_ref[...]
    inv = lax.rsqrt(jnp.mean(x * x, axis=-1, keepdims=True) + NORM_EPS)
    h_scr[...] = ((x_ref[...] * inv * nw_ref[...]) * (1.0 + sc_ref[0]) + sh_ref[0]).astype(BF16)


def _inproj_kernel(x_ref, nw_ref, sc_ref, sh_ref, w_ref, wg_ref, o_ref, g_ref, h_scr):
    @pl.when(pl.program_id(1) == 0)
    def _():
        _modulated_norm_rows(x_ref, nw_ref, sc_ref, sh_ref, h_scr)
        g_ref[...] = _dot(h_scr[...], wg_ref[...])

    o_ref[...] = _dot(h_scr[...], w_ref[...]).astype(BF16)


def _inproj(x2, nw, sc, sh, w_main, w_gate, seq):
    m, d = x2.shape
    nm = w_main.shape[1]
    tm = min(INPROJ_TM, seq)
    tn = next(t for t in INPROJ_TN if nm % t == 0)
    bpr = seq // tm
    return pl.pallas_call(
        _inproj_kernel,
        grid=(m // tm, nm // tn),
        in_specs=[pl.BlockSpec((tm, d), lambda i, j: (i, 0)),
                  pl.BlockSpec((1, d), lambda i, j: (0, 0)),
                  pl.BlockSpec((1, 1, d), lambda i, j: (i // bpr, 0, 0)),
                  pl.BlockSpec((1, 1, d), lambda i, j: (i // bpr, 0, 0)),
                  pl.BlockSpec((d, tn), lambda i, j: (0, j)),
                  pl.BlockSpec((d, GATE_LANES), lambda i, j: (0, 0))],
        out_specs=[pl.BlockSpec((tm, tn), lambda i, j: (i, j)),
                   pl.BlockSpec((tm, GATE_LANES), lambda i, j: (i, 0))],
        out_shape=[jax.ShapeDtypeStruct((m, nm), BF16),
                   jax.ShapeDtypeStruct((m, GATE_LANES), F32)],
        scratch_shapes=[pltpu.VMEM((tm, d), BF16)],
        compiler_params=_params("arbitrary", "arbitrary"),
        name="inproj",
    )(x2, nw, sc, sh, w_main, w_gate)


def _gates_kernel(g_ref, bias_ref, cum_ref, o_ref, carry):
    @pl.when(pl.program_id(1) == 0)
    def _():
        carry[...] = jnp.zeros_like(carry)

    g = g_ref[0] + bias_ref[...]
    ls = _log_sigmoid(g)
    ts = ls.shape[0]
    row = lax.broadcasted_iota(jnp.int32, (ts, ts), 0)
    col = lax.broadcasted_iota(jnp.int32, (ts, ts), 1)
    tri = jnp.where(row >= col, 1.0, 0.0).astype(BF16)
    hi = ls.astype(BF16)
    r1 = ls - hi.astype(F32)
    mid = r1.astype(BF16)
    lo = (r1 - mid.astype(F32)).astype(BF16)
    csum = (_dot(tri, hi) + _dot(tri, mid)) + _dot(tri, lo) + carry[...]
    carry[...] = csum[ts - 1:ts, :]
    o_ref[0] = jnp.where(cum_ref[...] > 0.5, csum, g)


def _gates(g3, bias, cum_mask):
    b, s, n = g3.shape
    ts = min(GATES_TS, s)
    return pl.pallas_call(
        _gates_kernel,
        grid=(b, s // ts),
        in_specs=[pl.BlockSpec((1, ts, n), lambda i, j: (i, j, 0)),
                  pl.BlockSpec((1, n), lambda i, j: (0, 0)),
                  pl.BlockSpec((1, n), lambda i, j: (0, 0))],
        out_specs=pl.BlockSpec((1, ts, n), lambda i, j: (i, j, 0)),
        out_shape=jax.ShapeDtypeStruct((b, s, n), F32),
        scratch_shapes=[pltpu.VMEM((1, n), F32)],
        compiler_params=_params("arbitrary", "arbitrary"),
        name="gates",
    )(g3, bias, cum_mask)


def _retention_kernel(q_ref, k_ref, v_ref, g_ref, cos_ref, sin_ref, ms_ref, o_ref, state, decay, *, heads):
    n = cos_ref.shape[1]
    log_gamma = [math.log(1.0 - 2.0 ** (-5.0 - h)) for h in range(heads)]

    @pl.when(pl.program_id(1) == 0)
    def _():
        state[...] = jnp.zeros_like(state)
        row = lax.broadcasted_iota(jnp.int32, (n, n), 0)
        col = lax.broadcasted_iota(jnp.int32, (n, n), 1)
        rel = (row - col).astype(F32)
        for h in range(heads):
            decay[h] = jnp.where(rel >= 0, jnp.exp(jnp.maximum(rel, 0.0) * log_gamma[h]), 0.0)

    cos = cos_ref[0]
    sin = sin_ref[0]
    half = HEAD_DIM // 2
    idx = lax.broadcasted_iota(jnp.int32, (n, 1), 0).astype(F32)

    def rot(t):
        return t * cos + pltpu.roll(t, half, 1) * sin

    for h in range(heads):
        lg = log_gamma[h]
        cs = slice(h * HEAD_DIM, (h + 1) * HEAD_DIM)
        q = rot(q_ref[0, :, cs].astype(F32))
        k = rot(k_ref[0, :, cs].astype(F32)) * (HEAD_DIM ** -0.5)
        v = v_ref[0, :, cs]
        q_decay = jnp.exp((idx + 1.0) * lg)
        k_decay = jnp.exp((n - 1.0 - idx) * lg)
        chunk_decay = math.exp(n * lg)

        scores = _dot_nt(q.astype(BF16), k.astype(BF16)) * decay[h]
        intra = _dot(scores.astype(BF16), v)
        st = state[h]
        inter = _dot((q * q_decay).astype(BF16), st.astype(BF16))
        state[h] = st * chunk_decay + _dot_tn((k * k_decay).astype(BF16), v)

        y = _head_norm(intra + inter)
        gate = g_ref[0, :, cs].astype(F32)
        o_ref[0, :, cs] = (y * (gate * _sigmoid(gate)) * ms_ref[:, cs]).astype(BF16)


def _retention(proj, cos, sin, merge_scale, heads, col0, chunk):
    b, s, _ = proj.shape
    w = heads * HEAD_DIM
    assert (col0 * HEAD_DIM) % w == 0
    blk0 = col0 * HEAD_DIM // w
    grp = lambda base: pl.BlockSpec((1, chunk, w), lambda i, c: (i, c, blk0 + base))
    tok = pl.BlockSpec((1, chunk, HEAD_DIM), lambda i, c: (i, c, 0))
    return pl.pallas_call(
        functools.partial(_retention_kernel, heads=heads),
        grid=(b, s // chunk),
        in_specs=[grp(0), grp(1), grp(2), grp(3), tok, tok,
                  pl.BlockSpec((1, w), lambda i, c: (0, 0))],
        out_specs=pl.BlockSpec((1, chunk, w), lambda i, c: (i, c, 0)),
        out_shape=jax.ShapeDtypeStruct((b, s, w), BF16),
        scratch_shapes=[pltpu.VMEM((heads, HEAD_DIM, HEAD_DIM), F32),
                        pltpu.VMEM((heads, chunk, chunk), F32)],
        compiler_params=_params("arbitrary", "arbitrary"),
        name="retention",
    )(proj, proj, proj, proj, cos, sin, merge_scale)


def _fox_keys_kernel(k_ref, gp_ref, o_ref, *, tk, heads):
    ts = k_ref.shape[1]
    c = gp_ref[0]
    parts = []
    for i in range(ts // tk):
        blk = c[i * tk:(i + 1) * tk]
        parts.append((blk[0:1, :] - blk) * LOG2E)
    x = jnp.concatenate(parts, axis=0) if len(parts) > 1 else parts[0]
    hi = x.astype(BF16)
    r1 = x - hi.astype(F32)
    mid = r1.astype(BF16)
    lo = (r1 - mid.astype(F32)).astype(BF16)
    row = lax.broadcasted_iota(jnp.int32, (GATE_LANES, HEAD_DIM), 0)
    col = lax.broadcasted_iota(jnp.int32, (GATE_LANES, HEAD_DIM), 1)
    for h in range(heads):
        sel = lambda piece: jnp.where((row == h) & (col == piece), 1.0, 0.0).astype(BF16)
        bias = _dot(hi, sel(0)) + _dot(mid, sel(1)) + _dot(lo, sel(2))
        o_ref[0, h] = jnp.concatenate([k_ref[0, :, h * HEAD_DIM:(h + 1) * HEAD_DIM], bias.astype(BF16)], axis=1)


def _fox_keys(proj, gp, heads, k_col0, tk):
    b, s, _ = proj.shape
    ts = min(FOX_KEYS_TS, s)
    w = heads * HEAD_DIM
    assert (k_col0 * HEAD_DIM) % w == 0 and ts % tk == 0
    return pl.pallas_call(
        functools.partial(_fox_keys_kernel, tk=tk, heads=heads),
        grid=(b, s // ts),
        in_specs=[pl.BlockSpec((1, ts, w), lambda i, j: (i, j, k_col0 * HEAD_DIM // w)),
                  pl.BlockSpec((1, ts, GATE_LANES), lambda i, j: (i, j, 0))],
        out_specs=pl.BlockSpec((1, heads, ts, 2 * HEAD_DIM), lambda i, j: (i, 0, j, 0)),
        out_shape=jax.ShapeDtypeStruct((b, heads, s, 2 * HEAD_DIM), BF16),
        compiler_params=_params("arbitrary", "arbitrary"),
        name="fox_keys",
    )(proj, gp)


def _fox_query_block(qi, nq, q_ref, k_ref, v_ref, c_ref, ms_ref, o_ref, s_scr, m_scr, l_scr, acc_scr,
                     *, tq, tk, unroll):
    ndiag = tq // tk
    static_slots = unroll % 2 == 0 and ndiag % 2 == 0
    nfull = qi * ndiag
    q0 = pl.multiple_of(qi * tq, tq)

    def augmented_queries(start):
        q_t = (q_ref[0, pl.ds(start, tq), :].astype(F32) * (HEAD_DIM ** -0.5 * LOG2E)).T.astype(BF16)
        row = lax.broadcasted_iota(jnp.int32, (HEAD_DIM, tq), 0)
        return jnp.concatenate([q_t, jnp.where(row < BIAS_PIECES, 1.0, 0.0).astype(BF16)], axis=0)

    q_aug = augmented_queries(q0)
    c_first = c_ref[0, :, pl.ds(q0, HEAD_DIM)][:, 0:1]

    m_scr[...] = jnp.full_like(m_scr, -jnp.inf)
    l_scr[...] = jnp.zeros_like(l_scr)
    acc_scr[...] = jnp.zeros_like(acc_scr)

    def scores(j, slot, c0):
        k0 = pl.multiple_of(j * tk, tk)
        s_scr[slot, :, c0:] = _dot(k_ref[0, 0, pl.ds(k0, tk), :], q_aug[:, c0:])

    def softmax(j, slot, diag):
        k0 = pl.multiple_of(j * tk, tk)
        c0 = 0 if diag is None else diag * tk
        delta = (c_first - c_ref[0, :, pl.ds(k0, HEAD_DIM)][:, 0:1]) * LOG2E
        t = s_scr[slot, :, c0:]
        if diag is not None:
            key = lax.broadcasted_iota(jnp.int32, t.shape, 0)
            qry = lax.broadcasted_iota(jnp.int32, t.shape, 1)
            t = jnp.where(qry >= key, t, -jnp.inf)
        m_prev = m_scr[:, c0:]
        m_new = jnp.maximum(m_prev, jnp.max(t, axis=0, keepdims=True) + delta)
        p = jnp.exp2(t - (m_new - delta)).astype(BF16)
        alpha = jnp.exp2(m_prev - m_new)
        half = p[:tk // 2] + p[tk // 2:]
        quarter = half[:tk // 4] + half[tk // 4:]
        l_scr[:, c0:] = alpha * l_scr[:, c0:] + jnp.sum(quarter.astype(F32), axis=0, keepdims=True)
        m_scr[:, c0:] = m_new
        acc_scr[:, c0:] = alpha * acc_scr[:, c0:] + _dot_tn(v_ref[0, pl.ds(k0, tk), :], p)

    def step(j, slot, diag, last):
        if not last:
            scores(j + 1, 1 - slot, 0 if diag is None else (diag + 1) * tk)
        softmax(j, slot, diag)

    @pl.when(qi == 0)
    def _():
        scores(0, 0, 0)

    def body(i, carry):
        for u in range(unroll):
            j = i * unroll + u
            step(j, u % 2 if static_slots else j & 1, None, False)
        return carry

    lax.fori_loop(0, nfull // unroll, body, 0)
    for d in range(ndiag):
        j = nfull + d
        slot = d % 2 if ndiag % 2 == 0 else j & 1
        step(j, slot, d, d == ndiag - 1)

    def finalize():
        out = (acc_scr[...] / l_scr[...]).T
        o_ref[0, pl.ds(q0, tq), :] = (_head_norm(out) * ms_ref[...]).astype(BF16)

    @pl.when(qi + 1 < nq)
    def _():
        s_scr[0] = _dot(k_ref[0, 0, 0:tk, :], augmented_queries(pl.multiple_of(q0 + tq, tq)))
        finalize()

    @pl.when(qi + 1 == nq)
    def _():
        finalize()


def _fox_kernel(q_ref, *refs, tq, tk, unroll):
    nq = q_ref.shape[1] // tq

    def body(qi, carry):
        _fox_query_block(qi, nq, q_ref, *refs, tq=tq, tk=tk, unroll=unroll)
        return carry

    lax.fori_loop(0, nq, body, 0)


def _fox(proj, k_aug, c_rows, merge_scale, heads, col0, ms_col0, tq, tk, unroll):
    b, s, _ = proj.shape
    assert tq % tk == 0 and (tq // tk) % unroll == 0
    return pl.pallas_call(
        functools.partial(_fox_kernel, tq=tq, tk=tk, unroll=unroll),
        grid=(b, heads),
        in_specs=[pl.BlockSpec((1, s, HEAD_DIM), lambda i, h: (i, 0, col0 + h)),
                  pl.BlockSpec((1, 1, s, 2 * HEAD_DIM), lambda i, h: (i, h, 0, 0)),
                  pl.BlockSpec((1, s, HEAD_DIM), lambda i, h: (i, 0, col0 + 2 * heads + h)),
                  pl.BlockSpec((1, 1, s), lambda i, h: (i * heads + h, 0, 0)),
                  pl.BlockSpec((1, HEAD_DIM), lambda i, h: (0, ms_col0 + h))],
        out_specs=pl.BlockSpec((1, s, HEAD_DIM), lambda i, h: (i, 0, h)),
        out_shape=jax.ShapeDtypeStruct((b, s, heads * HEAD_DIM), BF16),
        scratch_shapes=[pltpu.VMEM((2, tk, tq), F32),
                        pltpu.VMEM((1, tq), F32),
                        pltpu.VMEM((1, tq), F32),
                        pltpu.VMEM((HEAD_DIM, tq), F32)],
        compiler_params=_params("arbitrary", "arbitrary"),
        name="fox_attention",
    )(proj, k_aug, proj, c_rows, merge_scale)


def _mlstm_kernel(q_ref, k_ref, v_ref, og_ref, w_ref, b_ref, gp_ref, ir_ref, gr_ref, ms_ref, o_ref,
                  c_scr, n_scr, m_scr, g0_scr, xq_scr, xk_scr, *, heads, i_lane0, f_lane0):
    n = q_ref.shape[1]
    w = heads * HEAD_DIM

    @pl.when(pl.program_id(1) == 0)
    def _():
        c_scr[...] = jnp.zeros_like(c_scr)
        n_scr[...] = jnp.zeros_like(n_scr)
        m_scr[...] = jnp.zeros_like(m_scr)
        g0_scr[...] = jnp.zeros_like(g0_scr)
        xq_scr[0:CONV_HALO, :] = jnp.zeros((CONV_HALO, w), F32)
        xk_scr[0:CONV_HALO, :] = jnp.zeros((CONV_HALO, w), F32)

    def conv_silu(x_ref, x_scr, cols):
        x_scr[CONV_HALO:CONV_HALO + n, :] = x_ref[0].astype(F32)
        y = b_ref[:, cols] + jnp.zeros((n, w), F32)
        for j in range(CONV_WIDTH):
            shift = CONV_WIDTH - 1 - j
            y = y + w_ref[j:j + 1, cols] * x_scr[CONV_HALO - shift:CONV_HALO - shift + n, :]
        x_scr[0:CONV_HALO, :] = x_scr[n:n + CONV_HALO, :]
        return y * _sigmoid(y)

    q_all = conv_silu(q_ref, xq_scr, slice(0, w))
    k_all = conv_silu(k_ref, xk_scr, slice(w, 2 * w)) * (HEAD_DIM ** -0.5)
    gp = gp_ref[0]
    row = lax.broadcasted_iota(jnp.int32, (n, n), 0)
    col = lax.broadcasted_iota(jnp.int32, (n, n), 1)
    causal = row >= col

    for h in range(heads):
        cs = slice(h * HEAD_DIM, (h + 1) * HEAD_DIM)
        q = q_all[:, cs]
        k = k_all[:, cs]
        v = v_ref[0, :, cs]
        ic = gp[:, i_lane0 + h:i_lane0 + h + 1]
        gc = gp[:, f_lane0 + h:f_lane0 + h + 1]
        ir = ir_ref[0, h:h + 1, :]
        gr = gr_ref[0, h:h + 1, :]

        g0 = g0_scr[h][:, 0:1]
        m_prev = m_scr[h][:, 0:1]
        b_col = gc - g0
        dmat = jnp.where(causal, gc + (ir - gr), -jnp.inf)
        inter_log = b_col + m_prev
        m_q = jnp.maximum(inter_log, jnp.max(dmat, axis=-1, keepdims=True))
        w_intra = jnp.exp(dmat - m_q)
        w_inter = jnp.exp(inter_log - m_q)

        qb = q.astype(BF16)
        s = _dot_nt(qb, k.astype(BF16)) * w_intra
        cst = c_scr[h]
        nst = n_scr[h]
        num = _dot(s.astype(BF16), v) + w_inter * _dot(qb, cst.astype(BF16))
        den = jnp.sum(s, axis=-1, keepdims=True) + w_inter * jnp.sum(q * nst, axis=-1, keepdims=True)
        hid = num / jnp.maximum(jnp.abs(den), jnp.exp(-m_q))

        b_last = b_col[n - 1:n, :]
        k_log = b_last - b_col + ic
        m_new = jnp.maximum(b_last + m_prev, jnp.max(k_log, axis=0, keepdims=True))
        wk = jnp.exp(k_log - m_new)
        carry_scale = jnp.exp(b_last + m_prev - m_new)
        kw = k * wk
        c_scr[h] = carry_scale * cst + _dot_tn(kw.astype(BF16), v)
        n_scr[h] = carry_scale * nst + jnp.sum(kw, axis=0, keepdims=True)
        m_scr[h] = jnp.broadcast_to(m_new, (1, HEAD_DIM))
        g0_scr[h] = jnp.broadcast_to(gc[n - 1:n, :], (1, HEAD_DIM))

        og = og_ref[0, :, cs].astype(F32)
        o_ref[0, :, cs] = (_head_norm(hid) * _sigmoid(og) * ms_ref[:, cs]).astype(BF16)


def _mlstm(proj, conv_w, conv_b, gp, i_rows, g_rows, merge_scale, heads, col0, ms_col0, i_lane0, f_lane0, chunk):
    b, s, _ = proj.shape
    w = heads * HEAD_DIM
    assert (col0 * HEAD_DIM) % w == 0 and (ms_col0 * HEAD_DIM) % w == 0
    blk0 = col0 * HEAD_DIM // w
    grp = lambda base: pl.BlockSpec((1, chunk, w), lambda i, c: (i, c, blk0 + base))
    row_spec = pl.BlockSpec((1, heads, chunk), lambda i, c: (i, 0, c))
    state = lambda r, c: pltpu.VMEM((heads, r, c), F32)
    return pl.pallas_call(
        functools.partial(_mlstm_kernel, heads=heads, i_lane0=i_lane0, f_lane0=f_lane0),
        grid=(b, s // chunk),
        in_specs=[grp(0), grp(1), grp(2), grp(3),
                  pl.BlockSpec(conv_w.shape, lambda i, c: (0, 0)),
                  pl.BlockSpec(conv_b.shape, lambda i, c: (0, 0)),
                  pl.BlockSpec((1, chunk, GATE_LANES), lambda i, c: (i, c, 0)),
                  row_spec, row_spec,
                  pl.BlockSpec((1, w), lambda i, c: (0, ms_col0 * HEAD_DIM // w))],
        out_specs=pl.BlockSpec((1, chunk, w), lambda i, c: (i, c, 0)),
        out_shape=jax.ShapeDtypeStruct((b, s, w), BF16),
        scratch_shapes=[state(HEAD_DIM, HEAD_DIM), state(1, HEAD_DIM), state(1, HEAD_DIM), state(1, HEAD_DIM),
                        pltpu.VMEM((chunk + CONV_HALO, w), F32),
                        pltpu.VMEM((chunk + CONV_HALO, w), F32)],
        compiler_params=_params("arbitrary", "arbitrary"),
        name="mlstm",
    )(proj, proj, proj, proj, conv_w, conv_b, gp, i_rows, g_rows, merge_scale)


def _outproj_kernel(yr_ref, yf_ref, ym_ref, w_ref, x_ref, ga_ref, o_ref):
    r = yr_ref.shape[1]
    f = yf_ref.shape[1]
    y = _dot(yr_ref[...], w_ref[0:r, :]) + _dot(yf_ref[...], w_ref[r:r + f, :])
    y = y + _dot(ym_ref[...], w_ref[r + f:, :])
    o_ref[...] = x_ref[...] + ga_ref[0] * y


def _outproj(y_ret, y_fox, y_m, w_out, x2, g_a, seq):
    m, d = x2.shape
    tm = min(OUTPROJ_TM, seq)
    bpr = seq // tm
    rows = lambda w: pl.BlockSpec((tm, w), lambda i: (i, 0))
    return pl.pallas_call(
        _outproj_kernel,
        grid=(m // tm,),
        in_specs=[rows(y_ret.shape[1]), rows(y_fox.shape[1]), rows(y_m.shape[1]),
                  pl.BlockSpec(w_out.shape, lambda i: (0, 0)),
                  rows(d),
                  pl.BlockSpec((1, 1, d), lambda i: (i // bpr, 0, 0))],
        out_specs=rows(d),
        out_shape=jax.ShapeDtypeStruct((m, d), F32),
        compiler_params=_params("arbitrary"),
        name="outproj",
    )(y_ret, y_fox, y_m, w_out, x2, g_a)


def _mlp_kernel(x_ref, nw_ref, sc_ref, sh_ref, gm_ref, w1_hbm, w2_hbm, fw_ref, o_ref,
                w1_buf, w2_buf, sem, h_scr, *, final_norm, tf, nf):
    i = pl.program_id(0)
    n = pl.num_programs(0)

    def tile_copies(f, slot):
        c0 = pl.multiple_of(f * tf, tf)
        return (pltpu.make_async_copy(w1_hbm.at[:, pl.ds(c0, tf)], w1_buf.at[slot], sem.at[0, slot]),
                pltpu.make_async_copy(w2_hbm.at[pl.ds(c0, tf), :], w2_buf.at[slot], sem.at[1, slot]))

    @pl.when(i == 0)
    def _():
        for cp in tile_copies(0, 0):
            cp.start()

    _modulated_norm_rows(x_ref, nw_ref, sc_ref, sh_ref, h_scr)
    o_ref[...] = jnp.zeros_like(o_ref)

    def body(f, carry):
        slot = f % 2

        @pl.when((f + 1 < nf) | (i + 1 < n))
        def _():
            for cp in tile_copies((f + 1) % nf, 1 - slot):
                cp.start()

        for cp in tile_copies(f, slot):
            cp.wait()
        a = jnp.maximum(_dot(h_scr[...], w1_buf[slot]), 0.0)
        o_ref[...] += _dot((a * a).astype(BF16), w2_buf[slot])
        return carry

    lax.fori_loop(0, nf, body, 0)

    y = x_ref[...] + gm_ref[0] * o_ref[...]
    if final_norm:
        ms = jnp.mean(y * y, axis=-1, keepdims=True)
        y = y * lax.rsqrt(ms + NORM_EPS) * fw_ref[...]
    o_ref[...] = y


def _mlp(x2, nw, sc, sh, g_m, w1, w2, final_w, seq, final_norm):
    m, d = x2.shape
    dff = w1.shape[1]
    tm = min(MLP_TM, seq)
    tf = min(MLP_TF, dff)
    nf = dff // tf
    assert nf % 2 == 0
    bpr = seq // tm
    mod = pl.BlockSpec((1, 1, d), lambda i: (i // bpr, 0, 0))
    vec = pl.BlockSpec((1, d), lambda i: (0, 0))
    hbm = pl.BlockSpec(memory_space=pl.ANY)
    return pl.pallas_call(
        functools.partial(_mlp_kernel, final_norm=final_norm, tf=tf, nf=nf),
        grid=(m // tm,),
        in_specs=[pl.BlockSpec((tm, d), lambda i: (i, 0)),
                  vec, mod, mod, mod, hbm, hbm, vec],
        out_specs=pl.BlockSpec((tm, d), lambda i: (i, 0)),
        out_shape=jax.ShapeDtypeStruct((m, d), F32),
        scratch_shapes=[pltpu.VMEM((2, d, tf), BF16),
                        pltpu.VMEM((2, tf, d), BF16),
                        pltpu.SemaphoreType.DMA((2, 2)),
                        pltpu.VMEM((tm, d), BF16)],
        compiler_params=_params("arbitrary"),
        name="mlp",
    )(x2, nw, sc, sh, g_m, w1, w2, final_w)


def kernel(x, c, positions, ada_w, ada_b, norm_mix_w, norm_mlp_w, w_in, conv_w, conv_b, fox_f_bias,
           mlstm_i_bias, mlstm_f_bias, merge_scale, w_out, w_ff1, w_ff2, final_norm_w):
    b, s, d = x.shape
    depth = ada_w.shape[0]
    ret_w, fox_w, ml_w = d // 4, d // 2, d // 4
    ret_h, fox_h, ml_h = ret_w // HEAD_DIM, fox_w // HEAD_DIM, ml_w // HEAD_DIM
    assert s % 128 == 0 and d % (4 * HEAD_DIM) == 0
    assert fox_h + 2 * ml_h <= GATE_LANES

    sizes = [ret_w] * 4 + [fox_w] * 3 + [fox_h] + [ml_w] * 4 + [ml_h, ml_h]
    offs = [0]
    for sz in sizes:
        offs.append(offs[-1] + sz)
    n_gate = fox_h + 2 * ml_h
    ret_col0 = 0
    fox_col0 = (4 * ret_w) // HEAD_DIM
    ml_col0 = (4 * ret_w + 3 * fox_w) // HEAD_DIM
    i_lane0, f_lane0 = fox_h, fox_h + ml_h

    ret_chunk = min(RET_CHUNK, s)
    ml_chunk = min(MLSTM_CHUNK, s)
    tq = min(FOX_TQ, s)
    tk = min(FOX_TK, tq)

    mod = _ada_mod(c, ada_w, ada_b)
    cos, sin = _rope_tables(positions)
    cum_mask = jnp.zeros((1, GATE_LANES), F32).at[0, :fox_h].set(1.0).at[0, f_lane0:f_lane0 + ml_h].set(1.0)

    x2 = x.reshape(b * s, d)
    for layer in range(depth):
        sh_a, sc_a, g_a, sh_m, sc_m, g_m = [t.reshape(b, 1, d) for t in jnp.split(mod[layer], N_MOD, axis=-1)]
        wl = w_in[layer]
        w_main = jnp.concatenate([wl[:, offs[0]:offs[7]], wl[:, offs[8]:offs[12]]], axis=1).astype(BF16)
        w_gate = jnp.concatenate([wl[:, offs[7]:offs[8]], wl[:, offs[12]:offs[14]],
                                  jnp.zeros((d, GATE_LANES - n_gate), F32)], axis=1).astype(BF16)
        gate_bias = jnp.pad(jnp.concatenate([fox_f_bias[layer], mlstm_i_bias[layer], mlstm_f_bias[layer]]),
                            (0, GATE_LANES - n_gate)).reshape(1, GATE_LANES)
        ms = merge_scale[layer].reshape(1, d)

        proj, gates = _inproj(x2, norm_mix_w[layer].reshape(1, d), sc_a, sh_a, w_main, w_gate, s)
        proj = proj.reshape(b, s, -1)
        gp = _gates(gates.reshape(b, s, GATE_LANES), gate_bias, cum_mask)
        gp_t = jnp.swapaxes(gp[:, :, :n_gate], 1, 2)
        c_rows = gp_t[:, :fox_h].reshape(b * fox_h, 1, s)
        i_rows = gp_t[:, i_lane0:i_lane0 + ml_h]
        g_rows = gp_t[:, f_lane0:f_lane0 + ml_h]

        y_ret = _retention(proj, cos, sin, ms, ret_h, ret_col0, ret_chunk)
        k_aug = _fox_keys(proj, gp, fox_h, fox_col0 + fox_h, tk)
        y_fox = _fox(proj, k_aug, c_rows, ms, fox_h, fox_col0, ret_h, tq, tk, FOX_UNROLL if tq // tk % FOX_UNROLL == 0 else 1)
        y_m = _mlstm(proj, conv_w[layer], conv_b[layer].reshape(1, -1), gp, i_rows, g_rows, ms,
                     ml_h, ml_col0, ret_h + fox_h, i_lane0, f_lane0, ml_chunk)

        x2 = _outproj(y_ret.reshape(b * s, -1), y_fox.reshape(b * s, -1), y_m.reshape(b * s, -1),
                      w_out[layer].astype(BF16), x2, g_a, s)
        x2 = _mlp(x2, norm_mlp_w[layer].reshape(1, d), sc_m, sh_m, g_m,
                  w_ff1[layer].astype(BF16), w_ff2[layer].astype(BF16),
                  final_norm_w.reshape(1, d), s, layer == depth - 1)
    return x2.reshape(b, s, d)
```

```python
import functools
import math

import jax
import jax.numpy as jnp
from jax import lax
from jax.experimental import pallas as pl
from jax.experimental.pallas import tpu as pltpu

HEAD_DIM = 128
CONV_WIDTH = 4
ROPE_BASE = 10000.0
NORM_EPS = 1e-6
N_MOD = 6
GATE_LANES = 128
LOG2E = math.log2(math.e)
BIAS_PIECES = 3
CONV_HALO = 8

ADA_TN, ADA_KC = 1024, 256
ROPE_TS = 2048
INPROJ_TM, INPROJ_TN = 1024, (1792, 1024, 512, 256, 128)
GATES_TS = 512
RET_CHUNK, MLSTM_CHUNK = 256, 512
FOX_KEYS_TS = 1024
FOX_TQ, FOX_TK, FOX_UNROLL = 1024, 512, 2
OUTPROJ_TM = 512
MLP_TM, MLP_TF = 1024, 512

F32 = jnp.float32
BF16 = jnp.bfloat16

VMEM_LIMIT_BYTES = 56 * 1024 * 1024


def _params(*sem):
    return pltpu.CompilerParams(dimension_semantics=sem, vmem_limit_bytes=VMEM_LIMIT_BYTES)


def _dot(a, b):
    return jnp.dot(a, b, preferred_element_type=F32)


def _dot_nt(a, b):
    return lax.dot_general(a, b, (((1,), (1,)), ((), ())), preferred_element_type=F32)


def _dot_tn(a, b):
    return lax.dot_general(a, b, (((0,), (0,)), ((), ())), preferred_element_type=F32)


def _sigmoid(x):
    return 1.0 / (1.0 + jnp.exp(-x))


def _log_sigmoid(x):
    return jnp.minimum(x, 0.0) - jnp.log1p(jnp.exp(-jnp.abs(x)))


def _head_norm(y):
    mu = jnp.mean(y, axis=-1, keepdims=True)
    yc = y - mu
    var = jnp.mean(yc * yc, axis=-1, keepdims=True)
    return yc * lax.rsqrt(var + NORM_EPS)


def _ada_kernel(ct_ref, w_ref, b_ref, o_ref, *, kc):
    d, nb = ct_ref.shape
    tn = w_ref.shape[2]

    def body(i, accs):
        r0 = pl.multiple_of(i * kc, kc)
        ck = ct_ref[pl.ds(r0, kc), :]
        ck = ck * _sigmoid(ck)
        wk = w_ref[0, pl.ds(r0, kc), :]
        return tuple(acc + jnp.sum(wk * ck[:, b:b + 1], axis=0, keepdims=True)
                     for b, acc in enumerate(accs))

    accs = lax.fori_loop(0, d // kc, body, tuple(jnp.zeros((1, tn), F32) for _ in range(nb)))
    o_ref[0] = jnp.concatenate(accs, axis=0) + b_ref[0]


def _ada_mod(c, ada_w, ada_b):
    depth, d, n = ada_w.shape
    nb = c.shape[0]
    tn = min(ADA_TN, n)
    kc = min(ADA_KC, d)
    return pl.pallas_call(
        functools.partial(_ada_kernel, kc=kc),
        grid=(depth, n // tn),
        in_specs=[pl.BlockSpec((d, nb), lambda l, j: (0, 0)),
                  pl.BlockSpec((1, d, tn), lambda l, j: (l, 0, j)),
                  pl.BlockSpec((1, 1, tn), lambda l, j: (l, 0, j))],
        out_specs=pl.BlockSpec((1, nb, tn), lambda l, j: (l, 0, j)),
        out_shape=jax.ShapeDtypeStruct((depth, nb, n), F32),
        compiler_params=_params("arbitrary", "arbitrary"),
        name="ada_mod",
    )(c.T, ada_w, ada_b.reshape(depth, 1, n))


def _rope_kernel(pos_ref, invf_ref, sign_ref, cos_ref, sin_ref):
    ang = pos_ref[0] * invf_ref[...]
    cos_ref[0] = jnp.cos(ang)
    sin_ref[0] = jnp.sin(ang) * sign_ref[...]


def _rope_tables(positions):
    b, s = positions.shape
    ts = min(ROPE_TS, s)
    half = HEAD_DIM // 2
    inv_freq = ROPE_BASE ** (-jnp.arange(0, HEAD_DIM, 2, dtype=F32) / HEAD_DIM)
    invf = jnp.concatenate([inv_freq, inv_freq]).reshape(1, HEAD_DIM)
    sign = jnp.concatenate([-jnp.ones((half,), F32), jnp.ones((half,), F32)]).reshape(1, HEAD_DIM)
    pos = positions.astype(F32).reshape(b, s, 1)
    out = jax.ShapeDtypeStruct((b, s, HEAD_DIM), F32)
    return pl.pallas_call(
        _rope_kernel,
        grid=(b, s // ts),
        in_specs=[pl.BlockSpec((1, ts, 1), lambda i, j: (i, j, 0)),
                  pl.BlockSpec((1, HEAD_DIM), lambda i, j: (0, 0)),
                  pl.BlockSpec((1, HEAD_DIM), lambda i, j: (0, 0))],
        out_specs=[pl.BlockSpec((1, ts, HEAD_DIM), lambda i, j: (i, j, 0))] * 2,
        out_shape=[out, out],
        compiler_params=_params("arbitrary", "arbitrary"),
        name="rope_tables",
    )(pos, invf, sign)


def _modulated_norm_rows(x_ref, nw_ref, sc_ref, sh_ref, h_scr):
    x = x_ref[...]
    inv = lax.rsqrt(jnp.mean(x * x, axis=-1, keepdims=True) + NORM_EPS)
    h_scr[...] = ((x_ref[...] * inv * nw_ref[...]) * (1.0 + sc_ref[0]) + sh_ref[0]).astype(BF16)


def _inproj_kernel(x_ref, nw_ref, sc_ref, sh_ref, w_ref, wg_ref, o_ref, g_ref, h_scr):
    @pl.when(pl.program_id(1) == 0)
    def _():
        _modulated_norm_rows(x_ref, nw_ref, sc_ref, sh_ref, h_scr)
        g_ref[...] = _dot(h_scr[...], wg_ref[...])

    o_ref[...] = _dot(h_scr[...], w_ref[...]).astype(BF16)


def _inproj(x2, nw, sc, sh, w_main, w_gate, seq):
    m, d = x2.shape
    nm = w_main.shape[1]
    tm = min(INPROJ_TM, seq)
    tn = next(t for t in INPROJ_TN if nm % t == 0)
    bpr = seq // tm
    return pl.pallas_call(
        _inproj_kernel,
        grid=(m // tm, nm // tn),
        in_specs=[pl.BlockSpec((tm, d), lambda i, j: (i, 0)),
                  pl.BlockSpec((1, d), lambda i, j: (0, 0)),
                  pl.BlockSpec((1, 1, d), lambda i, j: (i // bpr, 0, 0)),
                  pl.BlockSpec((1, 1, d), lambda i, j: (i // bpr, 0, 0)),
                  pl.BlockSpec((d, tn), lambda i, j: (0, j)),
                  pl.BlockSpec((d, GATE_LANES), lambda i, j: (0, 0))],
        out_specs=[pl.BlockSpec((tm, tn), lambda i, j: (i, j)),
                   pl.BlockSpec((tm, GATE_LANES), lambda i, j: (i, 0))],
        out_shape=[jax.ShapeDtypeStruct((m, nm), BF16),
                   jax.ShapeDtypeStruct((m, GATE_LANES), F32)],
        scratch_shapes=[pltpu.VMEM((tm, d), BF16)],
        compiler_params=_params("arbitrary", "arbitrary"),
        name="inproj",
    )(x2, nw, sc, sh, w_main, w_gate)


def _gates_kernel(g_ref, bias_ref, cum_ref, o_ref, carry):
    @pl.when(pl.program_id(1) == 0)
    def _():
        carry[...] = jnp.zeros_like(carry)

    g = g_ref[0] + bias_ref[...]
    ls = _log_sigmoid(g)
    ts = ls.shape[0]
    row = lax.broadcasted_iota(jnp.int32, (ts, ts), 0)
    col = lax.broadcasted_iota(jnp.int32, (ts, ts), 1)
    tri = jnp.where(row >= col, 1.0, 0.0).astype(BF16)
    hi = ls.astype(BF16)
    r1 = ls - hi.astype(F32)
    mid = r1.astype(BF16)
    lo = (r1 - mid.astype(F32)).astype(BF16)
    csum = (_dot(tri, hi) + _dot(tri, mid)) + _dot(tri, lo) + carry[...]
    carry[...] = csum[ts - 1:ts, :]
    o_ref[0] = jnp.where(cum_ref[...] > 0.5, csum, g)


def _gates(g3, bias, cum_mask):
    b, s, n = g3.shape
    ts = min(GATES_TS, s)
    return pl.pallas_call(
        _gates_kernel,
        grid=(b, s // ts),
        in_specs=[pl.BlockSpec((1, ts, n), lambda i, j: (i, j, 0)),
                  pl.BlockSpec((1, n), lambda i, j: (0, 0)),
                  pl.BlockSpec((1, n), lambda i, j: (0, 0))],
        out_specs=pl.BlockSpec((1, ts, n), lambda i, j: (i, j, 0)),
        out_shape=jax.ShapeDtypeStruct((b, s, n), F32),
        scratch_shapes=[pltpu.VMEM((1, n), F32)],
        compiler_params=_params("arbitrary", "arbitrary"),
        name="gates",
    )(g3, bias, cum_mask)


def _retention_kernel(q_ref, k_ref, v_ref, g_ref, cos_ref, sin_ref, ms_ref, o_ref, state, decay, *, heads):
    n = cos_ref.shape[1]
    log_gamma = [math.log(1.0 - 2.0 ** (-5.0 - h)) for h in range(heads)]

    @pl.when(pl.program_id(1) == 0)
    def _():
        state[...] = jnp.zeros_like(state)
        row = lax.broadcasted_iota(jnp.int32, (n, n), 0)
        col = lax.broadcasted_iota(jnp.int32, (n, n), 1)
        rel = (row - col).astype(F32)
        for h in range(heads):
            decay[h] = jnp.where(rel >= 0, jnp.exp(jnp.maximum(rel, 0.0) * log_gamma[h]), 0.0)

    cos = cos_ref[0]
    sin = sin_ref[0]
    half = HEAD_DIM // 2
    idx = lax.broadcasted_iota(jnp.int32, (n, 1), 0).astype(F32)

    def rot(t):
        return t * cos + pltpu.roll(t, half, 1) * sin

    for h in range(heads):
        lg = log_gamma[h]
        cs = slice(h * HEAD_DIM, (h + 1) * HEAD_DIM)
        q = rot(q_ref[0, :, cs].astype(F32))
        k = rot(k_ref[0, :, cs].astype(F32)) * (HEAD_DIM ** -0.5)
        v = v_ref[0, :, cs]
        q_decay = jnp.exp((idx + 1.0) * lg)
        k_decay = jnp.exp((n - 1.0 - idx) * lg)
        chunk_decay = math.exp(n * lg)

        scores = _dot_nt(q.astype(BF16), k.astype(BF16)) * decay[h]
        intra = _dot(scores.astype(BF16), v)
        st = state[h]
        inter = _dot((q * q_decay).astype(BF16), st.astype(BF16))
        state[h] = st * chunk_decay + _dot_tn((k * k_decay).astype(BF16), v)

        y = _head_norm(intra + inter)
        gate = g_ref[0, :, cs].astype(F32)
        o_ref[0, :, cs] = (y * (gate * _sigmoid(gate)) * ms_ref[:, cs]).astype(BF16)


def _retention(proj, cos, sin, merge_scale, heads, col0, chunk):
    b, s, _ = proj.shape
    w = heads * HEAD_DIM
    assert (col0 * HEAD_DIM) % w == 0
    blk0 = col0 * HEAD_DIM // w
    grp = lambda base: pl.BlockSpec((1, chunk, w), lambda i, c: (i, c, blk0 + base))
    tok = pl.BlockSpec((1, chunk, HEAD_DIM), lambda i, c: (i, c, 0))
    return pl.pallas_call(
        functools.partial(_retention_kernel, heads=heads),
        grid=(b, s // chunk),
        in_specs=[grp(0), grp(1), grp(2), grp(3), tok, tok,
                  pl.BlockSpec((1, w), lambda i, c: (0, 0))],
        out_specs=pl.BlockSpec((1, chunk, w), lambda i, c: (i, c, 0)),
        out_shape=jax.ShapeDtypeStruct((b, s, w), BF16),
        scratch_shapes=[pltpu.VMEM((heads, HEAD_DIM, HEAD_DIM), F32),
                        pltpu.VMEM((heads, chunk, chunk), F32)],
        compiler_params=_params("arbitrary", "arbitrary"),
        name="retention",
    )(proj, proj, proj, proj, cos, sin, merge_scale)


def _fox_keys_kernel(k_ref, gp_ref, o_ref, *, tk, heads):
    ts = k_ref.shape[1]
    c = gp_ref[0]
    parts = []
    for i in range(ts // tk):
        blk = c[i * tk:(i + 1) * tk]
        parts.append((blk[0:1, :] - blk) * LOG2E)
    x = jnp.concatenate(parts, axis=0) if len(parts) > 1 else parts[0]
    hi = x.astype(BF16)
    r1 = x - hi.astype(F32)
    mid = r1.astype(BF16)
    lo = (r1 - mid.astype(F32)).astype(BF16)
    row = lax.broadcasted_iota(jnp.int32, (GATE_LANES, HEAD_DIM), 0)
    col = lax.broadcasted_iota(jnp.int32, (GATE_LANES, HEAD_DIM), 1)
    for h in range(heads):
        sel = lambda piece: jnp.where((row == h) & (col == piece), 1.0, 0.0).astype(BF16)
        bias = _dot(hi, sel(0)) + _dot(mid, sel(1)) + _dot(lo, sel(2))
        o_ref[0, h] = jnp.concatenate([k_ref[0, :, h * HEAD_DIM:(h + 1) * HEAD_DIM], bias.astype(BF16)], axis=1)


def _fox_keys(proj, gp, heads, k_col0, tk):
    b, s, _ = proj.shape
    ts = min(FOX_KEYS_TS, s)
    w = heads * HEAD_DIM
    assert (k_col0 * HEAD_DIM) % w == 0 and ts % tk == 0
    return pl.pallas_call(
        functools.partial(_fox_keys_kernel, tk=tk, heads=heads),
        grid=(b, s // ts),
        in_specs=[pl.BlockSpec((1, ts, w), lambda i, j: (i, j, k_col0 * HEAD_DIM // w)),
                  pl.BlockSpec((1, ts, GATE_LANES), lambda i, j: (i, j, 0))],
        out_specs=pl.BlockSpec((1, heads, ts, 2 * HEAD_DIM), lambda i, j: (i, 0, j, 0)),
        out_shape=jax.ShapeDtypeStruct((b, heads, s, 2 * HEAD_DIM), BF16),
        compiler_params=_params("arbitrary", "arbitrary"),
        name="fox_keys",
    )(proj, gp)


def _fox_query_block(qi, nq, q_ref, k_ref, v_ref, c_ref, ms_ref, o_ref, s_scr, m_scr, l_scr, acc_scr,
                     *, tq, tk, unroll):
    ndiag = tq // tk
    static_slots = unroll % 2 == 0 and ndiag % 2 == 0
    nfull = qi * ndiag
    q0 = pl.multiple_of(qi * tq, tq)

    def augmented_queries(start):
        q_t = (q_ref[0, pl.ds(start, tq), :].astype(F32) * (HEAD_DIM ** -0.5 * LOG2E)).T.astype(BF16)
        row = lax.broadcasted_iota(jnp.int32, (HEAD_DIM, tq), 0)
        return jnp.concatenate([q_t, jnp.where(row < BIAS_PIECES, 1.0, 0.0).astype(BF16)], axis=0)

    q_aug = augmented_queries(q0)
    c_first = c_ref[0, :, pl.ds(q0, HEAD_DIM)][:, 0:1]

    m_scr[...] = jnp.full_like(m_scr, -jnp.inf)
    l_scr[...] = jnp.zeros_like(l_scr)
    acc_scr[...] = jnp.zeros_like(acc_scr)

    def scores(j, slot, c0):
        k0 = pl.multiple_of(j * tk, tk)
        s_scr[slot, :, c0:] = _dot(k_ref[0, 0, pl.ds(k0, tk), :], q_aug[:, c0:])

    def softmax(j, slot, diag):
        k0 = pl.multiple_of(j * tk, tk)
        c0 = 0 if diag is None else diag * tk
        delta = (c_first - c_ref[0, :, pl.ds(k0, HEAD_DIM)][:, 0:1]) * LOG2E
        t = s_scr[slot, :, c0:]
        if diag is not None:
            key = lax.broadcasted_iota(jnp.int32, t.shape, 0)
            qry = lax.broadcasted_iota(jnp.int32, t.shape, 1)
            t = jnp.where(qry >= key, t, -jnp.inf)
        m_prev = m_scr[:, c0:]
        m_new = jnp.maximum(m_prev, jnp.max(t, axis=0, keepdims=True) + delta)
        p = jnp.exp2(t - (m_new - delta)).astype(BF16)
        alpha = jnp.exp2(m_prev - m_new)
        half = p[:tk // 2] + p[tk // 2:]
        quarter = half[:tk // 4] + half[tk // 4:]
        l_scr[:, c0:] = alpha * l_scr[:, c0:] + jnp.sum(quarter.astype(F32), axis=0, keepdims=True)
        m_scr[:, c0:] = m_new
        acc_scr[:, c0:] = alpha * acc_scr[:, c0:] + _dot_tn(v_ref[0, pl.ds(k0, tk), :], p)

    def step(j, slot, diag, last):
        if not last:
            scores(j + 1, 1 - slot, 0 if diag is None else (diag + 1) * tk)
        softmax(j, slot, diag)

    @pl.when(qi == 0)
    def _():
        scores(0, 0, 0)

    def body(i, carry):
        for u in range(unroll):
            j = i * unroll + u
            step(j, u % 2 if static_slots else j & 1, None, False)
        return carry

    lax.fori_loop(0, nfull // unroll, body, 0)
    for d in range(ndiag):
        j = nfull + d
        slot = d % 2 if ndiag % 2 == 0 else j & 1
        step(j, slot, d, d == ndiag - 1)

    def finalize():
        out = (acc_scr[...] / l_scr[...]).T
        o_ref[0, pl.ds(q0, tq), :] = (_head_norm(out) * ms_ref[...]).astype(BF16)

    @pl.when(qi + 1 < nq)
    def _():
        s_scr[0] = _dot(k_ref[0, 0, 0:tk, :], augmented_queries(pl.multiple_of(q0 + tq, tq)))
        finalize()

    @pl.when(qi + 1 == nq)
    def _():
        finalize()


def _fox_kernel(q_ref, *refs, tq, tk, unroll):
    nq = q_ref.shape[1] // tq

    def body(qi, carry):
        _fox_query_block(qi, nq, q_ref, *refs, tq=tq, tk=tk, unroll=unroll)
        return carry

    lax.fori_loop(0, nq, body, 0)


def _fox(proj, k_aug, c_rows, merge_scale, heads, col0, ms_col0, tq, tk, unroll):
    b, s, _ = proj.shape
    assert tq % tk == 0 and (tq // tk) % unroll == 0
    return pl.pallas_call(
        functools.partial(_fox_kernel, tq=tq, tk=tk, unroll=unroll),
        grid=(b, heads),
        in_specs=[pl.BlockSpec((1, s, HEAD_DIM), lambda i, h: (i, 0, col0 + h)),
                  pl.BlockSpec((1, 1, s, 2 * HEAD_DIM), lambda i, h: (i, h, 0, 0)),
                  pl.BlockSpec((1, s, HEAD_DIM), lambda i, h: (i, 0, col0 + 2 * heads + h)),
                  pl.BlockSpec((1, 1, s), lambda i, h: (i * heads + h, 0, 0)),
                  pl.BlockSpec((1, HEAD_DIM), lambda i, h: (0, ms_col0 + h))],
        out_specs=pl.BlockSpec((1, s, HEAD_DIM), lambda i, h: (i, 0, h)),
        out_shape=jax.ShapeDtypeStruct((b, s, heads * HEAD_DIM), BF16),
        scratch_shapes=[pltpu.VMEM((2, tk, tq), F32),
                        pltpu.VMEM((1, tq), F32),
                        pltpu.VMEM((1, tq), F32),
                        pltpu.VMEM((HEAD_DIM, tq), F32)],
        compiler_params=_params("arbitrary", "arbitrary"),
        name="fox_attention",
    )(proj, k_aug, proj, c_rows, merge_scale)


def _mlstm_kernel(q_ref, k_ref, v_ref, og_ref, w_ref, b_ref, gp_ref, ir_ref, gr_ref, ms_ref, o_ref,
                  c_scr, n_scr, m_scr, g0_scr, xq_scr, xk_scr, *, heads, i_lane0, f_lane0):
    n = q_ref.shape[1]
    w = heads * HEAD_DIM

    @pl.when(pl.program_id(1) == 0)
    def _():
        c_scr[...] = jnp.zeros_like(c_scr)
        n_scr[...] = jnp.zeros_like(n_scr)
        m_scr[...] = jnp.zeros_like(m_scr)
        g0_scr[...] = jnp.zeros_like(g0_scr)
        xq_scr[0:CONV_HALO, :] = jnp.zeros((CONV_HALO, w), F32)
        xk_scr[0:CONV_HALO, :] = jnp.zeros((CONV_HALO, w), F32)

    def conv_silu(x_ref, x_scr, cols):
        x_scr[CONV_HALO:CONV_HALO + n, :] = x_ref[0].astype(F32)
        y = b_ref[:, cols] + jnp.zeros((n, w), F32)
        for j in range(CONV_WIDTH):
            shift = CONV_WIDTH - 1 - j
            y = y + w_ref[j:j + 1, cols] * x_scr[CONV_HALO - shift:CONV_HALO - shift + n, :]
        x_scr[0:CONV_HALO, :] = x_scr[n:n + CONV_HALO, :]
        return y * _sigmoid(y)

    q_all = conv_silu(q_ref, xq_scr, slice(0, w))
    k_all = conv_silu(k_ref, xk_scr, slice(w, 2 * w)) * (HEAD_DIM ** -0.5)
    gp = gp_ref[0]
    row = lax.broadcasted_iota(jnp.int32, (n, n), 0)
    col = lax.broadcasted_iota(jnp.int32, (n, n), 1)
    causal = row >= col

    for h in range(heads):
        cs = slice(h * HEAD_DIM, (h + 1) * HEAD_DIM)
        q = q_all[:, cs]
        k = k_all[:, cs]
        v = v_ref[0, :, cs]
        ic = gp[:, i_lane0 + h:i_lane0 + h + 1]
        gc = gp[:, f_lane0 + h:f_lane0 + h + 1]
        ir = ir_ref[0, h:h + 1, :]
        gr = gr_ref[0, h:h + 1, :]

        g0 = g0_scr[h][:, 0:1]
        m_prev = m_scr[h][:, 0:1]
        b_col = gc - g0
        dmat = jnp.where(causal, gc + (ir - gr), -jnp.inf)
        inter_log = b_col + m_prev
        m_q = jnp.maximum(inter_log, jnp.max(dmat, axis=-1, keepdims=True))
        w_intra = jnp.exp(dmat - m_q)
        w_inter = jnp.exp(inter_log - m_q)

        qb = q.astype(BF16)
        s = _dot_nt(qb, k.astype(BF16)) * w_intra
        cst = c_scr[h]
        nst = n_scr[h]
        num = _dot(s.astype(BF16), v) + w_inter * _dot(qb, cst.astype(BF16))
        den = jnp.sum(s, axis=-1, keepdims=True) + w_inter * jnp.sum(q * nst, axis=-1, keepdims=True)
        hid = num / jnp.maximum(jnp.abs(den), jnp.exp(-m_q))

        b_last = b_col[n - 1:n, :]
        k_log = b_last - b_col + ic
        m_new = jnp.maximum(b_last + m_prev, jnp.max(k_log, axis=0, keepdims=True))
        wk = jnp.exp(k_log - m_new)
        carry_scale = jnp.exp(b_last + m_prev - m_new)
        kw = k * wk
        c_scr[h] = carry_scale * cst + _dot_tn(kw.astype(BF16), v)
        n_scr[h] = carry_scale * nst + jnp.sum(kw, axis=0, keepdims=True)
        m_scr[h] = jnp.broadcast_to(m_new, (1, HEAD_DIM))
        g0_scr[h] = jnp.broadcast_to(gc[n - 1:n, :], (1, HEAD_DIM))

        og = og_ref[0, :, cs].astype(F32)
        o_ref[0, :, cs] = (_head_norm(hid) * _sigmoid(og) * ms_ref[:, cs]).astype(BF16)


def _mlstm(proj, conv_w, conv_b, gp, i_rows, g_rows, merge_scale, heads, col0, ms_col0, i_lane0, f_lane0, chunk):
    b, s, _ = proj.shape
    w = heads * HEAD_DIM
    assert (col0 * HEAD_DIM) % w == 0 and (ms_col0 * HEAD_DIM) % w == 0
    blk0 = col0 * HEAD_DIM // w
    grp = lambda base: pl.BlockSpec((1, chunk, w), lambda i, c: (i, c, blk0 + base))
    row_spec = pl.BlockSpec((1, heads, chunk), lambda i, c: (i, 0, c))
    state = lambda r, c: pltpu.VMEM((heads, r, c), F32)
    return pl.pallas_call(
        functools.partial(_mlstm_kernel, heads=heads, i_lane0=i_lane0, f_lane0=f_lane0),
        grid=(b, s // chunk),
        in_specs=[grp(0), grp(1), grp(2), grp(3),
                  pl.BlockSpec(conv_w.shape, lambda i, c: (0, 0)),
                  pl.BlockSpec(conv_b.shape, lambda i, c: (0, 0)),
                  pl.BlockSpec((1, chunk, GATE_LANES), lambda i, c: (i, c, 0)),
                  row_spec, row_spec,
                  pl.BlockSpec((1, w), lambda i, c: (0, ms_col0 * HEAD_DIM // w))],
        out_specs=pl.BlockSpec((1, chunk, w), lambda i, c: (i, c, 0)),
        out_shape=jax.ShapeDtypeStruct((b, s, w), BF16),
        scratch_shapes=[state(HEAD_DIM, HEAD_DIM), state(1, HEAD_DIM), state(1, HEAD_DIM), state(1, HEAD_DIM),
                        pltpu.VMEM((chunk + CONV_HALO, w), F32),
                        pltpu.VMEM((chunk + CONV_HALO, w), F32)],
        compiler_params=_params("arbitrary", "arbitrary"),
        name="mlstm",
    )(proj, proj, proj, proj, conv_w, conv_b, gp, i_rows, g_rows, merge_scale)


def _outproj_kernel(yr_ref, yf_ref, ym_ref, w_ref, x_ref, ga_ref, o_ref):
    r = yr_ref.shape[1]
    f = yf_ref.shape[1]
    y = _dot(yr_ref[...], w_ref[0:r, :]) + _dot(yf_ref[...], w_ref[r:r + f, :])
    y = y + _dot(ym_ref[...], w_ref[r + f:, :])
    o_ref[...] = x_ref[...] + ga_ref[0] * y


def _outproj(y_ret, y_fox, y_m, w_out, x2, g_a, seq):
    m, d = x2.shape
    tm = min(OUTPROJ_TM, seq)
    bpr = seq // tm
    rows = lambda w: pl.BlockSpec((tm, w), lambda i: (i, 0))
    return pl.pallas_call(
        _outproj_kernel,
        grid=(m // tm,),
        in_specs=[rows(y_ret.shape[1]), rows(y_fox.shape[1]), rows(y_m.shape[1]),
                  pl.BlockSpec(w_out.shape, lambda i: (0, 0)),
                  rows(d),
                  pl.BlockSpec((1, 1, d), lambda i: (i // bpr, 0, 0))],
        out_specs=rows(d),
        out_shape=jax.ShapeDtypeStruct((m, d), F32),
        compiler_params=_params("arbitrary"),
        name="outproj",
    )(y_ret, y_fox, y_m, w_out, x2, g_a)


def _mlp_kernel(x_ref, nw_ref, sc_ref, sh_ref, gm_ref, w1_hbm, w2_hbm, fw_ref, o_ref,
                w1_buf, w2_buf, sem, h_scr, *, final_norm, tf, nf):
    i = pl.program_id(0)
    n = pl.num_programs(0)

    def tile_copies(f, slot):
        c0 = pl.multiple_of(f * tf, tf)
        return (pltpu.make_async_copy(w1_hbm.at[:, pl.ds(c0, tf)], w1_buf.at[slot], sem.at[0, slot]),
                pltpu.make_async_copy(w2_hbm.at[pl.ds(c0, tf), :], w2_buf.at[slot], sem.at[1, slot]))

    @pl.when(i == 0)
    def _():
        for cp in tile_copies(0, 0):
            cp.start()

    _modulated_norm_rows(x_ref, nw_ref, sc_ref, sh_ref, h_scr)
    o_ref[...] = jnp.zeros_like(o_ref)

    def body(f, carry):
        slot = f % 2

        @pl.when((f + 1 < nf) | (i + 1 < n))
        def _():
            for cp in tile_copies((f + 1) % nf, 1 - slot):
                cp.start()

        for cp in tile_copies(f, slot):
            cp.wait()
        a = jnp.maximum(_dot(h_scr[...], w1_buf[slot]), 0.0)
        o_ref[...] += _dot((a * a).astype(BF16), w2_buf[slot])
        return carry

    lax.fori_loop(0, nf, body, 0)

    y = x_ref[...] + gm_ref[0] * o_ref[...]
    if final_norm:
        ms = jnp.mean(y * y, axis=-1, keepdims=True)
        y = y * lax.rsqrt(ms + NORM_EPS) * fw_ref[...]
    o_ref[...] = y


def _mlp(x2, nw, sc, sh, g_m, w1, w2, final_w, seq, final_norm):
    m, d = x2.shape
    dff = w1.shape[1]
    tm = min(MLP_TM, seq)
    tf = min(MLP_TF, dff)
    nf = dff // tf
    assert nf % 2 == 0
    bpr = seq // tm
    mod = pl.BlockSpec((1, 1, d), lambda i: (i // bpr, 0, 0))
    vec = pl.BlockSpec((1, d), lambda i: (0, 0))
    hbm = pl.BlockSpec(memory_space=pl.ANY)
    return pl.pallas_call(
        functools.partial(_mlp_kernel, final_norm=final_norm, tf=tf, nf=nf),
        grid=(m // tm,),
        in_specs=[pl.BlockSpec((tm, d), lambda i: (i, 0)),
                  vec, mod, mod, mod, hbm, hbm, vec],
        out_specs=pl.BlockSpec((tm, d), lambda i: (i, 0)),
        out_shape=jax.ShapeDtypeStruct((m, d), F32),
        scratch_shapes=[pltpu.VMEM((2, d, tf), BF16),
                        pltpu.VMEM((2, tf, d), BF16),
                        pltpu.SemaphoreType.DMA((2, 2)),
                        pltpu.VMEM((tm, d), BF16)],
        compiler_params=_params("arbitrary"),
        name="mlp",
    )(x2, nw, sc, sh, g_m, w1, w2, final_w)


def kernel(x, c, positions, ada_w, ada_b, norm_mix_w, norm_mlp_w, w_in, conv_w, conv_b, fox_f_bias,
           mlstm_i_bias, mlstm_f_bias, merge_scale, w_out, w_ff1, w_ff2, final_norm_w):
    b, s, d = x.shape
    depth = ada_w.shape[0]
    ret_w, fox_w, ml_w = d // 4, d // 2, d // 4
    ret_h, fox_h, ml_h = ret_w // HEAD_DIM, fox_w // HEAD_DIM, ml_w // HEAD_DIM
    assert s % 128 == 0 and d % (4 * HEAD_DIM) == 0
    assert all(s % min(t, s) == 0 for t in (ROPE_TS, INPROJ_TM, GATES_TS, RET_CHUNK, MLSTM_CHUNK, FOX_KEYS_TS,
                                             FOX_TQ, OUTPROJ_TM, MLP_TM))
    assert fox_h + 2 * ml_h <= GATE_LANES

    sizes = [ret_w] * 4 + [fox_w] * 3 + [fox_h] + [ml_w] * 4 + [ml_h, ml_h]
    offs = [0]
    for sz in sizes:
        offs.append(offs[-1] + sz)
    n_gate = fox_h + 2 * ml_h
    ret_col0 = 0
    fox_col0 = (4 * ret_w) // HEAD_DIM
    ml_col0 = (4 * ret_w + 3 * fox_w) // HEAD_DIM
    i_lane0, f_lane0 = fox_h, fox_h + ml_h

    ret_chunk = min(RET_CHUNK, s)
    ml_chunk = min(MLSTM_CHUNK, s)
    tq = min(FOX_TQ, s)
    tk = min(FOX_TK, tq)

    mod = _ada_mod(c, ada_w, ada_b)
    cos, sin = _rope_tables(positions)
    cum_mask = jnp.zeros((1, GATE_LANES), F32).at[0, :fox_h].set(1.0).at[0, f_lane0:f_lane0 + ml_h].set(1.0)

    x2 = x.reshape(b * s, d)
    for layer in range(depth):
        sh_a, sc_a, g_a, sh_m, sc_m, g_m = [t.reshape(b, 1, d) for t in jnp.split(mod[layer], N_MOD, axis=-1)]
        wl = w_in[layer]
        w_main = jnp.concatenate([wl[:, offs[0]:offs[7]], wl[:, offs[8]:offs[12]]], axis=1).astype(BF16)
        w_gate = jnp.concatenate([wl[:, offs[7]:offs[8]], wl[:, offs[12]:offs[14]],
                                  jnp.zeros((d, GATE_LANES - n_gate), F32)], axis=1).astype(BF16)
        gate_bias = jnp.pad(jnp.concatenate([fox_f_bias[layer], mlstm_i_bias[layer], mlstm_f_bias[layer]]),
                            (0, GATE_LANES - n_gate)).reshape(1, GATE_LANES)
        ms = merge_scale[layer].reshape(1, d)

        proj, gates = _inproj(x2, norm_mix_w[layer].reshape(1, d), sc_a, sh_a, w_main, w_gate, s)
        proj = proj.reshape(b, s, -1)
        gp = _gates(gates.reshape(b, s, GATE_LANES), gate_bias, cum_mask)
        gp_t = jnp.swapaxes(gp[:, :, :n_gate], 1, 2)
        c_rows = gp_t[:, :fox_h].reshape(b * fox_h, 1, s)
        i_rows = gp_t[:, i_lane0:i_lane0 + ml_h]
        g_rows = gp_t[:, f_lane0:f_lane0 + ml_h]

        y_ret = _retention(proj, cos, sin, ms, ret_h, ret_col0, ret_chunk)
        k_aug = _fox_keys(proj, gp, fox_h, fox_col0 + fox_h, tk)
        y_fox = _fox(proj, k_aug, c_rows, ms, fox_h, fox_col0, ret_h, tq, tk, FOX_UNROLL if tq // tk % FOX_UNROLL == 0 else 1)
        y_m = _mlstm(proj, conv_w[layer], conv_b[layer].reshape(1, -1), gp, i_rows, g_rows, ms,
                     ml_h, ml_col0, ret_h + fox_h, i_lane0, f_lane0, ml_chunk)

        x2 = _outproj(y_ret.reshape(b * s, -1), y_fox.reshape(b * s, -1), y_m.reshape(b * s, -1),
                      w_out[layer].astype(BF16), x2, g_a, s)
        x2 = _mlp(x2, norm_mlp_w[layer].reshape(1, d), sc_m, sh_m, g_m,
                  w_ff1[layer].astype(BF16), w_ff2[layer].astype(BF16),
                  final_norm_w.reshape(1, d), s, layer == depth - 1)
    return x2.reshape(b, s, d)
```

```python
import functools
import math

import jax
import jax.numpy as jnp
from jax import lax
from jax.experimental import pallas as pl
from jax.experimental.pallas import tpu as pltpu

HEAD_DIM = 128
CONV_WIDTH = 4
ROPE_BASE = 10000.0
NORM_EPS = 1e-6
N_MOD = 6
GATE_LANES = 128
LOG2E = math.log2(math.e)
BIAS_PIECES = 3
CONV_HALO = 8

ADA_TN, ADA_ROWS = 1024, 16
ROPE_TS = 2048
INPROJ_TM, INPROJ_TN = 1024, (1792, 1024, 512, 256, 128)
GATES_TS = 512
RET_CHUNK, MLSTM_CHUNK = 256, 512
FOX_KEYS_TS = 1024
FOX_TQ, FOX_TK, FOX_UNROLL = 1024, 512, 2
OUTPROJ_TM = 512
MLP_TM, MLP_TF = 1024, 512

F32 = jnp.float32
BF16 = jnp.bfloat16

VMEM_LIMIT_BYTES = 56 * 1024 * 1024


def _params(*sem):
    return pltpu.CompilerParams(dimension_semantics=sem, vmem_limit_bytes=VMEM_LIMIT_BYTES)


def _dot(a, b):
    return jnp.dot(a, b, preferred_element_type=F32)


def _dot_nt(a, b):
    return lax.dot_general(a, b, (((1,), (1,)), ((), ())), preferred_element_type=F32)


def _dot_tn(a, b):
    return lax.dot_general(a, b, (((0,), (0,)), ((), ())), preferred_element_type=F32)


def _sigmoid(x):
    return 1.0 / (1.0 + jnp.exp(-x))


def _log_sigmoid(x):
    return jnp.minimum(x, 0.0) - jnp.log1p(jnp.exp(-jnp.abs(x)))


def _head_norm(y):
    mu = jnp.mean(y, axis=-1, keepdims=True)
    yc = y - mu
    var = jnp.mean(yc * yc, axis=-1, keepdims=True)
    return yc * lax.rsqrt(var + NORM_EPS)


def _ada_kernel(c_ref, w_ref, b_ref, o_ref):
    c = c_ref[...]
    cond = (c * _sigmoid(c)).astype(BF16)
    o_ref[0] = _dot(cond, w_ref[0].astype(BF16)) + b_ref[0]


def _ada_mod(c, ada_w, ada_b):
    depth, d, n = ada_w.shape
    nb = c.shape[0]
    rows = -(-nb // ADA_ROWS) * ADA_ROWS
    tn = min(ADA_TN, n)
    c_pad = jnp.pad(c, ((0, rows - nb), (0, 0)))
    out = pl.pallas_call(
        _ada_kernel,
        grid=(depth, n // tn),
        in_specs=[pl.BlockSpec((rows, d), lambda l, j: (0, 0)),
                  pl.BlockSpec((1, d, tn), lambda l, j: (l, 0, j)),
                  pl.BlockSpec((1, 1, tn), lambda l, j: (l, 0, j))],
        out_specs=pl.BlockSpec((1, rows, tn), lambda l, j: (l, 0, j)),
        out_shape=jax.ShapeDtypeStruct((depth, rows, n), F32),
        compiler_params=_params("arbitrary", "arbitrary"),
        name="ada_mod",
    )(c_pad, ada_w, ada_b.reshape(depth, 1, n))
    return out[:, :nb]


def _rope_kernel(pos_ref, invf_ref, sign_ref, cos_ref, sin_ref):
    ang = pos_ref[0] * invf_ref[...]
    cos_ref[0] = jnp.cos(ang)
    sin_ref[0] = jnp.sin(ang) * sign_ref[...]


def _rope_tables(positions):
    b, s = positions.shape
    ts = min(ROPE_TS, s)
    half = HEAD_DIM // 2
    inv_freq = ROPE_BASE ** (-jnp.arange(0, HEAD_DIM, 2, dtype=F32) / HEAD_DIM)
    invf = jnp.concatenate([inv_freq, inv_freq]).reshape(1, HEAD_DIM)
    sign = jnp.concatenate([-jnp.ones((half,), F32), jnp.ones((half,), F32)]).reshape(1, HEAD_DIM)
    pos = positions.astype(F32).reshape(b, s, 1)
    out = jax.ShapeDtypeStruct((b, s, HEAD_DIM), F32)
    return pl.pallas_call(
        _rope_kernel,
        grid=(b, s // ts),
        in_specs=[pl.BlockSpec((1, ts, 1), lambda i, j: (i, j, 0)),
                  pl.BlockSpec((1, HEAD_DIM), lambda i, j: (0, 0)),
                  pl.BlockSpec((1, HEAD_DIM), lambda i, j: (0, 0))],
        out_specs=[pl.BlockSpec((1, ts, HEAD_DIM), lambda i, j: (i, j, 0))] * 2,
        out_shape=[out, out],
        compiler_params=_params("arbitrary", "arbitrary"),
        name="rope_tables",
    )(pos, invf, sign)


def _modulated_norm_rows(x_ref, nw_ref, sc_ref, sh_ref, h_scr):
    x = x_ref[...]
    inv = lax.rsqrt(jnp.mean(x * x, axis=-1, keepdims=True) + NORM_EPS)
    h_scr[...] = ((x_ref[...] * inv * nw_ref[...]) * (1.0 + sc_ref[0]) + sh_ref[0]).astype(BF16)


def _inproj_kernel(x_ref, nw_ref, sc_ref, sh_ref, w_ref, wg_ref, o_ref, g_ref, h_scr):
    @pl.when(pl.program_id(1) == 0)
    def _():
        _modulated_norm_rows(x_ref, nw_ref, sc_ref, sh_ref, h_scr)
        g_ref[...] = _dot(h_scr[...], wg_ref[...])

    o_ref[...] = _dot(h_scr[...], w_ref[...]).astype(BF16)


def _inproj(x2, nw, sc, sh, w_main, w_gate, seq):
    m, d = x2.shape
    nm = w_main.shape[1]
    tm = min(INPROJ_TM, seq)
    tn = next(t for t in INPROJ_TN if nm % t == 0)
    bpr = seq // tm
    return pl.pallas_call(
        _inproj_kernel,
        grid=(m // tm, nm // tn),
        in_specs=[pl.BlockSpec((tm, d), lambda i, j: (i, 0)),
                  pl.BlockSpec((1, d), lambda i, j: (0, 0)),
                  pl.BlockSpec((1, 1, d), lambda i, j: (i // bpr, 0, 0)),
                  pl.BlockSpec((1, 1, d), lambda i, j: (i // bpr, 0, 0)),
                  pl.BlockSpec((d, tn), lambda i, j: (0, j)),
                  pl.BlockSpec((d, GATE_LANES), lambda i, j: (0, 0))],
        out_specs=[pl.BlockSpec((tm, tn), lambda i, j: (i, j)),
                   pl.BlockSpec((tm, GATE_LANES), lambda i, j: (i, 0))],
        out_shape=[jax.ShapeDtypeStruct((m, nm), BF16),
                   jax.ShapeDtypeStruct((m, GATE_LANES), F32)],
        scratch_shapes=[pltpu.VMEM((tm, d), BF16)],
        compiler_params=_params("arbitrary", "arbitrary"),
        name="inproj",
    )(x2, nw, sc, sh, w_main, w_gate)


def _gates_kernel(g_ref, bias_ref, cum_ref, o_ref, carry):
    @pl.when(pl.program_id(1) == 0)
    def _():
        carry[...] = jnp.zeros_like(carry)

    g = g_ref[0] + bias_ref[...]
    ls = _log_sigmoid(g)
    ts = ls.shape[0]
    row = lax.broadcasted_iota(jnp.int32, (ts, ts), 0)
    col = lax.broadcasted_iota(jnp.int32, (ts, ts), 1)
    tri = jnp.where(row >= col, 1.0, 0.0).astype(BF16)
    hi = ls.astype(BF16)
    r1 = ls - hi.astype(F32)
    mid = r1.astype(BF16)
    lo = (r1 - mid.astype(F32)).astype(BF16)
    csum = (_dot(tri, hi) + _dot(tri, mid)) + _dot(tri, lo) + carry[...]
    carry[...] = csum[ts - 1:ts, :]
    o_ref[0] = jnp.where(cum_ref[...] > 0.5, csum, g)


def _gates(g3, bias, cum_mask):
    b, s, n = g3.shape
    ts = min(GATES_TS, s)
    return pl.pallas_call(
        _gates_kernel,
        grid=(b, s // ts),
        in_specs=[pl.BlockSpec((1, ts, n), lambda i, j: (i, j, 0)),
                  pl.BlockSpec((1, n), lambda i, j: (0, 0)),
                  pl.BlockSpec((1, n), lambda i, j: (0, 0))],
        out_specs=pl.BlockSpec((1, ts, n), lambda i, j: (i, j, 0)),
        out_shape=jax.ShapeDtypeStruct((b, s, n), F32),
        scratch_shapes=[pltpu.VMEM((1, n), F32)],
        compiler_params=_params("arbitrary", "arbitrary"),
        name="gates",
    )(g3, bias, cum_mask)


def _retention_kernel(q_ref, k_ref, v_ref, g_ref, cos_ref, sin_ref, ms_ref, o_ref, state, decay, *, heads):
    n = cos_ref.shape[1]
    log_gamma = [math.log(1.0 - 2.0 ** (-5.0 - h)) for h in range(heads)]

    @pl.when(pl.program_id(1) == 0)
    def _():
        state[...] = jnp.zeros_like(state)
        row = lax.broadcasted_iota(jnp.int32, (n, n), 0)
        col = lax.broadcasted_iota(jnp.int32, (n, n), 1)
        rel = (row - col).astype(F32)
        for h in range(heads):
            decay[h] = jnp.where(rel >= 0, jnp.exp(jnp.maximum(rel, 0.0) * log_gamma[h]), 0.0)

    cos = cos_ref[0]
    sin = sin_ref[0]
    half = HEAD_DIM // 2
    idx = lax.broadcasted_iota(jnp.int32, (n, 1), 0).astype(F32)

    def rot(t):
        return t * cos + pltpu.roll(t, half, 1) * sin

    for h in range(heads):
        lg = log_gamma[h]
        cs = slice(h * HEAD_DIM, (h + 1) * HEAD_DIM)
        q = rot(q_ref[0, :, cs].astype(F32))
        k = rot(k_ref[0, :, cs].astype(F32)) * (HEAD_DIM ** -0.5)
        v = v_ref[0, :, cs]
        q_decay = jnp.exp((idx + 1.0) * lg)
        k_decay = jnp.exp((n - 1.0 - idx) * lg)
        chunk_decay = math.exp(n * lg)

        scores = _dot_nt(q.astype(BF16), k.astype(BF16)) * decay[h]
        intra = _dot(scores.astype(BF16), v)
        st = state[h]
        inter = _dot((q * q_decay).astype(BF16), st.astype(BF16))
        state[h] = st * chunk_decay + _dot_tn((k * k_decay).astype(BF16), v)

        y = _head_norm(intra + inter)
        gate = g_ref[0, :, cs].astype(F32)
        o_ref[0, :, cs] = (y * (gate * _sigmoid(gate)) * ms_ref[:, cs]).astype(BF16)


def _retention(proj, cos, sin, merge_scale, heads, col0, chunk):
    b, s, _ = proj.shape
    w = heads * HEAD_DIM
    assert (col0 * HEAD_DIM) % w == 0
    blk0 = col0 * HEAD_DIM // w
    grp = lambda base: pl.BlockSpec((1, chunk, w), lambda i, c: (i, c, blk0 + base))
    tok = pl.BlockSpec((1, chunk, HEAD_DIM), lambda i, c: (i, c, 0))
    return pl.pallas_call(
        functools.partial(_retention_kernel, heads=heads),
        grid=(b, s // chunk),
        in_specs=[grp(0), grp(1), grp(2), grp(3), tok, tok,
                  pl.BlockSpec((1, w), lambda i, c: (0, 0))],
        out_specs=pl.BlockSpec((1, chunk, w), lambda i, c: (i, c, 0)),
        out_shape=jax.ShapeDtypeStruct((b, s, w), BF16),
        scratch_shapes=[pltpu.VMEM((heads, HEAD_DIM, HEAD_DIM), F32),
                        pltpu.VMEM((heads, chunk, chunk), F32)],
        compiler_params=_params("arbitrary", "arbitrary"),
        name="retention",
    )(proj, proj, proj, proj, cos, sin, merge_scale)


def _fox_keys_kernel(k_ref, gp_ref, o_ref, *, tk, heads):
    ts = k_ref.shape[1]
    c = gp_ref[0]
    parts = []
    for i in range(ts // tk):
        blk = c[i * tk:(i + 1) * tk]
        parts.append((blk[0:1, :] - blk) * LOG2E)
    x = jnp.concatenate(parts, axis=0) if len(parts) > 1 else parts[0]
    hi = x.astype(BF16)
    r1 = x - hi.astype(F32)
    mid = r1.astype(BF16)
    lo = (r1 - mid.astype(F32)).astype(BF16)
    row = lax.broadcasted_iota(jnp.int32, (GATE_LANES, HEAD_DIM), 0)
    col = lax.broadcasted_iota(jnp.int32, (GATE_LANES, HEAD_DIM), 1)
    for h in range(heads):
        sel = lambda piece: jnp.where((row == h) & (col == piece), 1.0, 0.0).astype(BF16)
        bias = _dot(hi, sel(0)) + _dot(mid, sel(1)) + _dot(lo, sel(2))
        o_ref[0, h] = jnp.concatenate([k_ref[0, :, h * HEAD_DIM:(h + 1) * HEAD_DIM], bias.astype(BF16)], axis=1)


def _fox_keys(proj, gp, heads, k_col0, tk):
    b, s, _ = proj.shape
    ts = min(FOX_KEYS_TS, s)
    w = heads * HEAD_DIM
    assert (k_col0 * HEAD_DIM) % w == 0 and ts % tk == 0
    return pl.pallas_call(
        functools.partial(_fox_keys_kernel, tk=tk, heads=heads),
        grid=(b, s // ts),
        in_specs=[pl.BlockSpec((1, ts, w), lambda i, j: (i, j, k_col0 * HEAD_DIM // w)),
                  pl.BlockSpec((1, ts, GATE_LANES), lambda i, j: (i, j, 0))],
        out_specs=pl.BlockSpec((1, heads, ts, 2 * HEAD_DIM), lambda i, j: (i, 0, j, 0)),
        out_shape=jax.ShapeDtypeStruct((b, heads, s, 2 * HEAD_DIM), BF16),
        compiler_params=_params("arbitrary", "arbitrary"),
        name="fox_keys",
    )(proj, gp)


def _fox_query_block(qi, nq, q_ref, k_ref, v_ref, c_ref, ms_ref, o_ref, s_scr, m_scr, l_scr, acc_scr,
                     *, tq, tk, unroll):
    ndiag = tq // tk
    static_slots = unroll % 2 == 0 and ndiag % 2 == 0
    nfull = qi * ndiag
    q0 = pl.multiple_of(qi * tq, tq)

    def augmented_queries(start):
        q_t = (q_ref[0, pl.ds(start, tq), :].astype(F32) * (HEAD_DIM ** -0.5 * LOG2E)).T.astype(BF16)
        row = lax.broadcasted_iota(jnp.int32, (HEAD_DIM, tq), 0)
        return jnp.concatenate([q_t, jnp.where(row < BIAS_PIECES, 1.0, 0.0).astype(BF16)], axis=0)

    q_aug = augmented_queries(q0)
    c_first = c_ref[0, :, pl.ds(q0, HEAD_DIM)][:, 0:1]

    m_scr[...] = jnp.full_like(m_scr, -jnp.inf)
    l_scr[...] = jnp.zeros_like(l_scr)
    acc_scr[...] = jnp.zeros_like(acc_scr)

    def scores(j, slot, c0):
        k0 = pl.multiple_of(j * tk, tk)
        s_scr[slot, :, c0:] = _dot(k_ref[0, 0, pl.ds(k0, tk), :], q_aug[:, c0:])

    def softmax(j, slot, diag):
        k0 = pl.multiple_of(j * tk, tk)
        c0 = 0 if diag is None else diag * tk
        delta = (c_first - c_ref[0, :, pl.ds(k0, HEAD_DIM)][:, 0:1]) * LOG2E
        t = s_scr[slot, :, c0:]
        if diag is not None:
            key = lax.broadcasted_iota(jnp.int32, t.shape, 0)
            qry = lax.broadcasted_iota(jnp.int32, t.shape, 1)
            t = jnp.where(qry >= key, t, -jnp.inf)
        m_prev = m_scr[:, c0:]
        m_new = jnp.maximum(m_prev, jnp.max(t, axis=0, keepdims=True) + delta)
        p = jnp.exp2(t - (m_new - delta)).astype(BF16)
        alpha = jnp.exp2(m_prev - m_new)
        half = p[:tk // 2] + p[tk // 2:]
        quarter = half[:tk // 4] + half[tk // 4:]
        l_scr[:, c0:] = alpha * l_scr[:, c0:] + jnp.sum(quarter.astype(F32), axis=0, keepdims=True)
        m_scr[:, c0:] = m_new
        acc_scr[:, c0:] = alpha * acc_scr[:, c0:] + _dot_tn(v_ref[0, pl.ds(k0, tk), :], p)

    def step(j, slot, diag, last):
        if not last:
            scores(j + 1, 1 - slot, 0 if diag is None else (diag + 1) * tk)
        softmax(j, slot, diag)

    @pl.when(qi == 0)
    def _():
        scores(0, 0, 0)

    def body(i, carry):
        for u in range(unroll):
            j = i * unroll + u
            step(j, u % 2 if static_slots else j & 1, None, False)
        return carry

    lax.fori_loop(0, nfull // unroll, body, 0)
    for d in range(ndiag):
        j = nfull + d
        slot = d % 2 if ndiag % 2 == 0 else j & 1
        step(j, slot, d, d == ndiag - 1)

    def finalize():
        out = (acc_scr[...] / l_scr[...]).T
        o_ref[0, pl.ds(q0, tq), :] = (_head_norm(out) * ms_ref[...]).astype(BF16)

    @pl.when(qi + 1 < nq)
    def _():
        s_scr[0] = _dot(k_ref[0, 0, 0:tk, :], augmented_queries(pl.multiple_of(q0 + tq, tq)))
        finalize()

    @pl.when(qi + 1 == nq)
    def _():
        finalize()


def _fox_kernel(q_ref, *refs, tq, tk, unroll):
    nq = q_ref.shape[1] // tq

    def body(qi, carry):
        _fox_query_block(qi, nq, q_ref, *refs, tq=tq, tk=tk, unroll=unroll)
        return carry

    lax.fori_loop(0, nq, body, 0)


def _fox(proj, k_aug, c_rows, merge_scale, heads, col0, ms_col0, tq, tk, unroll):
    b, s, _ = proj.shape
    assert tq % tk == 0 and (tq // tk) % unroll == 0
    return pl.pallas_call(
        functools.partial(_fox_kernel, tq=tq, tk=tk, unroll=unroll),
        grid=(b, heads),
        in_specs=[pl.BlockSpec((1, s, HEAD_DIM), lambda i, h: (i, 0, col0 + h)),
                  pl.BlockSpec((1, 1, s, 2 * HEAD_DIM), lambda i, h: (i, h, 0, 0)),
                  pl.BlockSpec((1, s, HEAD_DIM), lambda i, h: (i, 0, col0 + 2 * heads + h)),
                  pl.BlockSpec((1, 1, s), lambda i, h: (i * heads + h, 0, 0)),
                  pl.BlockSpec((1, HEAD_DIM), lambda i, h: (0, ms_col0 + h))],
        out_specs=pl.BlockSpec((1, s, HEAD_DIM), lambda i, h: (i, 0, h)),
        out_shape=jax.ShapeDtypeStruct((b, s, heads * HEAD_DIM), BF16),
        scratch_shapes=[pltpu.VMEM((2, tk, tq), F32),
                        pltpu.VMEM((1, tq), F32),
                        pltpu.VMEM((1, tq), F32),
                        pltpu.VMEM((HEAD_DIM, tq), F32)],
        compiler_params=_params("arbitrary", "arbitrary"),
        name="fox_attention",
    )(proj, k_aug, proj, c_rows, merge_scale)


def _mlstm_kernel(q_ref, k_ref, v_ref, og_ref, w_ref, b_ref, gp_ref, ir_ref, gr_ref, ms_ref, o_ref,
                  c_scr, n_scr, m_scr, g0_scr, xq_scr, xk_scr, *, heads, i_lane0, f_lane0):
    n = q_ref.shape[1]
    w = heads * HEAD_DIM

    @pl.when(pl.program_id(1) == 0)
    def _():
        c_scr[...] = jnp.zeros_like(c_scr)
        n_scr[...] = jnp.zeros_like(n_scr)
        m_scr[...] = jnp.zeros_like(m_scr)
        g0_scr[...] = jnp.zeros_like(g0_scr)
        xq_scr[0:CONV_HALO, :] = jnp.zeros((CONV_HALO, w), F32)
        xk_scr[0:CONV_HALO, :] = jnp.zeros((CONV_HALO, w), F32)

    def conv_silu(x_ref, x_scr, cols):
        x_scr[CONV_HALO:CONV_HALO + n, :] = x_ref[0].astype(F32)
        y = b_ref[:, cols] + jnp.zeros((n, w), F32)
        for j in range(CONV_WIDTH):
            shift = CONV_WIDTH - 1 - j
            y = y + w_ref[j:j + 1, cols] * x_scr[CONV_HALO - shift:CONV_HALO - shift + n, :]
        x_scr[0:CONV_HALO, :] = x_scr[n:n + CONV_HALO, :]
        return y * _sigmoid(y)

    q_all = conv_silu(q_ref, xq_scr, slice(0, w))
    k_all = conv_silu(k_ref, xk_scr, slice(w, 2 * w)) * (HEAD_DIM ** -0.5)
    gp = gp_ref[0]
    row = lax.broadcasted_iota(jnp.int32, (n, n), 0)
    col = lax.broadcasted_iota(jnp.int32, (n, n), 1)
    causal = row >= col

    for h in range(heads):
        cs = slice(h * HEAD_DIM, (h + 1) * HEAD_DIM)
        q = q_all[:, cs]
        k = k_all[:, cs]
        v = v_ref[0, :, cs]
        ic = gp[:, i_lane0 + h:i_lane0 + h + 1]
        gc = gp[:, f_lane0 + h:f_lane0 + h + 1]
        ir = ir_ref[0, h:h + 1, :]
        gr = gr_ref[0, h:h + 1, :]

        g0 = g0_scr[h][:, 0:1]
        m_prev = m_scr[h][:, 0:1]
        b_col = gc - g0
        dmat = jnp.where(causal, gc + (ir - gr), -jnp.inf)
        inter_log = b_col + m_prev
        m_q = jnp.maximum(inter_log, jnp.max(dmat, axis=-1, keepdims=True))
        w_intra = jnp.exp(dmat - m_q)
        w_inter = jnp.exp(inter_log - m_q)

        qb = q.astype(BF16)
        s = _dot_nt(qb, k.astype(BF16)) * w_intra
        cst = c_scr[h]
        nst = n_scr[h]
        num = _dot(s.astype(BF16), v) + w_inter * _dot(qb, cst.astype(BF16))
        den = jnp.sum(s, axis=-1, keepdims=True) + w_inter * jnp.sum(q * nst, axis=-1, keepdims=True)
        hid = num / jnp.maximum(jnp.abs(den), jnp.exp(-m_q))

        b_last = b_col[n - 1:n, :]
        k_log = b_last - b_col + ic
        m_new = jnp.maximum(b_last + m_prev, jnp.max(k_log, axis=0, keepdims=True))
        wk = jnp.exp(k_log - m_new)
        carry_scale = jnp.exp(b_last + m_prev - m_new)
        kw = k * wk
        c_scr[h] = carry_scale * cst + _dot_tn(kw.astype(BF16), v)
        n_scr[h] = carry_scale * nst + jnp.sum(kw, axis=0, keepdims=True)
        m_scr[h] = jnp.broadcast_to(m_new, (1, HEAD_DIM))
        g0_scr[h] = jnp.broadcast_to(gc[n - 1:n, :], (1, HEAD_DIM))

        og = og_ref[0, :, cs].astype(F32)
        o_ref[0, :, cs] = (_head_norm(hid) * _sigmoid(og) * ms_ref[:, cs]).astype(BF16)


def _mlstm(proj, conv_w, conv_b, gp, i_rows, g_rows, merge_scale, heads, col0, ms_col0, i_lane0, f_lane0, chunk):
    b, s, _ = proj.shape
    w = heads * HEAD_DIM
    assert (col0 * HEAD_DIM) % w == 0 and (ms_col0 * HEAD_DIM) % w == 0
    blk0 = col0 * HEAD_DIM // w
    grp = lambda base: pl.BlockSpec((1, chunk, w), lambda i, c: (i, c, blk0 + base))
    row_spec = pl.BlockSpec((1, heads, chunk), lambda i, c: (i, 0, c))
    state = lambda r, c: pltpu.VMEM((heads, r, c), F32)
    return pl.pallas_call(
        functools.partial(_mlstm_kernel, heads=heads, i_lane0=i_lane0, f_lane0=f_lane0),
        grid=(b, s // chunk),
        in_specs=[grp(0), grp(1), grp(2), grp(3),
                  pl.BlockSpec(conv_w.shape, lambda i, c: (0, 0)),
                  pl.BlockSpec(conv_b.shape, lambda i, c: (0, 0)),
                  pl.BlockSpec((1, chunk, GATE_LANES), lambda i, c: (i, c, 0)),
                  row_spec, row_spec,
                  pl.BlockSpec((1, w), lambda i, c: (0, ms_col0 * HEAD_DIM // w))],
        out_specs=pl.BlockSpec((1, chunk, w), lambda i, c: (i, c, 0)),
        out_shape=jax.ShapeDtypeStruct((b, s, w), BF16),
        scratch_shapes=[state(HEAD_DIM, HEAD_DIM), state(1, HEAD_DIM), state(1, HEAD_DIM), state(1, HEAD_DIM),
                        pltpu.VMEM((chunk + CONV_HALO, w), F32),
                        pltpu.VMEM((chunk + CONV_HALO, w), F32)],
        compiler_params=_params("arbitrary", "arbitrary"),
        name="mlstm",
    )(proj, proj, proj, proj, conv_w, conv_b, gp, i_rows, g_rows, merge_scale)


def _outproj_kernel(yr_ref, yf_ref, ym_ref, w_ref, x_ref, ga_ref, o_ref):
    r = yr_ref.shape[1]
    f = yf_ref.shape[1]
    y = _dot(yr_ref[...], w_ref[0:r, :]) + _dot(yf_ref[...], w_ref[r:r + f, :])
    y = y + _dot(ym_ref[...], w_ref[r + f:, :])
    o_ref[...] = x_ref[...] + ga_ref[0] * y


def _outproj(y_ret, y_fox, y_m, w_out, x2, g_a, seq):
    m, d = x2.shape
    tm = min(OUTPROJ_TM, seq)
    bpr = seq // tm
    rows = lambda w: pl.BlockSpec((tm, w), lambda i: (i, 0))
    return pl.pallas_call(
        _outproj_kernel,
        grid=(m // tm,),
        in_specs=[rows(y_ret.shape[1]), rows(y_fox.shape[1]), rows(y_m.shape[1]),
                  pl.BlockSpec(w_out.shape, lambda i: (0, 0)),
                  rows(d),
                  pl.BlockSpec((1, 1, d), lambda i: (i // bpr, 0, 0))],
        out_specs=rows(d),
        out_shape=jax.ShapeDtypeStruct((m, d), F32),
        compiler_params=_params("arbitrary"),
        name="outproj",
    )(y_ret, y_fox, y_m, w_out, x2, g_a)


def _mlp_kernel(x_ref, nw_ref, sc_ref, sh_ref, gm_ref, w1_hbm, w2_hbm, fw_ref, o_ref,
                w1_buf, w2_buf, sem, h_scr, *, final_norm, tf, nf):
    i = pl.program_id(0)
    n = pl.num_programs(0)

    def tile_copies(f, slot):
        c0 = pl.multiple_of(f * tf, tf)
        return (pltpu.make_async_copy(w1_hbm.at[:, pl.ds(c0, tf)], w1_buf.at[slot], sem.at[0, slot]),
                pltpu.make_async_copy(w2_hbm.at[pl.ds(c0, tf), :], w2_buf.at[slot], sem.at[1, slot]))

    @pl.when(i == 0)
    def _():
        for cp in tile_copies(0, 0):
            cp.start()

    _modulated_norm_rows(x_ref, nw_ref, sc_ref, sh_ref, h_scr)
    o_ref[...] = jnp.zeros_like(o_ref)

    def body(f, carry):
        slot = f % 2

        @pl.when((f + 1 < nf) | (i + 1 < n))
        def _():
            for cp in tile_copies((f + 1) % nf, 1 - slot):
                cp.start()

        for cp in tile_copies(f, slot):
            cp.wait()
        a = jnp.maximum(_dot(h_scr[...], w1_buf[slot]), 0.0)
        o_ref[...] += _dot((a * a).astype(BF16), w2_buf[slot])
        return carry

    lax.fori_loop(0, nf, body, 0)

    y = x_ref[...] + gm_ref[0] * o_ref[...]
    if final_norm:
        ms = jnp.mean(y * y, axis=-1, keepdims=True)
        y = y * lax.rsqrt(ms + NORM_EPS) * fw_ref[...]
    o_ref[...] = y


def _mlp(x2, nw, sc, sh, g_m, w1, w2, final_w, seq, final_norm):
    m, d = x2.shape
    dff = w1.shape[1]
    tm = min(MLP_TM, seq)
    tf = min(MLP_TF, dff)
    nf = dff // tf
    assert nf % 2 == 0
    bpr = seq // tm
    mod = pl.BlockSpec((1, 1, d), lambda i: (i // bpr, 0, 0))
    vec = pl.BlockSpec((1, d), lambda i: (0, 0))
    hbm = pl.BlockSpec(memory_space=pl.ANY)
    return pl.pallas_call(
        functools.partial(_mlp_kernel, final_norm=final_norm, tf=tf, nf=nf),
        grid=(m // tm,),
        in_specs=[pl.BlockSpec((tm, d), lambda i: (i, 0)),
                  vec, mod, mod, mod, hbm, hbm, vec],
        out_specs=pl.BlockSpec((tm, d), lambda i: (i, 0)),
        out_shape=jax.ShapeDtypeStruct((m, d), F32),
        scratch_shapes=[pltpu.VMEM((2, d, tf), BF16),
                        pltpu.VMEM((2, tf, d), BF16),
                        pltpu.SemaphoreType.DMA((2, 2)),
                        pltpu.VMEM((tm, d), BF16)],
        compiler_params=_params("arbitrary"),
        name="mlp",
    )(x2, nw, sc, sh, g_m, w1, w2, final_w)


def kernel(x, c, positions, ada_w, ada_b, norm_mix_w, norm_mlp_w, w_in, conv_w, conv_b, fox_f_bias,
           mlstm_i_bias, mlstm_f_bias, merge_scale, w_out, w_ff1, w_ff2, final_norm_w):
    b, s, d = x.shape
    depth = ada_w.shape[0]
    ret_w, fox_w, ml_w = d // 4, d // 2, d // 4
    ret_h, fox_h, ml_h = ret_w // HEAD_DIM, fox_w // HEAD_DIM, ml_w // HEAD_DIM
    assert s % 128 == 0 and d % (4 * HEAD_DIM) == 0
    assert all(s % min(t, s) == 0 for t in (ROPE_TS, INPROJ_TM, GATES_TS, RET_CHUNK, MLSTM_CHUNK, FOX_KEYS_TS,
                                             FOX_TQ, OUTPROJ_TM, MLP_TM))
    assert fox_h + 2 * ml_h <= GATE_LANES

    sizes = [ret_w] * 4 + [fox_w] * 3 + [fox_h] + [ml_w] * 4 + [ml_h, ml_h]
    offs = [0]
    for sz in sizes:
        offs.append(offs[-1] + sz)
    n_gate = fox_h + 2 * ml_h
    ret_col0 = 0
    fox_col0 = (4 * ret_w) // HEAD_DIM
    ml_col0 = (4 * ret_w + 3 * fox_w) // HEAD_DIM
    i_lane0, f_lane0 = fox_h, fox_h + ml_h

    ret_chunk = min(RET_CHUNK, s)
    ml_chunk = min(MLSTM_CHUNK, s)
    tq = min(FOX_TQ, s)
    tk = min(FOX_TK, tq)

    mod = _ada_mod(c, ada_w, ada_b)
    cos, sin = _rope_tables(positions)
    cum_mask = jnp.zeros((1, GATE_LANES), F32).at[0, :fox_h].set(1.0).at[0, f_lane0:f_lane0 + ml_h].set(1.0)

    x2 = x.reshape(b * s, d)
    for layer in range(depth):
        sh_a, sc_a, g_a, sh_m, sc_m, g_m = [t.reshape(b, 1, d) for t in jnp.split(mod[layer], N_MOD, axis=-1)]
        wl = w_in[layer]
        w_main = jnp.concatenate([wl[:, offs[0]:offs[7]], wl[:, offs[8]:offs[12]]], axis=1).astype(BF16)
        w_gate = jnp.concatenate([wl[:, offs[7]:offs[8]], wl[:, offs[12]:offs[14]],
                                  jnp.zeros((d, GATE_LANES - n_gate), F32)], axis=1).astype(BF16)
        gate_bias = jnp.pad(jnp.concatenate([fox_f_bias[layer], mlstm_i_bias[layer], mlstm_f_bias[layer]]),
                            (0, GATE_LANES - n_gate)).reshape(1, GATE_LANES)
        ms = merge_scale[layer].reshape(1, d)

        proj, gates = _inproj(x2, norm_mix_w[layer].reshape(1, d), sc_a, sh_a, w_main, w_gate, s)
        proj = proj.reshape(b, s, -1)
        gp = _gates(gates.reshape(b, s, GATE_LANES), gate_bias, cum_mask)
        gp_t = jnp.swapaxes(gp[:, :, :n_gate], 1, 2)
        c_rows = gp_t[:, :fox_h].reshape(b * fox_h, 1, s)
        i_rows = gp_t[:, i_lane0:i_lane0 + ml_h]
        g_rows = gp_t[:, f_lane0:f_lane0 + ml_h]

        y_ret = _retention(proj, cos, sin, ms, ret_h, ret_col0, ret_chunk)
        k_aug = _fox_keys(proj, gp, fox_h, fox_col0 + fox_h, tk)
        y_fox = _fox(proj, k_aug, c_rows, ms, fox_h, fox_col0, ret_h, tq, tk, FOX_UNROLL if tq // tk % FOX_UNROLL == 0 else 1)
        y_m = _mlstm(proj, conv_w[layer], conv_b[layer].reshape(1, -1), gp, i_rows, g_rows, ms,
                     ml_h, ml_col0, ret_h + fox_h, i_lane0, f_lane0, ml_chunk)

        x2 = _outproj(y_ret.reshape(b * s, -1), y_fox.reshape(b * s, -1), y_m.reshape(b * s, -1),
                      w_out[layer].astype(BF16), x2, g_a, s)
        x2 = _mlp(x2, norm_mlp_w[layer].reshape(1, d), sc_m, sh_m, g_m,
                  w_ff1[layer].astype(BF16), w_ff2[layer].astype(BF16),
                  final_norm_w.reshape(1, d), s, layer == depth - 1)
    return x2.reshape(b, s, d)
```
